```python
import math
import jax, jax.numpy as jnp
from jax import lax
import numpy as np

D_MODEL = 1024
BATCH = 8
SEQ = 4096
DEPTH = 1

HEAD_DIM = 64
MIX_WIDTH = D_MODEL
N_HEADS = MIX_WIDTH // HEAD_DIM
N_HEADS_B = N_HEADS // 4
N_HEADS_A = N_HEADS - N_HEADS_B
WIDTH_A = N_HEADS_A * HEAD_DIM
WIDTH_B = N_HEADS_B * HEAD_DIM
DILATED_PATTERNS = ((128, 1), (512, 4), (2048, 16))
DIL_BLOCK = 128
MOBA_BLOCK = 256
MOBA_TOPK = 3
MOBA_QCHUNK = 64
N_MEM = 256
CROSS_HEADS = 4
CROSS_HEAD_DIM = D_MODEL // CROSS_HEADS
D_FF = ((8 * D_MODEL // 3 + 255) // 256) * 256
REL_BUCKETS = 32
REL_MAX_DIST = 2048
NORM_EPS = 1e-6
NEG_INF = -jnp.inf

kernel_name = "hymba_style_dilated_moba_macaron_block"


def rms_norm(x, g):
    xf = x.astype(jnp.float32)
    y = xf * lax.rsqrt(jnp.mean(xf * xf, axis=-1, keepdims=True) + NORM_EPS)
    return (y * g.astype(jnp.float32)).astype(x.dtype)


def swiglu(x, w_gu, w_down):
    gu = x @ w_gu
    g, u = jnp.split(gu, 2, axis=-1)
    return (jax.nn.silu(g) * u) @ w_down


def rel_bucket(dist):
    max_exact = REL_BUCKETS // 2
    n = jnp.maximum(dist, 0)
    nf = jnp.maximum(n, 1).astype(jnp.float32)
    large = max_exact + (jnp.log(nf / max_exact) / math.log(REL_MAX_DIST / max_exact)
                         * (REL_BUCKETS - max_exact)).astype(jnp.int32)
    large = jnp.minimum(large, REL_BUCKETS - 1)
    return jnp.where(n < max_exact, n, large)


def split_heads(a, n_heads, head_dim):
    b, s, _ = a.shape
    return a.reshape(b, s, n_heads, head_dim).transpose(0, 2, 1, 3)


def merge_heads(a):
    b, h, s, e = a.shape
    return a.transpose(0, 2, 1, 3).reshape(b, s, h * e)


def dilated_window_pattern(q, k, v, bias_tab, window, dilation):
    B, H, S, E = q.shape
    n_steps = window // dilation
    L = S // dilation
    nblk = -(-L // DIL_BLOCK)
    Lp = nblk * DIL_BLOCK

    def to_sub(a):
        a = a.reshape(B, H, L, dilation, E).transpose(0, 1, 3, 2, 4)
        a = jnp.pad(a, ((0, 0), (0, 0), (0, 0), (0, Lp - L), (0, 0)))
        return a.reshape(B, H, dilation, nblk, DIL_BLOCK, E)

    def band(a):
        prev = jnp.pad(a, ((0, 0), (0, 0), (0, 0), (1, 0), (0, 0), (0, 0)))[:, :, :, :-1]
        return jnp.concatenate([prev, a], axis=4)

    qs, ks, vs = to_sub(q), to_sub(k), to_sub(v)
    kb, vb = band(ks), band(vs)
    scale = HEAD_DIM ** -0.5
    s = jnp.einsum('bhrnqe,bhrnke->bhrnqk', qs, kb).astype(jnp.float32) * scale
    i = jnp.arange(DIL_BLOCK)[:, None]
    j = jnp.arange(2 * DIL_BLOCK)[None, :]
    delta = DIL_BLOCK + i - j
    bias = bias_tab[:, rel_bucket(delta * dilation)].astype(jnp.float32)
    blk = jnp.arange(nblk)[:, None, None]
    valid = (delta >= 0) & (delta <= n_steps) & ((blk > 0) | (j >= DIL_BLOCK))
    s = jnp.where(valid, s + bias[None, :, None, None], NEG_INF)
    m = jnp.max(s, axis=-1, keepdims=True)
    p = jnp.exp(s - m)
    l = jnp.sum(p, axis=-1, keepdims=True)
    o = jnp.einsum('bhrnqk,bhrnke->bhrnqe', p, vb.astype(jnp.float32)) / l

    def from_sub(a):
        f = a.shape[-1]
        a = a.reshape(B, H, dilation, Lp, f)[:, :, :, :L]
        return a.transpose(0, 1, 3, 2, 4).reshape(B, H, S, f)

    return from_sub(o), from_sub(m), from_sub(l)


def dilated_attention(q, k, v, bias_tab):
    outs = [dilated_window_pattern(q, k, v, bias_tab, w, d) for (w, d) in DILATED_PATTERNS]
    m_all = outs[0][1]
    for _, m_i, _ in outs[1:]:
        m_all = jnp.maximum(m_all, m_i)
    num = 0.0
    den = 0.0
    for o_i, m_i, l_i in outs:
        w_i = l_i * jnp.exp(m_i - m_all)
        num = num + w_i * o_i
        den = den + w_i
    return num / den


def moba_attention(q, k, v, bias_tab):
    B, H, S, E = q.shape
    BS = MOBA_BLOCK
    nb = -(-S // BS)
    Sp = nb * BS
    pad = lambda a: jnp.pad(a, ((0, 0), (0, 0), (0, Sp - S), (0, 0)))
    qp, kp, vp = pad(q), pad(k), pad(v)
    kb = kp.reshape(B, H, nb, BS, E)
    vb = vp.reshape(B, H, nb, BS, E)
    kmean = jnp.mean(kb.astype(jnp.float32), axis=3)
    gate = jnp.einsum('bhse,bhne->bhsn', qp.astype(jnp.float32), kmean)
    qblk = jnp.arange(Sp) // BS
    past = jnp.arange(nb)[None, :] < qblk[:, None]
    gate = jnp.where(past, gate, NEG_INF)
    top = min(MOBA_TOPK, nb)
    _, sel = lax.top_k(gate, top)
    nc = Sp // MOBA_QCHUNK
    chunk = lambda a: jnp.moveaxis(a.reshape(B, H, nc, MOBA_QCHUNK, *a.shape[3:]), 2, 0)
    gather = jax.vmap(jax.vmap(lambda tab, ix: tab[ix]))
    head_idx = jnp.arange(H)[:, None, None, None]
    scale = HEAD_DIM ** -0.5

    def one_chunk(args):
        qc, selc, c = args
        t = c * MOBA_QCHUNK + jnp.arange(MOBA_QCHUNK)
        own = (c * MOBA_QCHUNK) // BS
        k_own = lax.dynamic_index_in_dim(kb, own, axis=2, keepdims=False)
        v_own = lax.dynamic_index_in_dim(vb, own, axis=2, keepdims=False)
        k_sel = gather(kb, selc)
        v_sel = gather(vb, selc)
        s_sel = jnp.einsum('bhqe,bhqjke->bhqjk', qc, k_sel).astype(jnp.float32) * scale
        key_pos = selc[..., None] * BS + jnp.arange(BS)
        s_sel = s_sel + bias_tab[head_idx, rel_bucket(t[:, None, None] - key_pos)].astype(jnp.float32)
        sel_valid = jnp.arange(top)[None, :] < (t // BS)[:, None]
        s_sel = jnp.where(sel_valid[:, :, None], s_sel, NEG_INF)
        s_own = jnp.einsum('bhqe,bhke->bhqk', qc, k_own).astype(jnp.float32) * scale
        dist_own = t[:, None] - (own * BS + jnp.arange(BS))[None, :]
        s_own = s_own + bias_tab[:, rel_bucket(dist_own)].astype(jnp.float32)[None]
        s_own = jnp.where(dist_own >= 0, s_own, NEG_INF)
        logits = jnp.concatenate([s_sel.reshape(B, H, MOBA_QCHUNK, top * BS), s_own], axis=-1)
        p = jax.nn.softmax(logits, axis=-1)
        p_sel = p[..., :top * BS].reshape(B, H, MOBA_QCHUNK, top, BS)
        p_own = p[..., top * BS:]
        return (jnp.einsum('bhqjk,bhqjke->bhqe', p_sel, v_sel.astype(jnp.float32))
                + jnp.einsum('bhqk,bhke->bhqe', p_own, v_own.astype(jnp.float32)))

    o = lax.map(one_chunk, (chunk(qp), chunk(sel), jnp.arange(nc)))
    o = jnp.moveaxis(o, 0, 2).reshape(B, H, Sp, E)[:, :, :S]
    return o


def memory_cross_attention(x, mem, g_x, g_mem, w_cq, w_ckv, w_co):
    B, S, _ = x.shape
    h = rms_norm(x, g_x)
    mn = rms_norm(mem, g_mem)
    q = (h @ w_cq).reshape(B, S, CROSS_HEADS, CROSS_HEAD_DIM)
    kv = mn @ w_ckv
    k, v = jnp.split(kv, 2, axis=-1)
    k = k.reshape(B, mem.shape[1], CROSS_HEADS, CROSS_HEAD_DIM)
    v = v.reshape(B, mem.shape[1], CROSS_HEADS, CROSS_HEAD_DIM)
    s = jnp.einsum('bshe,bmhe->bhsm', q, k).astype(jnp.float32) * CROSS_HEAD_DIM ** -0.5
    p = jax.nn.softmax(s, axis=-1)
    o = jnp.einsum('bhsm,bmhe->bshe', p, v.astype(jnp.float32)).reshape(B, S, D_MODEL)
    return o.astype(x.dtype) @ w_co


def setup_inputs(seed: int = 0) -> dict:
    key = jax.random.key(seed)
    ks = jax.random.split(key, 24)
    f32 = jnp.float32

    def w(k, shape, fan_in):
        return jax.random.normal(k, shape, f32) * fan_in ** -0.5

    def gain(k, shape):
        return 1.0 + 0.05 * jax.random.normal(k, shape, f32)

    L = DEPTH
    return {
        "x": jax.random.normal(ks[0], (BATCH, SEQ, D_MODEL), f32),
        "mem": jax.random.normal(ks[1], (BATCH, N_MEM, D_MODEL), f32),
        "g_ffn1": gain(ks[2], (L, D_MODEL)),
        "w_ffn1_gu": w(ks[3], (L, D_MODEL, 2 * D_FF), D_MODEL),
        "w_ffn1_down": w(ks[4], (L, D_FF, D_MODEL), D_FF),
        "g_mix": gain(ks[5], (L, D_MODEL)),
        "w_in": w(ks[6], (L, D_MODEL, 3 * MIX_WIDTH), D_MODEL),
        "rel_bias": 0.5 * jax.random.normal(ks[7], (REL_BUCKETS, N_HEADS), f32),
        "g_out_a": gain(ks[8], (L, WIDTH_A)),
        "g_out_b": gain(ks[9], (L, WIDTH_B)),
        "w_out": w(ks[10], (L, MIX_WIDTH, D_MODEL), MIX_WIDTH),
        "g_cross": gain(ks[11], (L, D_MODEL)),
        "g_mem": gain(ks[12], (L, D_MODEL)),
        "w_cq": w(ks[13], (L, D_MODEL, D_MODEL), D_MODEL),
        "w_ckv": w(ks[14], (L, D_MODEL, 2 * D_MODEL), D_MODEL),
        "w_co": w(ks[15], (L, D_MODEL, D_MODEL), D_MODEL),
        "g_ffn2": gain(ks[16], (L, D_MODEL)),
        "w_ffn2_gu": w(ks[17], (L, D_MODEL, 2 * D_FF), D_MODEL),
        "w_ffn2_down": w(ks[18], (L, D_FF, D_MODEL), D_FF),
        "g_final": gain(ks[19], (D_MODEL,)),
    }


def reference(x, mem, g_ffn1, w_ffn1_gu, w_ffn1_down, g_mix, w_in, rel_bias, g_out_a, g_out_b,
              w_out, g_cross, g_mem, w_cq, w_ckv, w_co, g_ffn2, w_ffn2_gu, w_ffn2_down, g_final):
    bias_a = rel_bias[:, :N_HEADS_A].T
    bias_b = rel_bias[:, N_HEADS_A:].T
    for layer in range(DEPTH):
        x = x + 0.5 * swiglu(rms_norm(x, g_ffn1[layer]), w_ffn1_gu[layer], w_ffn1_down[layer])
        h = rms_norm(x, g_mix[layer])
        proj = h @ w_in[layer]
        qa, ka, va, qb, kb, vb = jnp.split(
            proj, np.cumsum([WIDTH_A, WIDTH_A, WIDTH_A, WIDTH_B, WIDTH_B]).tolist(), axis=-1)
        oa = dilated_attention(split_heads(qa, N_HEADS_A, HEAD_DIM), split_heads(ka, N_HEADS_A, HEAD_DIM),
                               split_heads(va, N_HEADS_A, HEAD_DIM), bias_a)
        ob = moba_attention(split_heads(qb, N_HEADS_B, HEAD_DIM), split_heads(kb, N_HEADS_B, HEAD_DIM),
                            split_heads(vb, N_HEADS_B, HEAD_DIM), bias_b)
        ya = rms_norm(merge_heads(oa).astype(x.dtype), g_out_a[layer])
        yb = rms_norm(merge_heads(ob).astype(x.dtype), g_out_b[layer])
        x = x + jnp.concatenate([ya, yb], axis=-1) @ w_out[layer]
        x = x + memory_cross_attention(x, mem, g_cross[layer], g_mem[layer], w_cq[layer], w_ckv[layer], w_co[layer])
        x = x + 0.5 * swiglu(rms_norm(x, g_ffn2[layer]), w_ffn2_gu[layer], w_ffn2_down[layer])
    return rms_norm(x, g_final)
```

```python
import functools
import math

import numpy as np
import jax
import jax.numpy as jnp
from jax import lax
from jax.experimental import pallas as pl
from jax.experimental.pallas import tpu as pltpu

F32 = jnp.float32
BF16 = jnp.bfloat16

HEAD_DIM = 64
LANES = 128
N_HEADS_A = 12
N_HEADS_B = 4
DILATED_PATTERNS = ((128, 1), (512, 4), (2048, 16))
DIL_BLOCK = 128
MOBA_BLOCK = 256
MOBA_TOPK = 3
CROSS_HEADS = 4
REL_BUCKETS = 32
REL_MAX_DIST = 2048
NORM_EPS = 1e-6
NEG_INF = -jnp.inf
VMEM_LIMIT = 56 * 1024 * 1024

FFN_CHUNK = 256
TOKEN_TILE = 512


def _rms(x, g):
    xf = x.astype(F32)
    return xf * lax.rsqrt(jnp.mean(xf * xf, axis=-1, keepdims=True) + NORM_EPS) * g


def _dot(a, b):
    return jnp.dot(a, b, preferred_element_type=F32)


def _dot_nt(a, b):
    return lax.dot_general(a, b, (((1,), (1,)), ((), ())), preferred_element_type=F32)


def _resident(shape):
    nd = len(shape)
    return pl.BlockSpec(shape, lambda *_: (0,) * nd, pipeline_mode=pl.Buffered(1))


def _swiglu_half(h, wgu_ref, wd_ref, d_ff):
    acc = None
    for c in range(d_ff // FFN_CHUNK):
        lo = c * FFN_CHUNK
        g = _dot(h, wgu_ref[:, lo:lo + FFN_CHUNK])
        u = _dot(h, wgu_ref[:, d_ff + lo:d_ff + lo + FFN_CHUNK])
        a = (g * (1.0 / (1.0 + jnp.exp(-g))) * u).astype(BF16)
        part = _dot(a, wd_ref[lo:lo + FFN_CHUNK, :])
        acc = part if acc is None else acc + part
    return acc


def _memkv_kernel(mem_ref, g_ref, w_ref, kv_ref):
    h = _rms(mem_ref[0], g_ref[...]).astype(BF16)
    kv_ref[0] = _dot(h, w_ref[...]).astype(BF16)


def _memkv(mem, g_mem, w_ckv):
    B, M, D = mem.shape
    N = w_ckv.shape[1]
    return pl.pallas_call(
        _memkv_kernel,
        grid=(B,),
        in_specs=[pl.BlockSpec((1, M, D), lambda b: (b, 0, 0)),
                  _resident((1, D)), _resident((D, N))],
        out_specs=pl.BlockSpec((1, M, N), lambda b: (b, 0, 0)),
        out_shape=jax.ShapeDtypeStruct((B, M, N), BF16),
        compiler_params=pltpu.CompilerParams(dimension_semantics=("arbitrary",),
                                             vmem_limit_bytes=VMEM_LIMIT),
        name="memkv",
    )(mem, g_mem.reshape(1, D), w_ckv)


def _ffn1_kernel(x_ref, g_ref, wgu_ref, wd_ref, o_ref, *, d_ff):
    x = x_ref[...]
    h = _rms(x, g_ref[...]).astype(BF16)
    o_ref[...] = x + 0.5 * _swiglu_half(h, wgu_ref, wd_ref, d_ff)


def _ffn1(x2d, g, wgu, wd):
    T, D = x2d.shape
    d_ff = wd.shape[0]
    tm = TOKEN_TILE
    return pl.pallas_call(
        functools.partial(_ffn1_kernel, d_ff=d_ff),
        grid=(T // tm,),
        in_specs=[pl.BlockSpec((tm, D), lambda i: (i, 0)),
                  _resident((1, D)), _resident((D, 2 * d_ff)), _resident((d_ff, D))],
        out_specs=pl.BlockSpec((tm, D), lambda i: (i, 0)),
        out_shape=jax.ShapeDtypeStruct((T, D), F32),
        compiler_params=pltpu.CompilerParams(dimension_semantics=("arbitrary",),
                                             vmem_limit_bytes=VMEM_LIMIT),
        name="ffn1",
    )(x2d, g.reshape(1, D), wgu, wd)


def _proj_kernel(x_ref, g_ref, w_ref, a1_ref, a4_ref, a16_ref, b_ref, scr, *, wa, wqa, wqb):
    tm = x_ref.shape[1]
    h = _rms(x_ref[0], g_ref[...]).astype(BF16)
    scale = HEAD_DIM ** -0.5
    pb = _dot(h, w_ref[:, wa:])
    b_ref[0, :, :wqb] = (pb[:, :wqb] * scale).astype(BF16)
    b_ref[0, :, wqb:] = pb[:, wqb:].astype(BF16)
    pa = _dot(h, w_ref[:, :wa])
    for c in range(wa // LANES):
        cols = slice(c * LANES, (c + 1) * LANES)
        tile = pa[:, cols] * scale if c < wqa // LANES else pa[:, cols]
        a1_ref[0, :, cols] = tile.astype(BF16)
        scr[c] = tile
        for r in range(4):
            a4_ref[0, r, :, cols] = scr[c, pl.ds(r, tm // 4, stride=4), :].astype(BF16)
        for r in range(16):
            a16_ref[0, r, :, cols] = scr[c, pl.ds(r, tm // 16, stride=16), :].astype(BF16)


def _proj(x1, g, w_in):
    B, S, D = x1.shape
    wa = 3 * N_HEADS_A * HEAD_DIM
    wb = 3 * N_HEADS_B * HEAD_DIM
    tm = TOKEN_TILE
    kern = functools.partial(_proj_kernel, wa=wa, wqa=N_HEADS_A * HEAD_DIM, wqb=N_HEADS_B * HEAD_DIM)
    return pl.pallas_call(
        kern,
        grid=(B, S // tm),
        in_specs=[pl.BlockSpec((1, tm, D), lambda b, i: (b, i, 0)),
                  _resident((1, D)), _resident((D, wa + wb))],
        out_specs=[pl.BlockSpec((1, tm, wa), lambda b, i: (b, i, 0)),
                   pl.BlockSpec((1, 4, tm // 4, wa), lambda b, i: (b, 0, i, 0)),
                   pl.BlockSpec((1, 16, tm // 16, wa), lambda b, i: (b, 0, i, 0)),
                   pl.BlockSpec((1, tm, wb), lambda b, i: (b, i, 0))],
        out_shape=[jax.ShapeDtypeStruct((B, S, wa), BF16),
                   jax.ShapeDtypeStruct((B, 4, S // 4, wa), BF16),
                   jax.ShapeDtypeStruct((B, 16, S // 16, wa), BF16),
                   jax.ShapeDtypeStruct((B, S, wb), BF16)],
        scratch_shapes=[pltpu.VMEM((wa // LANES, tm, LANES), F32)],
        compiler_params=pltpu.CompilerParams(dimension_semantics=("arbitrary", "arbitrary"),
                                             vmem_limit_bytes=VMEM_LIMIT),
        name="proj",
    )(x1, g.reshape(1, D), w_in)


def _rel_bucket(dist):
    max_exact = REL_BUCKETS // 2
    n = jnp.maximum(dist, 0)
    nf = jnp.maximum(n, 1).astype(F32)
    large = max_exact + (jnp.log(nf / max_exact) / math.log(REL_MAX_DIST / max_exact)
                         * (REL_BUCKETS - max_exact)).astype(jnp.int32)
    large = jnp.minimum(large, REL_BUCKETS - 1)
    return jnp.where(n < max_exact, n, large)


def _dilated_tables(rel_bias):
    qb = DIL_BLOCK
    i = jnp.arange(qb)[:, None]
    j = jnp.arange(2 * qb)[None, :]
    per_pattern = []
    for window, dil in DILATED_PATTERNS:
        n_steps = window // dil
        d_band = qb + i - j
        ok_band = (d_band >= 0) & (d_band <= n_steps)
        d_own = i - j
        ok_own = (d_own >= 0) & (d_own <= n_steps)
        variants = []
        for delta, ok in ((d_band, ok_band), (d_band, ok_band & (j >= qb)), (d_own, ok_own)):
            bias = rel_bias[_rel_bucket(delta * dil), :N_HEADS_A]
            variants.append(jnp.where(ok[:, :, None], bias, NEG_INF))
        per_pattern.append(jnp.stack(variants))
    t = jnp.stack(per_pattern)
    t = t.transpose(4, 0, 1, 2, 3).reshape(N_HEADS_A // 2, 2, len(DILATED_PATTERNS), 3, qb, 2 * qb)
    return t.transpose(0, 2, 1, 3, 4, 5).astype(F32)


def _moba_num_tables():
    return -(-(REL_MAX_DIST + MOBA_BLOCK - 1) // MOBA_BLOCK) + 1


def _moba_tables(rel_bias):
    bs = MOBA_BLOCK
    i = jnp.arange(bs)[:, None]
    j = jnp.arange(bs)[None, :]
    tabs = []
    for off in range(_moba_num_tables()):
        n = off * bs + i - j
        bias = rel_bias[_rel_bucket(n), N_HEADS_A:]
        tabs.append(jnp.where((n >= 0)[:, :, None], bias, NEG_INF))
    t = jnp.stack(tabs)
    return t.transpose(3, 0, 1, 2).reshape(N_HEADS_B // 2, 2, len(tabs), bs, bs).astype(F32)


def _head_lane_masks(shape):
    lane = lax.broadcasted_iota(jnp.int32, shape, len(shape) - 1)
    return lane < HEAD_DIM, lane >= HEAD_DIM


_DEN_LANE = (HEAD_DIM, 1)
_MAX_LANE = (LANES - 1, 0)


def _dil_kernel(q1, k1, v1, q4, k4, v4, q16, k16, v16, tab_ref, o_ref, vext, oscr, *, seq):
    qb = DIL_BLOCK
    nblk = seq // qb
    qrefs, krefs, vrefs = (q1, q4, q16), (k1, k4, k16), (v1, v4, v16)
    dils = tuple(d for _, d in DILATED_PATTERNS)

    for p in range(3):
        v = vrefs[p][0]
        m0, m1 = _head_lane_masks(v.shape)
        one = jnp.ones_like(v)
        vext[p, 0] = jnp.where(m0, v, one)
        vext[p, 1] = jnp.where(m1, v, one)

    lane_q = lax.broadcasted_iota(jnp.int32, (qb, LANES), 1)

    def block(blk, carry):
        q0 = pl.multiple_of(blk * qb, qb)
        k0 = pl.multiple_of(jnp.maximum(blk - 1, 0) * qb, qb)
        for p in range(3):
            per = nblk // dils[p]
            variant = jnp.where(blk == 0, 2, jnp.where(blk % per == 0, 1, 0))
            q = qrefs[p][0, pl.ds(q0, qb), :]
            kband = krefs[p][0, pl.ds(k0, 2 * qb), :]
            for h in range(2):
                qh = jnp.where((lane_q < HEAD_DIM) if h == 0 else (lane_q >= HEAD_DIM), q, jnp.zeros_like(q))
                s = _dot_nt(qh, kband) + tab_ref[0, p, h, variant]
                m = jnp.max(s, axis=-1, keepdims=True)
                pr = jnp.exp(s - m).astype(BF16)
                pv = _dot(pr, vext[p, h, pl.ds(k0, 2 * qb), :])
                oscr[p, h, pl.ds(q0, qb), :] = jnp.where(lane_q == _MAX_LANE[h], m, pv)
        return carry

    lax.fori_loop(0, nblk, block, 0)

    n16 = seq // 16
    n4 = seq // 4
    mc_rows = 128

    def merge(r, carry):
        for mc in range(n16 // mc_rows):
            m_lo = mc * mc_rows
            row16 = pl.multiple_of(r * n16 + m_lo, mc_rows)
            row4 = (r % 4) * n4 + r // 4 + 4 * m_lo
            row1 = r + 16 * m_lo
            res = []
            for h in range(2):
                e = (oscr[0, h, pl.ds(row1, mc_rows, stride=16), :],
                     oscr[1, h, pl.ds(row4, mc_rows, stride=4), :],
                     oscr[2, h, pl.ds(row16, mc_rows), :])
                ml = _MAX_LANE[h]
                ms = [x[:, ml:ml + 1] for x in e]
                m_all = jnp.maximum(jnp.maximum(ms[0], ms[1]), ms[2])
                acc = e[0] * jnp.exp(ms[0] - m_all)
                acc = acc + e[1] * jnp.exp(ms[1] - m_all)
                acc = acc + e[2] * jnp.exp(ms[2] - m_all)
                dl = _DEN_LANE[h]
                res.append(acc / acc[:, dl:dl + 1])
            lane = lax.broadcasted_iota(jnp.int32, res[0].shape, 1)
            o_ref[0, pl.ds(row1, mc_rows, stride=16), :] = jnp.where(lane < HEAD_DIM, res[0], res[1])
        return carry

    lax.fori_loop(0, 16, merge, 0)


def _dilated(a1, a4, a16, tabs):
    B, S, wa = a1.shape
    npair = N_HEADS_A // 2
    nq = npair

    def col(off):
        return pl.BlockSpec((1, S, LANES), lambda b, p, off=off: (b, 0, off + p))

    in_specs = []
    for _ in range(3):
        in_specs += [col(0), col(nq), col(2 * nq)]
    in_specs.append(pl.BlockSpec((1,) + tabs.shape[1:], lambda b, p: (p, 0, 0, 0, 0, 0)))
    return pl.pallas_call(
        functools.partial(_dil_kernel, seq=S),
        grid=(B, npair),
        in_specs=in_specs,
        out_specs=pl.BlockSpec((1, S, LANES), lambda b, p: (b, 0, p)),
        out_shape=jax.ShapeDtypeStruct((B, S, npair * LANES), F32),
        scratch_shapes=[pltpu.VMEM((3, 2, S, LANES), BF16), pltpu.VMEM((3, 2, S, LANES), F32)],
        compiler_params=pltpu.CompilerParams(dimension_semantics=("arbitrary", "arbitrary"),
                                             vmem_limit_bytes=VMEM_LIMIT),
        name="dilated",
    )(a1, a1, a1, a4, a4, a4, a16, a16, a16, tabs)


def _moba_kernel(q_ref, k_ref, v_ref, tab_ref, o_ref, vext, acc, selb, *, seq):
    bs = MOBA_BLOCK
    nb = seq // bs
    ntab = tab_ref.shape[2]

    v = v_ref[0]
    m0, m1 = _head_lane_masks(v.shape)
    one = jnp.ones_like(v)
    vext[0] = jnp.where(m0, v, one)
    vext[1] = jnp.where(m1, v, one)

    kmean = jnp.sum(k_ref[0].astype(F32).reshape(nb, bs, LANES), axis=1) * (1.0 / bs)
    kmean = jnp.concatenate([kmean, jnp.zeros((LANES - nb, LANES), F32)], axis=0)
    km_hi = kmean.astype(BF16)
    km_lo = (kmean - km_hi.astype(F32)).astype(BF16)

    lane_b = lax.broadcasted_iota(jnp.int32, (bs, LANES), 1)
    head_mask = (lane_b < HEAD_DIM, lane_b >= HEAD_DIM)

    def masked_q(i0, h):
        q = q_ref[0, pl.ds(i0, bs), :]
        return jnp.where(head_mask[h], q, jnp.zeros_like(q))

    def select_and_diag(i, carry):
        i0 = pl.multiple_of(i * bs, bs)
        for h in range(2):
            qh = masked_q(i0, h)
            gate = _dot_nt(qh, km_hi) + _dot_nt(qh, km_lo)
            past = lane_b < i
            g = jnp.where(past, gate, NEG_INF)
            sel = jnp.zeros(g.shape, jnp.bool_)
            for _ in range(MOBA_TOPK):
                mx = jnp.max(g, axis=-1, keepdims=True)
                first = jnp.min(jnp.where(g == mx, lane_b, LANES), axis=-1, keepdims=True)
                pick = lane_b == first
                sel = sel | (pick & past)
                g = jnp.where(pick, NEG_INF, g)
            selb[h, pl.ds(i0, bs), :] = jnp.where(sel, 0.0, NEG_INF)
            s = _dot_nt(qh, k_ref[0, pl.ds(i0, bs), :]) + tab_ref[0, h, 0]
            m = jnp.max(s, axis=-1, keepdims=True)
            pr = jnp.exp(s - m).astype(BF16)
            pv = _dot(pr, vext[h, pl.ds(i0, bs), :])
            acc[h, pl.ds(i0, bs), :] = jnp.where(lane_b == _MAX_LANE[h], m, pv)
        return carry

    lax.fori_loop(0, nb, select_and_diag, 0)

    for j in range(nb - 1):
        j0 = j * bs
        kj = k_ref[0, j0:j0 + bs, :]

        def fold(i, carry, j=j, j0=j0, kj=kj):
            i0 = pl.multiple_of(i * bs, bs)
            t = jnp.minimum(i - j, ntab - 1)
            for h in range(2):
                qh = masked_q(i0, h)
                s = _dot_nt(qh, kj) + tab_ref[0, h, t] + selb[h, pl.ds(i0, bs), j:j + 1]
                old = acc[h, pl.ds(i0, bs), :]
                ml = _MAX_LANE[h]
                m_old = old[:, ml:ml + 1]
                m_new = jnp.maximum(m_old, jnp.max(s, axis=-1, keepdims=True))
                pr = jnp.exp(s - m_new).astype(BF16)
                new = old * jnp.exp(m_old - m_new) + _dot(pr, vext[h, j0:j0 + bs, :])
                acc[h, pl.ds(i0, bs), :] = jnp.where(lane_b == ml, m_new, new)
            return carry

        lax.fori_loop(j + 1, nb, fold, 0)

    def finish(i, carry):
        i0 = pl.multiple_of(i * bs, bs)
        a0 = acc[0, pl.ds(i0, bs), :]
        a1 = acc[1, pl.ds(i0, bs), :]
        r0 = a0 / a0[:, _DEN_LANE[0]:_DEN_LANE[0] + 1]
        r1 = a1 / a1[:, _DEN_LANE[1]:_DEN_LANE[1] + 1]
        o_ref[0, pl.ds(i0, bs), :] = jnp.where(lane_b < HEAD_DIM, r0, r1)
        return carry

    lax.fori_loop(0, nb, finish, 0)


def _moba(b1, tabs):
    B, S, wb = b1.shape
    npair = N_HEADS_B // 2

    def col(off):
        return pl.BlockSpec((1, S, LANES), lambda b, p, off=off: (b, 0, off + p))

    return pl.pallas_call(
        functools.partial(_moba_kernel, seq=S),
        grid=(B, npair),
        in_specs=[col(0), col(npair), col(2 * npair),
                  pl.BlockSpec((1,) + tabs.shape[1:], lambda b, p: (p, 0, 0, 0, 0))],
        out_specs=pl.BlockSpec((1, S, LANES), lambda b, p: (b, 0, p)),
        out_shape=jax.ShapeDtypeStruct((B, S, npair * LANES), F32),
        scratch_shapes=[pltpu.VMEM((2, S, LANES), BF16), pltpu.VMEM((2, S, LANES), F32),
                        pltpu.VMEM((2, S, LANES), F32)],
        compiler_params=pltpu.CompilerParams(dimension_semantics=("arbitrary", "arbitrary"),
                                             vmem_limit_bytes=VMEM_LIMIT),
        name="moba",
    )(b1, b1, b1, tabs)


def _post_kernel(x_ref, oa_ref, ob_ref, kv_ref, ga_ref, gb_ref, wout_ref, gc_ref, wcq_ref, wco_ref,
                 gf2_ref, wgu_ref, wd_ref, gfin_ref, y_ref, *, d_ff, final_norm):
    wa = oa_ref.shape[2]
    d = x_ref.shape[2]
    ya = _rms(oa_ref[0], ga_ref[...]).astype(BF16)
    yb = _rms(ob_ref[0], gb_ref[...]).astype(BF16)
    x = x_ref[0] + _dot(ya, wout_ref[:wa, :]) + _dot(yb, wout_ref[wa:, :])

    h = _rms(x, gc_ref[...]).astype(BF16)
    q = _dot(h, wcq_ref[...])
    dh = d // CROSS_HEADS
    scale = dh ** -0.5
    cross = None
    for c in range(CROSS_HEADS):
        qc = (q[:, c * dh:(c + 1) * dh] * scale).astype(BF16)
        s = _dot_nt(qc, kv_ref[0, :, c * dh:(c + 1) * dh])
        s = s - jnp.max(s, axis=-1, keepdims=True)
        e = jnp.exp(s)
        p = e / jnp.sum(e, axis=-1, keepdims=True)
        oc = _dot(p.astype(BF16), kv_ref[0, :, d + c * dh:d + (c + 1) * dh]).astype(BF16)
        part = _dot(oc, wco_ref[c * dh:(c + 1) * dh, :])
        cross = part if cross is None else cross + part
    x = x + cross

    h = _rms(x, gf2_ref[...]).astype(BF16)
    x = x + 0.5 * _swiglu_half(h, wgu_ref, wd_ref, d_ff)
    y_ref[0] = _rms(x, gfin_ref[...]) if final_norm else x


def _post(x1, oa, ob, kv, g_out_a, g_out_b, w_out, g_cross, w_cq, w_co, g_ffn2, wgu, wd, g_final, final_norm):
    B, S, D = x1.shape
    wa, wb = oa.shape[2], ob.shape[2]
    M = kv.shape[1]
    d_ff = wd.shape[0]
    tm = TOKEN_TILE
    tok = lambda w: pl.BlockSpec((1, tm, w), lambda b, i: (b, i, 0))
    return pl.pallas_call(
        functools.partial(_post_kernel, d_ff=d_ff, final_norm=final_norm),
        grid=(B, S // tm),
        in_specs=[tok(D), tok(wa), tok(wb),
                  pl.BlockSpec((1, M, 2 * D), lambda b, i: (b, 0, 0)),
                  _resident((1, wa)), _resident((1, wb)), _resident((wa + wb, D)),
                  _resident((1, D)), _resident((D, D)), _resident((D, D)),
                  _resident((1, D)), _resident((D, 2 * d_ff)), _resident((d_ff, D)),
                  _resident((1, D))],
        out_specs=tok(D),
        out_shape=jax.ShapeDtypeStruct((B, S, D), F32),
        compiler_params=pltpu.CompilerParams(dimension_semantics=("arbitrary", "arbitrary"),
                                             vmem_limit_bytes=VMEM_LIMIT),
        name="post",
    )(x1, oa, ob, kv, g_out_a.reshape(1, wa), g_out_b.reshape(1, wb), w_out, g_cross.reshape(1, D),
      w_cq, w_co, g_ffn2.reshape(1, D), wgu, wd, g_final.reshape(1, D))


def kernel(x, mem, g_ffn1, w_ffn1_gu, w_ffn1_down, g_mix, w_in, rel_bias, g_out_a, g_out_b, w_out,
           g_cross, g_mem, w_cq, w_ckv, w_co, g_ffn2, w_ffn2_gu, w_ffn2_down, g_final):
    B, S, D = x.shape
    depth = g_ffn1.shape[0]
    dil_tabs = _dilated_tables(rel_bias)
    moba_tabs = _moba_tables(rel_bias)
    bf = lambda w: w.astype(BF16)
    y = None
    for l in range(depth):
        kv = _memkv(mem, g_mem[l], bf(w_ckv[l]))
        x1 = _ffn1(x.reshape(B * S, D), g_ffn1[l], bf(w_ffn1_gu[l]), bf(w_ffn1_down[l])).reshape(B, S, D)
        a1, a4, a16, b1 = _proj(x1, g_mix[l], bf(w_in[l]))
        wa = a1.shape[2]
        oa = _dilated(a1, a4.reshape(B, S, wa), a16.reshape(B, S, wa), dil_tabs)
        ob = _moba(b1, moba_tabs)
        y = _post(x1, oa, ob, kv, g_out_a[l], g_out_b[l], bf(w_out[l]), g_cross[l], bf(w_cq[l]),
                  bf(w_co[l]), g_ffn2[l], bf(w_ffn2_gu[l]), bf(w_ffn2_down[l]), g_final,
                  final_norm=(l == depth - 1))
        x = y
    return y
```

```python
import functools
import math

import numpy as np
import jax
import jax.numpy as jnp
from jax import lax
from jax.experimental import pallas as pl
from jax.experimental.pallas import tpu as pltpu

F32 = jnp.float32
BF16 = jnp.bfloat16

HEAD_DIM = 64
LANES = 128
N_HEADS_A = 12
N_HEADS_B = 4
DILATED_PATTERNS = ((128, 1), (512, 4), (2048, 16))
DIL_BLOCK = 128
MOBA_BLOCK = 256
MOBA_TOPK = 3
CROSS_HEADS = 4
REL_BUCKETS = 32
REL_MAX_DIST = 2048
NORM_EPS = 1e-6
NEG_INF = -jnp.inf
VMEM_LIMIT = 56 * 1024 * 1024

FFN_CHUNK = 256
TOKEN_TILE = 512


def _rms(x, g):
    xf = x.astype(F32)
    return xf * lax.rsqrt(jnp.mean(xf * xf, axis=-1, keepdims=True) + NORM_EPS) * g


def _dot(a, b):
    return jnp.dot(a, b, preferred_element_type=F32)


def _dot_nt(a, b):
    return lax.dot_general(a, b, (((1,), (1,)), ((), ())), preferred_element_type=F32)


def _resident(shape):
    nd = len(shape)
    return pl.BlockSpec(shape, lambda *_: (0,) * nd, pipeline_mode=pl.Buffered(1))


def _swiglu_half(h, wgu_ref, wd_ref, d_ff):
    acc = None
    for c in range(d_ff // FFN_CHUNK):
        lo = c * FFN_CHUNK
        g = _dot(h, wgu_ref[:, lo:lo + FFN_CHUNK])
        u = _dot(h, wgu_ref[:, d_ff + lo:d_ff + lo + FFN_CHUNK])
        a = (g * (1.0 / (1.0 + jnp.exp(-g))) * u).astype(BF16)
        part = _dot(a, wd_ref[lo:lo + FFN_CHUNK, :])
        acc = part if acc is None else acc + part
    return acc


def _memkv_kernel(mem_ref, g_ref, w_ref, kv_ref):
    h = _rms(mem_ref[0], g_ref[...]).astype(BF16)
    kv_ref[0] = _dot(h, w_ref[...]).astype(BF16)


def _memkv(mem, g_mem, w_ckv):
    B, M, D = mem.shape
    N = w_ckv.shape[1]
    return pl.pallas_call(
        _memkv_kernel,
        grid=(B,),
        in_specs=[pl.BlockSpec((1, M, D), lambda b: (b, 0, 0)),
                  _resident((1, D)), _resident((D, N))],
        out_specs=pl.BlockSpec((1, M, N), lambda b: (b, 0, 0)),
        out_shape=jax.ShapeDtypeStruct((B, M, N), BF16),
        compiler_params=pltpu.CompilerParams(dimension_semantics=("arbitrary",),
                                             vmem_limit_bytes=VMEM_LIMIT),
        name="memkv",
    )(mem, g_mem.reshape(1, D), w_ckv)


def _ffn1_kernel(x_ref, g_ref, wgu_ref, wd_ref, o_ref, *, d_ff):
    x = x_ref[...]
    h = _rms(x, g_ref[...]).astype(BF16)
    o_ref[...] = x + 0.5 * _swiglu_half(h, wgu_ref, wd_ref, d_ff)


def _ffn1(x2d, g, wgu, wd):
    T, D = x2d.shape
    d_ff = wd.shape[0]
    tm = TOKEN_TILE
    return pl.pallas_call(
        functools.partial(_ffn1_kernel, d_ff=d_ff),
        grid=(T // tm,),
        in_specs=[pl.BlockSpec((tm, D), lambda i: (i, 0)),
                  _resident((1, D)), _resident((D, 2 * d_ff)), _resident((d_ff, D))],
        out_specs=pl.BlockSpec((tm, D), lambda i: (i, 0)),
        out_shape=jax.ShapeDtypeStruct((T, D), F32),
        compiler_params=pltpu.CompilerParams(dimension_semantics=("arbitrary",),
                                             vmem_limit_bytes=VMEM_LIMIT),
        name="ffn1",
    )(x2d, g.reshape(1, D), wgu, wd)


def _proj_kernel(x_ref, g_ref, w_ref, a1_ref, a4_ref, a16_ref, b_ref, scr, *, wa, wqa, wqb):
    tm = x_ref.shape[1]
    h = _rms(x_ref[0], g_ref[...]).astype(BF16)
    scale = HEAD_DIM ** -0.5
    pb = _dot(h, w_ref[:, wa:])
    b_ref[0, :, :wqb] = (pb[:, :wqb] * scale).astype(BF16)
    b_ref[0, :, wqb:] = pb[:, wqb:].astype(BF16)
    pa = _dot(h, w_ref[:, :wa])
    for c in range(wa // LANES):
        cols = slice(c * LANES, (c + 1) * LANES)
        tile = pa[:, cols] * scale if c < wqa // LANES else pa[:, cols]
        a1_ref[0, :, cols] = tile.astype(BF16)
        scr[c] = tile
        for r in range(4):
            a4_ref[0, r, :, cols] = scr[c, pl.ds(r, tm // 4, stride=4), :].astype(BF16)
        for r in range(16):
            a16_ref[0, r, :, cols] = scr[c, pl.ds(r, tm // 16, stride=16), :].astype(BF16)


def _proj(x1, g, w_in):
    B, S, D = x1.shape
    wa = 3 * N_HEADS_A * HEAD_DIM
    wb = 3 * N_HEADS_B * HEAD_DIM
    tm = TOKEN_TILE
    kern = functools.partial(_proj_kernel, wa=wa, wqa=N_HEADS_A * HEAD_DIM, wqb=N_HEADS_B * HEAD_DIM)
    return pl.pallas_call(
        kern,
        grid=(B, S // tm),
        in_specs=[pl.BlockSpec((1, tm, D), lambda b, i: (b, i, 0)),
                  _resident((1, D)), _resident((D, wa + wb))],
        out_specs=[pl.BlockSpec((1, tm, wa), lambda b, i: (b, i, 0)),
                   pl.BlockSpec((1, 4, tm // 4, wa), lambda b, i: (b, 0, i, 0)),
                   pl.BlockSpec((1, 16, tm // 16, wa), lambda b, i: (b, 0, i, 0)),
                   pl.BlockSpec((1, tm, wb), lambda b, i: (b, i, 0))],
        out_shape=[jax.ShapeDtypeStruct((B, S, wa), BF16),
                   jax.ShapeDtypeStruct((B, 4, S // 4, wa), BF16),
                   jax.ShapeDtypeStruct((B, 16, S // 16, wa), BF16),
                   jax.ShapeDtypeStruct((B, S, wb), BF16)],
        scratch_shapes=[pltpu.VMEM((wa // LANES, tm, LANES), F32)],
        compiler_params=pltpu.CompilerParams(dimension_semantics=("arbitrary", "arbitrary"),
                                             vmem_limit_bytes=VMEM_LIMIT),
        name="proj",
    )(x1, g.reshape(1, D), w_in)


def _rel_bucket(dist):
    max_exact = REL_BUCKETS // 2
    n = jnp.maximum(dist, 0)
    nf = jnp.maximum(n, 1).astype(F32)
    large = max_exact + (jnp.log(nf / max_exact) / math.log(REL_MAX_DIST / max_exact)
                         * (REL_BUCKETS - max_exact)).astype(jnp.int32)
    large = jnp.minimum(large, REL_BUCKETS - 1)
    return jnp.where(n < max_exact, n, large)


def _bias_lookup(dist, bias_cols):
    onehot = (_rel_bucket(dist)[..., None] == jnp.arange(REL_BUCKETS)).astype(F32)
    return jnp.einsum("...b,bh->...h", onehot, bias_cols.astype(F32), precision=lax.Precision.HIGHEST)


def _dilated_tables(rel_bias):
    qb = DIL_BLOCK
    i = jnp.arange(qb)[:, None]
    j = jnp.arange(2 * qb)[None, :]
    per_pattern = []
    for window, dil in DILATED_PATTERNS:
        n_steps = window // dil
        d_band = qb + i - j
        ok_band = (d_band >= 0) & (d_band <= n_steps)
        d_own = i - j
        ok_own = (d_own >= 0) & (d_own <= n_steps)
        variants = []
        for delta, ok in ((d_band, ok_band), (d_band, ok_band & (j >= qb)), (d_own, ok_own)):
            bias = _bias_lookup(delta * dil, rel_bias[:, :N_HEADS_A])
            variants.append(jnp.where(ok[:, :, None], bias, NEG_INF))
        per_pattern.append(jnp.stack(variants))
    t = jnp.stack(per_pattern)
    t = t.transpose(4, 0, 1, 2, 3).reshape(N_HEADS_A // 2, 2, len(DILATED_PATTERNS), 3, qb, 2 * qb)
    return t.transpose(0, 2, 1, 3, 4, 5).astype(F32)


def _moba_num_tables():
    return -(-(REL_MAX_DIST + MOBA_BLOCK - 1) // MOBA_BLOCK) + 1


def _moba_tables(rel_bias):
    bs = MOBA_BLOCK
    i = jnp.arange(bs)[:, None]
    j = jnp.arange(bs)[None, :]
    tabs = []
    for off in range(_moba_num_tables()):
        n = off * bs + i - j
        bias = _bias_lookup(n, rel_bias[:, N_HEADS_A:])
        tabs.append(jnp.where((n >= 0)[:, :, None], bias, NEG_INF))
    t = jnp.stack(tabs)
    return t.transpose(3, 0, 1, 2).reshape(N_HEADS_B // 2, 2, len(tabs), bs, bs).astype(F32)


def _head_lane_masks(shape):
    lane = lax.broadcasted_iota(jnp.int32, shape, len(shape) - 1)
    return lane < HEAD_DIM, lane >= HEAD_DIM


_DEN_LANE = (HEAD_DIM, 1)
_MAX_LANE = (LANES - 1, 0)


def _dil_kernel(q1, k1, v1, q4, k4, v4, q16, k16, v16, tab_ref, o_ref, vext, oscr, *, seq):
    qb = DIL_BLOCK
    nblk = seq // qb
    qrefs, krefs, vrefs = (q1, q4, q16), (k1, k4, k16), (v1, v4, v16)
    dils = tuple(d for _, d in DILATED_PATTERNS)

    for p in range(3):
        v = vrefs[p][0]
        m0, m1 = _head_lane_masks(v.shape)
        one = jnp.ones_like(v)
        vext[p, 0] = jnp.where(m0, v, one)
        vext[p, 1] = jnp.where(m1, v, one)

    lane_q = lax.broadcasted_iota(jnp.int32, (qb, LANES), 1)

    def block(blk, carry):
        q0 = pl.multiple_of(blk * qb, qb)
        k0 = pl.multiple_of(jnp.maximum(blk - 1, 0) * qb, qb)
        for p in range(3):
            per = nblk // dils[p]
            variant = jnp.where(blk == 0, 2, jnp.where(blk % per == 0, 1, 0))
            q = qrefs[p][0, pl.ds(q0, qb), :]
            kband = krefs[p][0, pl.ds(k0, 2 * qb), :]
            for h in range(2):
                qh = jnp.where((lane_q < HEAD_DIM) if h == 0 else (lane_q >= HEAD_DIM), q, jnp.zeros_like(q))
                s = _dot_nt(qh, kband) + tab_ref[0, p, h, variant]
                m = jnp.max(s, axis=-1, keepdims=True)
                pr = jnp.exp(s - m).astype(BF16)
                pv = _dot(pr, vext[p, h, pl.ds(k0, 2 * qb), :])
                oscr[p, h, pl.ds(q0, qb), :] = jnp.where(lane_q == _MAX_LANE[h], m, pv)
        return carry

    lax.fori_loop(0, nblk, block, 0)

    n16 = seq // 16
    n4 = seq // 4
    mc_rows = 128

    def merge(r, carry):
        for mc in range(n16 // mc_rows):
            m_lo = mc * mc_rows
            row16 = pl.multiple_of(r * n16 + m_lo, mc_rows)
            row4 = (r % 4) * n4 + r // 4 + 4 * m_lo
            row1 = r + 16 * m_lo
            res = []
            for h in range(2):
                e = (oscr[0, h, pl.ds(row1, mc_rows, stride=16), :],
                     oscr[1, h, pl.ds(row4, mc_rows, stride=4), :],
                     oscr[2, h, pl.ds(row16, mc_rows), :])
                ml = _MAX_LANE[h]
                ms = [x[:, ml:ml + 1] for x in e]
                m_all = jnp.maximum(jnp.maximum(ms[0], ms[1]), ms[2])
                acc = e[0] * jnp.exp(ms[0] - m_all)
                acc = acc + e[1] * jnp.exp(ms[1] - m_all)
                acc = acc + e[2] * jnp.exp(ms[2] - m_all)
                dl = _DEN_LANE[h]
                res.append(acc / acc[:, dl:dl + 1])
            lane = lax.broadcasted_iota(jnp.int32, res[0].shape, 1)
            o_ref[0, pl.ds(row1, mc_rows, stride=16), :] = jnp.where(lane < HEAD_DIM, res[0], res[1])
        return carry

    lax.fori_loop(0, 16, merge, 0)


def _dilated(a1, a4, a16, tabs):
    B, S, wa = a1.shape
    npair = N_HEADS_A // 2
    nq = npair

    def col(off):
        return pl.BlockSpec((1, S, LANES), lambda b, p, off=off: (b, 0, off + p))

    in_specs = []
    for _ in range(3):
        in_specs += [col(0), col(nq), col(2 * nq)]
    in_specs.append(pl.BlockSpec((1,) + tabs.shape[1:], lambda b, p: (p, 0, 0, 0, 0, 0)))
    return pl.pallas_call(
        functools.partial(_dil_kernel, seq=S),
        grid=(B, npair),
        in_specs=in_specs,
        out_specs=pl.BlockSpec((1, S, LANES), lambda b, p: (b, 0, p)),
        out_shape=jax.ShapeDtypeStruct((B, S, npair * LANES), F32),
        scratch_shapes=[pltpu.VMEM((3, 2, S, LANES), BF16), pltpu.VMEM((3, 2, S, LANES), F32)],
        compiler_params=pltpu.CompilerParams(dimension_semantics=("arbitrary", "arbitrary"),
                                             vmem_limit_bytes=VMEM_LIMIT),
        name="dilated",
    )(a1, a1, a1, a4, a4, a4, a16, a16, a16, tabs)


def _moba_kernel(q_ref, k_ref, v_ref, tab_ref, o_ref, vext, acc, selb, *, seq):
    bs = MOBA_BLOCK
    nb = seq // bs
    ntab = tab_ref.shape[2]

    v = v_ref[0]
    m0, m1 = _head_lane_masks(v.shape)
    one = jnp.ones_like(v)
    vext[0] = jnp.where(m0, v, one)
    vext[1] = jnp.where(m1, v, one)

    kmean = jnp.sum(k_ref[0].astype(F32).reshape(nb, bs, LANES), axis=1) * (1.0 / bs)
    kmean = jnp.concatenate([kmean, jnp.zeros((LANES - nb, LANES), F32)], axis=0)
    km_hi = kmean.astype(BF16)
    km_lo = (kmean - km_hi.astype(F32)).astype(BF16)

    lane_b = lax.broadcasted_iota(jnp.int32, (bs, LANES), 1)
    head_mask = (lane_b < HEAD_DIM, lane_b >= HEAD_DIM)

    def masked_q(i0, h):
        q = q_ref[0, pl.ds(i0, bs), :]
        return jnp.where(head_mask[h], q, jnp.zeros_like(q))

    def select_and_diag(i, carry):
        i0 = pl.multiple_of(i * bs, bs)
        for h in range(2):
            qh = masked_q(i0, h)
            gate = _dot_nt(qh, km_hi) + _dot_nt(qh, km_lo)
            past = lane_b < i
            g = jnp.where(past, gate, NEG_INF)
            sel = jnp.zeros(g.shape, jnp.bool_)
            for _ in range(MOBA_TOPK):
                mx = jnp.max(g, axis=-1, keepdims=True)
                first = jnp.min(jnp.where(g == mx, lane_b, LANES), axis=-1, keepdims=True)
                pick = lane_b == first
                sel = sel | (pick & past)
                g = jnp.where(pick, NEG_INF, g)
            selb[h, pl.ds(i0, bs), :] = jnp.where(sel, 0.0, NEG_INF)
            s = _dot_nt(qh, k_ref[0, pl.ds(i0, bs), :]) + tab_ref[0, h, 0]
            m = jnp.max(s, axis=-1, keepdims=True)
            pr = jnp.exp(s - m).astype(BF16)
            pv = _dot(pr, vext[h, pl.ds(i0, bs), :])
            acc[h, pl.ds(i0, bs), :] = jnp.where(lane_b == _MAX_LANE[h], m, pv)
        return carry

    lax.fori_loop(0, nb, select_and_diag, 0)

    for j in range(nb - 1):
        j0 = j * bs
        kj = k_ref[0, j0:j0 + bs, :]

        def fold(i, carry, j=j, j0=j0, kj=kj):
            i0 = pl.multiple_of(i * bs, bs)
            t = jnp.minimum(i - j, ntab - 1)
            for h in range(2):
                qh = masked_q(i0, h)
                s = _dot_nt(qh, kj) + tab_ref[0, h, t] + selb[h, pl.ds(i0, bs), j:j + 1]
                old = acc[h, pl.ds(i0, bs), :]
                ml = _MAX_LANE[h]
                m_old = old[:, ml:ml + 1]
                m_new = jnp.maximum(m_old, jnp.max(s, axis=-1, keepdims=True))
                pr = jnp.exp(s - m_new).astype(BF16)
                new = old * jnp.exp(m_old - m_new) + _dot(pr, vext[h, j0:j0 + bs, :])
                acc[h, pl.ds(i0, bs), :] = jnp.where(lane_b == ml, m_new, new)
            return carry

        lax.fori_loop(j + 1, nb, fold, 0)

    def finish(i, carry):
        i0 = pl.multiple_of(i * bs, bs)
        a0 = acc[0, pl.ds(i0, bs), :]
        a1 = acc[1, pl.ds(i0, bs), :]
        r0 = a0 / a0[:, _DEN_LANE[0]:_DEN_LANE[0] + 1]
        r1 = a1 / a1[:, _DEN_LANE[1]:_DEN_LANE[1] + 1]
        o_ref[0, pl.ds(i0, bs), :] = jnp.where(lane_b < HEAD_DIM, r0, r1)
        return carry

    lax.fori_loop(0, nb, finish, 0)


def _moba(b1, tabs):
    B, S, wb = b1.shape
    npair = N_HEADS_B // 2

    def col(off):
        return pl.BlockSpec((1, S, LANES), lambda b, p, off=off: (b, 0, off + p))

    return pl.pallas_call(
        functools.partial(_moba_kernel, seq=S),
        grid=(B, npair),
        in_specs=[col(0), col(npair), col(2 * npair),
                  pl.BlockSpec((1,) + tabs.shape[1:], lambda b, p: (p, 0, 0, 0, 0))],
        out_specs=pl.BlockSpec((1, S, LANES), lambda b, p: (b, 0, p)),
        out_shape=jax.ShapeDtypeStruct((B, S, npair * LANES), F32),
        scratch_shapes=[pltpu.VMEM((2, S, LANES), BF16), pltpu.VMEM((2, S, LANES), F32),
                        pltpu.VMEM((2, S, LANES), F32)],
        compiler_params=pltpu.CompilerParams(dimension_semantics=("arbitrary", "arbitrary"),
                                             vmem_limit_bytes=VMEM_LIMIT),
        name="moba",
    )(b1, b1, b1, tabs)


def _post_kernel(x_ref, oa_ref, ob_ref, kv_ref, ga_ref, gb_ref, wout_ref, gc_ref, wcq_ref, wco_ref,
                 gf2_ref, wgu_ref, wd_ref, gfin_ref, y_ref, *, d_ff, final_norm):
    wa = oa_ref.shape[2]
    d = x_ref.shape[2]
    ya = _rms(oa_ref[0], ga_ref[...]).astype(BF16)
    yb = _rms(ob_ref[0], gb_ref[...]).astype(BF16)
    x = x_ref[0] + _dot(ya, wout_ref[:wa, :]) + _dot(yb, wout_ref[wa:, :])

    h = _rms(x, gc_ref[...]).astype(BF16)
    q = _dot(h, wcq_ref[...])
    dh = d // CROSS_HEADS
    scale = dh ** -0.5
    cross = None
    for c in range(CROSS_HEADS):
        qc = (q[:, c * dh:(c + 1) * dh] * scale).astype(BF16)
        s = _dot_nt(qc, kv_ref[0, :, c * dh:(c + 1) * dh])
        s = s - jnp.max(s, axis=-1, keepdims=True)
        e = jnp.exp(s)
        p = e / jnp.sum(e, axis=-1, keepdims=True)
        oc = _dot(p.astype(BF16), kv_ref[0, :, d + c * dh:d + (c + 1) * dh]).astype(BF16)
        part = _dot(oc, wco_ref[c * dh:(c + 1) * dh, :])
        cross = part if cross is None else cross + part
    x = x + cross

    h = _rms(x, gf2_ref[...]).astype(BF16)
    x = x + 0.5 * _swiglu_half(h, wgu_ref, wd_ref, d_ff)
    y_ref[0] = _rms(x, gfin_ref[...]) if final_norm else x


def _post(x1, oa, ob, kv, g_out_a, g_out_b, w_out, g_cross, w_cq, w_co, g_ffn2, wgu, wd, g_final, final_norm):
    B, S, D = x1.shape
    wa, wb = oa.shape[2], ob.shape[2]
    M = kv.shape[1]
    d_ff = wd.shape[0]
    tm = TOKEN_TILE
    tok = lambda w: pl.BlockSpec((1, tm, w), lambda b, i: (b, i, 0))
    return pl.pallas_call(
        functools.partial(_post_kernel, d_ff=d_ff, final_norm=final_norm),
        grid=(B, S // tm),
        in_specs=[tok(D), tok(wa), tok(wb),
                  pl.BlockSpec((1, M, 2 * D), lambda b, i: (b, 0, 0)),
                  _resident((1, wa)), _resident((1, wb)), _resident((wa + wb, D)),
                  _resident((1, D)), _resident((D, D)), _resident((D, D)),
                  _resident((1, D)), _resident((D, 2 * d_ff)), _resident((d_ff, D)),
                  _resident((1, D))],
        out_specs=tok(D),
        out_shape=jax.ShapeDtypeStruct((B, S, D), F32),
        compiler_params=pltpu.CompilerParams(dimension_semantics=("arbitrary", "arbitrary"),
                                             vmem_limit_bytes=VMEM_LIMIT),
        name="post",
    )(x1, oa, ob, kv, g_out_a.reshape(1, wa), g_out_b.reshape(1, wb), w_out, g_cross.reshape(1, D),
      w_cq, w_co, g_ffn2.reshape(1, D), wgu, wd, g_final.reshape(1, D))


def kernel(x, mem, g_ffn1, w_ffn1_gu, w_ffn1_down, g_mix, w_in, rel_bias, g_out_a, g_out_b, w_out,
           g_cross, g_mem, w_cq, w_ckv, w_co, g_ffn2, w_ffn2_gu, w_ffn2_down, g_final):
    B, S, D = x.shape
    depth = g_ffn1.shape[0]
    dil_tabs = _dilated_tables(rel_bias)
    moba_tabs = _moba_tables(rel_bias)
    bf = lambda w: w.astype(BF16)
    y = None
    for l in range(depth):
        kv = _memkv(mem, g_mem[l], bf(w_ckv[l]))
        x1 = _ffn1(x.reshape(B * S, D), g_ffn1[l], bf(w_ffn1_gu[l]), bf(w_ffn1_down[l])).reshape(B, S, D)
        a1, a4, a16, b1 = _proj(x1, g_mix[l], bf(w_in[l]))
        wa = a1.shape[2]
        oa = _dilated(a1, a4.reshape(B, S, wa), a16.reshape(B, S, wa), dil_tabs)
        ob = _moba(b1, moba_tabs)
        y = _post(x1, oa, ob, kv, g_out_a[l], g_out_b[l], bf(w_out[l]), g_cross[l], bf(w_cq[l]),
                  bf(w_co[l]), g_ffn2[l], bf(w_ffn2_gu[l]), bf(w_ffn2_down[l]), g_final,
                  final_norm=(l == depth - 1))
        x = y
    return y
```

```python
import functools
import math

import numpy as np
import jax
import jax.numpy as jnp
from jax import lax
from jax.experimental import pallas as pl
from jax.experimental.pallas import tpu as pltpu

F32 = jnp.float32
BF16 = jnp.bfloat16

HEAD_DIM = 64
LANES = 128
N_HEADS_A = 12
N_HEADS_B = 4
DILATED_PATTERNS = ((128, 1), (512, 4), (2048, 16))
DIL_BLOCK = 128
MOBA_BLOCK = 256
MOBA_TOPK = 3
CROSS_HEADS = 4
REL_BUCKETS = 32
REL_MAX_DIST = 2048
NORM_EPS = 1e-6
NEG_INF = -jnp.inf
VMEM_LIMIT = 56 * 1024 * 1024

FFN_CHUNK = 256
TOKEN_TILE = 512


def _rms(x, g):
    xf = x.astype(F32)
    return xf * lax.rsqrt(jnp.mean(xf * xf, axis=-1, keepdims=True) + NORM_EPS) * g


def _dot(a, b):
    return jnp.dot(a, b, preferred_element_type=F32)


def _dot_nt(a, b):
    return lax.dot_general(a, b, (((1,), (1,)), ((), ())), preferred_element_type=F32)


def _resident(shape):
    nd = len(shape)
    return pl.BlockSpec(shape, lambda *_: (0,) * nd, pipeline_mode=pl.Buffered(1))


def _swiglu_half(h, wgu_ref, wd_ref, d_ff):
    acc = None
    for c in range(d_ff // FFN_CHUNK):
        lo = c * FFN_CHUNK
        g = _dot(h, wgu_ref[:, lo:lo + FFN_CHUNK])
        u = _dot(h, wgu_ref[:, d_ff + lo:d_ff + lo + FFN_CHUNK])
        a = (g * (1.0 / (1.0 + jnp.exp(-g))) * u).astype(BF16)
        part = _dot(a, wd_ref[lo:lo + FFN_CHUNK, :])
        acc = part if acc is None else acc + part
    return acc


def _memkv_kernel(mem_ref, g_ref, w_ref, kv_ref):
    h = _rms(mem_ref[0], g_ref[...]).astype(BF16)
    kv_ref[0] = _dot(h, w_ref[...]).astype(BF16)


def _memkv(mem, g_mem, w_ckv):
    B, M, D = mem.shape
    N = w_ckv.shape[1]
    return pl.pallas_call(
        _memkv_kernel,
        grid=(B,),
        in_specs=[pl.BlockSpec((1, M, D), lambda b: (b, 0, 0)),
                  _resident((1, D)), _resident((D, N))],
        out_specs=pl.BlockSpec((1, M, N), lambda b: (b, 0, 0)),
        out_shape=jax.ShapeDtypeStruct((B, M, N), BF16),
        compiler_params=pltpu.CompilerParams(dimension_semantics=("arbitrary",),
                                             vmem_limit_bytes=VMEM_LIMIT),
        name="memkv",
    )(mem, g_mem.reshape(1, D), w_ckv)


def _ffn1_kernel(x_ref, g_ref, wgu_ref, wd_ref, o_ref, *, d_ff):
    x = x_ref[...]
    h = _rms(x, g_ref[...]).astype(BF16)
    o_ref[...] = x + 0.5 * _swiglu_half(h, wgu_ref, wd_ref, d_ff)


def _ffn1(x2d, g, wgu, wd):
    T, D = x2d.shape
    d_ff = wd.shape[0]
    tm = TOKEN_TILE
    return pl.pallas_call(
        functools.partial(_ffn1_kernel, d_ff=d_ff),
        grid=(T // tm,),
        in_specs=[pl.BlockSpec((tm, D), lambda i: (i, 0)),
                  _resident((1, D)), _resident((D, 2 * d_ff)), _resident((d_ff, D))],
        out_specs=pl.BlockSpec((tm, D), lambda i: (i, 0)),
        out_shape=jax.ShapeDtypeStruct((T, D), F32),
        compiler_params=pltpu.CompilerParams(dimension_semantics=("arbitrary",),
                                             vmem_limit_bytes=VMEM_LIMIT),
        name="ffn1",
    )(x2d, g.reshape(1, D), wgu, wd)


def _proj_kernel(x_ref, g_ref, w_ref, a1_ref, a4_ref, a16_ref, b_ref, scr, *, wa, wqa, wqb):
    tm = x_ref.shape[1]
    h = _rms(x_ref[0], g_ref[...]).astype(BF16)
    scale = HEAD_DIM ** -0.5
    pb = _dot(h, w_ref[:, wa:])
    b_ref[0, :, :wqb] = (pb[:, :wqb] * scale).astype(BF16)
    b_ref[0, :, wqb:] = pb[:, wqb:].astype(BF16)
    pa = _dot(h, w_ref[:, :wa])
    for c in range(wa // LANES):
        cols = slice(c * LANES, (c + 1) * LANES)
        tile = pa[:, cols] * scale if c < wqa // LANES else pa[:, cols]
        a1_ref[0, :, cols] = tile.astype(BF16)
        scr[c] = tile
        for r in range(4):
            a4_ref[0, r, :, cols] = scr[c, pl.ds(r, tm // 4, stride=4), :].astype(BF16)
        for r in range(16):
            a16_ref[0, r, :, cols] = scr[c, pl.ds(r, tm // 16, stride=16), :].astype(BF16)


def _proj(x1, g, w_in):
    B, S, D = x1.shape
    wa = 3 * N_HEADS_A * HEAD_DIM
    wb = 3 * N_HEADS_B * HEAD_DIM
    tm = TOKEN_TILE
    kern = functools.partial(_proj_kernel, wa=wa, wqa=N_HEADS_A * HEAD_DIM, wqb=N_HEADS_B * HEAD_DIM)
    return pl.pallas_call(
        kern,
        grid=(B, S // tm),
        in_specs=[pl.BlockSpec((1, tm, D), lambda b, i: (b, i, 0)),
                  _resident((1, D)), _resident((D, wa + wb))],
        out_specs=[pl.BlockSpec((1, tm, wa), lambda b, i: (b, i, 0)),
                   pl.BlockSpec((1, 4, tm // 4, wa), lambda b, i: (b, 0, i, 0)),
                   pl.BlockSpec((1, 16, tm // 16, wa), lambda b, i: (b, 0, i, 0)),
                   pl.BlockSpec((1, tm, wb), lambda b, i: (b, i, 0))],
        out_shape=[jax.ShapeDtypeStruct((B, S, wa), BF16),
                   jax.ShapeDtypeStruct((B, 4, S // 4, wa), BF16),
                   jax.ShapeDtypeStruct((B, 16, S // 16, wa), BF16),
                   jax.ShapeDtypeStruct((B, S, wb), BF16)],
        scratch_shapes=[pltpu.VMEM((wa // LANES, tm, LANES), F32)],
        compiler_params=pltpu.CompilerParams(dimension_semantics=("arbitrary", "arbitrary"),
                                             vmem_limit_bytes=VMEM_LIMIT),
        name="proj",
    )(x1, g.reshape(1, D), w_in)


def _rel_bucket(dist):
    max_exact = REL_BUCKETS // 2
    n = jnp.maximum(dist, 0)
    nf = jnp.maximum(n, 1).astype(F32)
    large = max_exact + (jnp.log(nf / max_exact) / math.log(REL_MAX_DIST / max_exact)
                         * (REL_BUCKETS - max_exact)).astype(jnp.int32)
    large = jnp.minimum(large, REL_BUCKETS - 1)
    return jnp.where(n < max_exact, n, large)


def _bias_lookup(dist, bias_cols):
    onehot = (_rel_bucket(dist)[..., None] == jnp.arange(REL_BUCKETS)).astype(F32)
    return jnp.einsum("...b,bh->...h", onehot, bias_cols.astype(F32), precision=lax.Precision.HIGHEST)


def _dilated_tables(rel_bias):
    qb = DIL_BLOCK
    i = jnp.arange(qb)[:, None]
    j = jnp.arange(2 * qb)[None, :]
    per_pattern = []
    for window, dil in DILATED_PATTERNS:
        n_steps = window // dil
        d_band = qb + i - j
        ok_band = (d_band >= 0) & (d_band <= n_steps)
        d_own = i - j
        ok_own = (d_own >= 0) & (d_own <= n_steps)
        variants = []
        for delta, ok in ((d_band, ok_band), (d_band, ok_band & (j >= qb)), (d_own, ok_own)):
            bias = _bias_lookup(delta * dil, rel_bias[:, :N_HEADS_A])
            variants.append(jnp.where(ok[:, :, None], bias, NEG_INF))
        per_pattern.append(jnp.stack(variants))
    t = jnp.stack(per_pattern)
    t = t.transpose(4, 0, 1, 2, 3).reshape(N_HEADS_A // 2, 2, len(DILATED_PATTERNS), 3, qb, 2 * qb)
    t = t.transpose(0, 2, 3, 1, 4, 5)
    return t.reshape(N_HEADS_A // 2, len(DILATED_PATTERNS), 3, 2 * qb, 2 * qb).astype(F32)


def _moba_num_tables():
    return -(-(REL_MAX_DIST + MOBA_BLOCK - 1) // MOBA_BLOCK) + 1


def _moba_tables(rel_bias):
    bs = MOBA_BLOCK
    j = jnp.arange(bs)[:, None]
    i = jnp.arange(bs)[None, :]
    tabs = []
    for off in range(_moba_num_tables()):
        n = off * bs + i - j
        bias = _bias_lookup(n, rel_bias[:, N_HEADS_A:])
        tabs.append(jnp.where((n >= 0)[:, :, None], bias, NEG_INF))
    t = jnp.stack(tabs)
    return t.transpose(3, 0, 1, 2).reshape(N_HEADS_B // 2, 2, len(tabs), bs, bs).astype(F32)


def _head_lane_masks(shape):
    lane = lax.broadcasted_iota(jnp.int32, shape, len(shape) - 1)
    return lane < HEAD_DIM, lane >= HEAD_DIM


_DEN_LANE = (HEAD_DIM, 1)
_MAX_LANE = (LANES - 1, 0)


def _dil_kernel(q1, k1, v1, q4, k4, v4, q16, k16, v16, tab_ref, o_ref, vext, oscr, *, seq):
    qb = DIL_BLOCK
    nblk = seq // qb
    qrefs, krefs, vrefs = (q1, q4, q16), (k1, k4, k16), (v1, v4, v16)
    dils = tuple(d for _, d in DILATED_PATTERNS)

    for p in range(3):
        v = vrefs[p][0]
        vext[p, :, :LANES] = v
        vext[p, :, LANES:] = jnp.ones_like(v)

    row2 = lax.broadcasted_iota(jnp.int32, (2 * qb, LANES), 0)
    lane2 = lax.broadcasted_iota(jnp.int32, (2 * qb, LANES), 1)
    own = (row2 < qb) == (lane2 < HEAD_DIM)
    max_lane = jnp.where(row2 < qb, _MAX_LANE[0], _MAX_LANE[1])

    def block(blk, carry):
        q0 = pl.multiple_of(blk * qb, qb)
        k0 = pl.multiple_of(jnp.maximum(blk - 1, 0) * qb, qb)
        for p in range(3):
            per = nblk // dils[p]
            variant = jnp.where(blk == 0, 2, jnp.where(blk % per == 0, 1, 0))
            q = qrefs[p][0, pl.ds(q0, qb), :]
            q2 = jnp.concatenate([q, q], axis=0)
            qs = jnp.where(own, q2, jnp.zeros_like(q2))
            s = _dot_nt(qs, krefs[p][0, pl.ds(k0, 2 * qb), :]) + tab_ref[0, p, variant]
            m = jnp.max(s, axis=-1, keepdims=True)
            pr = jnp.exp(s - m).astype(BF16)
            pv = _dot(pr, vext[p, pl.ds(k0, 2 * qb), :])
            o = jnp.where(own, pv[:, :LANES], pv[:, LANES:])
            o = jnp.where(lane2 == max_lane, m, o)
            oscr[p, 0, pl.ds(q0, qb), :] = o[:qb]
            oscr[p, 1, pl.ds(q0, qb), :] = o[qb:]
        return carry

    lax.fori_loop(0, nblk, block, 0, unroll=2)

    n16 = seq // 16
    n4 = seq // 4
    mc_rows = 128

    def merge(r, carry):
        for mc in range(n16 // mc_rows):
            m_lo = mc * mc_rows
            row16 = pl.multiple_of(r * n16 + m_lo, mc_rows)
            row4 = (r % 4) * n4 + r // 4 + 4 * m_lo
            row1 = r + 16 * m_lo
            res = []
            for h in range(2):
                e = (oscr[0, h, pl.ds(row1, mc_rows, stride=16), :],
                     oscr[1, h, pl.ds(row4, mc_rows, stride=4), :],
                     oscr[2, h, pl.ds(row16, mc_rows), :])
                ml = _MAX_LANE[h]
                ms = [x[:, ml:ml + 1] for x in e]
                m_all = jnp.maximum(jnp.maximum(ms[0], ms[1]), ms[2])
                acc = e[0] * jnp.exp(ms[0] - m_all)
                acc = acc + e[1] * jnp.exp(ms[1] - m_all)
                acc = acc + e[2] * jnp.exp(ms[2] - m_all)
                dl = _DEN_LANE[h]
                res.append(acc / acc[:, dl:dl + 1])
            lane = lax.broadcasted_iota(jnp.int32, res[0].shape, 1)
            o_ref[0, pl.ds(row1, mc_rows, stride=16), :] = jnp.where(lane < HEAD_DIM, res[0], res[1])
        return carry

    lax.fori_loop(0, 16, merge, 0)


def _dilated(a1, a4, a16, tabs):
    B, S, wa = a1.shape
    npair = N_HEADS_A // 2
    nq = npair

    def col(off):
        return pl.BlockSpec((1, S, LANES), lambda b, p, off=off: (b, 0, off + p))

    in_specs = []
    for _ in range(3):
        in_specs += [col(0), col(nq), col(2 * nq)]
    in_specs.append(pl.BlockSpec((1,) + tabs.shape[1:], lambda b, p: (p, 0, 0, 0, 0)))
    return pl.pallas_call(
        functools.partial(_dil_kernel, seq=S),
        grid=(B, npair),
        in_specs=in_specs,
        out_specs=pl.BlockSpec((1, S, LANES), lambda b, p: (b, 0, p)),
        out_shape=jax.ShapeDtypeStruct((B, S, npair * LANES), F32),
        scratch_shapes=[pltpu.VMEM((3, S, 2 * LANES), BF16), pltpu.VMEM((3, 2, S, LANES), F32)],
        compiler_params=pltpu.CompilerParams(dimension_semantics=("arbitrary", "arbitrary"),
                                             vmem_limit_bytes=VMEM_LIMIT),
        name="dilated",
    )(a1, a1, a1, a4, a4, a4, a16, a16, a16, tabs)


MOBA_GROUP = 4


def _moba_kernel(q_ref, k_ref, v_ref, tab_ref, o_ref, vt_ext, s_scr, p_scr, *, seq):
    bs = MOBA_BLOCK
    nb = seq // bs
    ntab = tab_ref.shape[2]

    vt = v_ref[0].astype(F32).T
    row_f = lax.broadcasted_iota(jnp.int32, vt.shape, 0)
    vt_ext[0] = jnp.where(row_f < HEAD_DIM, vt, 1.0).astype(BF16)
    vt_ext[1] = jnp.where(row_f >= HEAD_DIM, vt, 1.0).astype(BF16)

    kmean = jnp.sum(k_ref[0].astype(F32).reshape(nb, bs, LANES), axis=1) * (1.0 / bs)
    km_hi = kmean.astype(BF16)
    km_lo = (kmean - km_hi.astype(F32)).astype(BF16)

    lane_q = lax.broadcasted_iota(jnp.int32, (bs, LANES), 1)
    head_mask = (lane_q < HEAD_DIM, lane_q >= HEAD_DIM)
    blk_id = lax.broadcasted_iota(jnp.int32, (nb, bs), 0)
    row_o = lax.broadcasted_iota(jnp.int32, (LANES, bs), 0)

    def query_block(i, n_keys):
        i0 = pl.multiple_of(i * bs, bs)
        q = q_ref[0, pl.ds(i0, bs), :]
        outs = []
        for h in range(2):
            qh = jnp.where(head_mask[h], q, jnp.zeros_like(q))
            gate = _dot_nt(km_hi, qh) + _dot_nt(km_lo, qh)
            past = blk_id < i
            g = jnp.where(past, gate, NEG_INF)
            allow = blk_id == i
            for _ in range(MOBA_TOPK):
                mx = jnp.max(g, axis=0, keepdims=True)
                first = jnp.min(jnp.where(g == mx, blk_id, nb), axis=0, keepdims=True)
                pick = blk_id == first
                allow = allow | (pick & past)
                g = jnp.where(pick, NEG_INF, g)
            blk_bias = jnp.where(allow, 0.0, NEG_INF)

            st = _dot_nt(k_ref[0, :n_keys, :], qh)
            mx8 = None
            for jt in range(n_keys // bs):
                off = jnp.clip(i - jt, 0, ntab - 1)
                t = st[jt * bs:(jt + 1) * bs, :] + tab_ref[0, h, off] + blk_bias[jt:jt + 1, :]
                s_scr[h, jt * bs:(jt + 1) * bs, :] = t
                t8 = jnp.max(t.reshape(bs // 8, 8, bs), axis=0)
                mx8 = t8 if mx8 is None else jnp.maximum(mx8, t8)
            m = jnp.max(mx8, axis=0, keepdims=True)
            for jt in range(n_keys // bs):
                rows = slice(jt * bs, (jt + 1) * bs)
                p_scr[h, rows, :] = jnp.exp(s_scr[h, rows, :] - m).astype(BF16)
            outs.append(_dot(vt_ext[h, :, :n_keys], p_scr[h, :n_keys, :]))
        den0 = outs[0][HEAD_DIM:HEAD_DIM + 1, :]
        den1 = outs[1][0:1, :]
        o_t = jnp.where(row_o < HEAD_DIM, outs[0] / den0, outs[1] / den1)
        o_ref[0, pl.ds(i0, bs), :] = o_t.T

    for g0 in range(0, nb, MOBA_GROUP):
        n_keys = min(g0 + MOBA_GROUP, nb) * bs

        def body(i, carry, n_keys=n_keys):
            query_block(i, n_keys)
            return carry

        lax.fori_loop(g0, min(g0 + MOBA_GROUP, nb), body, 0)


def _moba(b1, tabs):
    B, S, wb = b1.shape
    npair = N_HEADS_B // 2

    def col(off):
        return pl.BlockSpec((1, S, LANES), lambda b, p, off=off: (b, 0, off + p))

    return pl.pallas_call(
        functools.partial(_moba_kernel, seq=S),
        grid=(B, npair),
        in_specs=[col(0), col(npair), col(2 * npair),
                  pl.BlockSpec((1,) + tabs.shape[1:], lambda b, p: (p, 0, 0, 0, 0))],
        out_specs=pl.BlockSpec((1, S, LANES), lambda b, p: (b, 0, p)),
        out_shape=jax.ShapeDtypeStruct((B, S, npair * LANES), F32),
        scratch_shapes=[pltpu.VMEM((2, LANES, S), BF16), pltpu.VMEM((2, S, MOBA_BLOCK), F32),
                        pltpu.VMEM((2, S, MOBA_BLOCK), BF16)],
        compiler_params=pltpu.CompilerParams(dimension_semantics=("arbitrary", "arbitrary"),
                                             vmem_limit_bytes=VMEM_LIMIT),
        name="moba",
    )(b1, b1, b1, tabs)


def _post_kernel(x_ref, oa_ref, ob_ref, kv_ref, ga_ref, gb_ref, wout_ref, gc_ref, wcq_ref, wco_ref,
                 gf2_ref, wgu_ref, wd_ref, gfin_ref, y_ref, *, d_ff, final_norm):
    wa = oa_ref.shape[2]
    d = x_ref.shape[2]
    ya = _rms(oa_ref[0], ga_ref[...]).astype(BF16)
    yb = _rms(ob_ref[0], gb_ref[...]).astype(BF16)
    x = x_ref[0] + _dot(ya, wout_ref[:wa, :]) + _dot(yb, wout_ref[wa:, :])

    h = _rms(x, gc_ref[...]).astype(BF16)
    q = _dot(h, wcq_ref[...])
    dh = d // CROSS_HEADS
    scale = dh ** -0.5
    cross = None
    for c in range(CROSS_HEADS):
        qc = (q[:, c * dh:(c + 1) * dh] * scale).astype(BF16)
        s = _dot_nt(qc, kv_ref[0, :, c * dh:(c + 1) * dh])
        s = s - jnp.max(s, axis=-1, keepdims=True)
        e = jnp.exp(s)
        p = e / jnp.sum(e, axis=-1, keepdims=True)
        oc = _dot(p.astype(BF16), kv_ref[0, :, d + c * dh:d + (c + 1) * dh]).astype(BF16)
        part = _dot(oc, wco_ref[c * dh:(c + 1) * dh, :])
        cross = part if cross is None else cross + part
    x = x + cross

    h = _rms(x, gf2_ref[...]).astype(BF16)
    x = x + 0.5 * _swiglu_half(h, wgu_ref, wd_ref, d_ff)
    y_ref[0] = _rms(x, gfin_ref[...]) if final_norm else x


def _post(x1, oa, ob, kv, g_out_a, g_out_b, w_out, g_cross, w_cq, w_co, g_ffn2, wgu, wd, g_final, final_norm):
    B, S, D = x1.shape
    wa, wb = oa.shape[2], ob.shape[2]
    M = kv.shape[1]
    d_ff = wd.shape[0]
    tm = TOKEN_TILE
    tok = lambda w: pl.BlockSpec((1, tm, w), lambda b, i: (b, i, 0))
    return pl.pallas_call(
        functools.partial(_post_kernel, d_ff=d_ff, final_norm=final_norm),
        grid=(B, S // tm),
        in_specs=[tok(D), tok(wa), tok(wb),
                  pl.BlockSpec((1, M, 2 * D), lambda b, i: (b, 0, 0)),
                  _resident((1, wa)), _resident((1, wb)), _resident((wa + wb, D)),
                  _resident((1, D)), _resident((D, D)), _resident((D, D)),
                  _resident((1, D)), _resident((D, 2 * d_ff)), _resident((d_ff, D)),
                  _resident((1, D))],
        out_specs=tok(D),
        out_shape=jax.ShapeDtypeStruct((B, S, D), F32),
        compiler_params=pltpu.CompilerParams(dimension_semantics=("arbitrary", "arbitrary"),
                                             vmem_limit_bytes=VMEM_LIMIT),
        name="post",
    )(x1, oa, ob, kv, g_out_a.reshape(1, wa), g_out_b.reshape(1, wb), w_out, g_cross.reshape(1, D),
      w_cq, w_co, g_ffn2.reshape(1, D), wgu, wd, g_final.reshape(1, D))


def kernel(x, mem, g_ffn1, w_ffn1_gu, w_ffn1_down, g_mix, w_in, rel_bias, g_out_a, g_out_b, w_out,
           g_cross, g_mem, w_cq, w_ckv, w_co, g_ffn2, w_ffn2_gu, w_ffn2_down, g_final):
    B, S, D = x.shape
    depth = g_ffn1.shape[0]
    dil_tabs = _dilated_tables(rel_bias)
    moba_tabs = _moba_tables(rel_bias)
    bf = lambda w: w.astype(BF16)
    y = None
    for l in range(depth):
        kv = _memkv(mem, g_mem[l], bf(w_ckv[l]))
        x1 = _ffn1(x.reshape(B * S, D), g_ffn1[l], bf(w_ffn1_gu[l]), bf(w_ffn1_down[l])).reshape(B, S, D)
        a1, a4, a16, b1 = _proj(x1, g_mix[l], bf(w_in[l]))
        wa = a1.shape[2]
        oa = _dilated(a1, a4.reshape(B, S, wa), a16.reshape(B, S, wa), dil_tabs)
        ob = _moba(b1, moba_tabs)
        y = _post(x1, oa, ob, kv, g_out_a[l], g_out_b[l], bf(w_out[l]), g_cross[l], bf(w_cq[l]),
                  bf(w_co[l]), g_ffn2[l], bf(w_ffn2_gu[l]), bf(w_ffn2_down[l]), g_final,
                  final_norm=(l == depth - 1))
        x = y
    return y
```

```python
import functools
import math

import numpy as np
import jax
import jax.numpy as jnp
from jax import lax
from jax.experimental import pallas as pl
from jax.experimental.pallas import tpu as pltpu

F32 = jnp.float32
BF16 = jnp.bfloat16

HEAD_DIM = 64
LANES = 128
N_HEADS_A = 12
N_HEADS_B = 4
DILATED_PATTERNS = ((128, 1), (512, 4), (2048, 16))
DIL_BLOCK = 128
MOBA_BLOCK = 256
MOBA_TOPK = 3
CROSS_HEADS = 4
REL_BUCKETS = 32
REL_MAX_DIST = 2048
NORM_EPS = 1e-6
NEG_INF = -jnp.inf
VMEM_LIMIT = 56 * 1024 * 1024

FFN_CHUNK = 256
TOKEN_TILE = 512


def _rms(x, g):
    xf = x.astype(F32)
    return xf * lax.rsqrt(jnp.mean(xf * xf, axis=-1, keepdims=True) + NORM_EPS) * g


def _dot(a, b):
    return jnp.dot(a, b, preferred_element_type=F32)


def _dot_nt(a, b):
    return lax.dot_general(a, b, (((1,), (1,)), ((), ())), preferred_element_type=F32)


def _resident(shape):
    nd = len(shape)
    return pl.BlockSpec(shape, lambda *_: (0,) * nd, pipeline_mode=pl.Buffered(1))


def _swiglu_half(h, wgu_ref, wd_ref, d_ff):
    acc = None
    for c in range(d_ff // FFN_CHUNK):
        lo = c * FFN_CHUNK
        g = _dot(h, wgu_ref[:, lo:lo + FFN_CHUNK])
        u = _dot(h, wgu_ref[:, d_ff + lo:d_ff + lo + FFN_CHUNK])
        a = (g * (1.0 / (1.0 + jnp.exp(-g))) * u).astype(BF16)
        part = _dot(a, wd_ref[lo:lo + FFN_CHUNK, :])
        acc = part if acc is None else acc + part
    return acc


def _memkv_kernel(mem_ref, g_ref, w_ref, kv_ref):
    h = _rms(mem_ref[0], g_ref[...]).astype(BF16)
    kv_ref[0] = _dot(h, w_ref[...]).astype(BF16)


def _memkv(mem, g_mem, w_ckv):
    B, M, D = mem.shape
    N = w_ckv.shape[1]
    return pl.pallas_call(
        _memkv_kernel,
        grid=(B,),
        in_specs=[pl.BlockSpec((1, M, D), lambda b: (b, 0, 0)),
                  _resident((1, D)), _resident((D, N))],
        out_specs=pl.BlockSpec((1, M, N), lambda b: (b, 0, 0)),
        out_shape=jax.ShapeDtypeStruct((B, M, N), BF16),
        compiler_params=pltpu.CompilerParams(dimension_semantics=("arbitrary",),
                                             vmem_limit_bytes=VMEM_LIMIT),
        name="memkv",
    )(mem, g_mem.reshape(1, D), w_ckv)


def _ffn1_kernel(x_ref, g_ref, wgu_ref, wd_ref, o_ref, *, d_ff):
    x = x_ref[...]
    h = _rms(x, g_ref[...]).astype(BF16)
    o_ref[...] = x + 0.5 * _swiglu_half(h, wgu_ref, wd_ref, d_ff)


def _ffn1(x2d, g, wgu, wd):
    T, D = x2d.shape
    d_ff = wd.shape[0]
    tm = TOKEN_TILE
    return pl.pallas_call(
        functools.partial(_ffn1_kernel, d_ff=d_ff),
        grid=(T // tm,),
        in_specs=[pl.BlockSpec((tm, D), lambda i: (i, 0)),
                  _resident((1, D)), _resident((D, 2 * d_ff)), _resident((d_ff, D))],
        out_specs=pl.BlockSpec((tm, D), lambda i: (i, 0)),
        out_shape=jax.ShapeDtypeStruct((T, D), F32),
        compiler_params=pltpu.CompilerParams(dimension_semantics=("arbitrary",),
                                             vmem_limit_bytes=VMEM_LIMIT),
        name="ffn1",
    )(x2d, g.reshape(1, D), wgu, wd)


def _proj_kernel(x_ref, g_ref, w_ref, a1_ref, a4_ref, a16_ref, b_ref, scr, scr4, *, wa, wqa, wqb):
    tm = x_ref.shape[1]
    h = _rms(x_ref[0], g_ref[...]).astype(BF16)
    scale = HEAD_DIM ** -0.5
    pb = _dot(h, w_ref[:, wa:])
    b_ref[0, :, :wqb] = (pb[:, :wqb] * scale).astype(BF16)
    b_ref[0, :, wqb:] = pb[:, wqb:].astype(BF16)
    group = 2 * LANES
    for c in range(wa // LANES):
        if c % (group // LANES) == 0:
            pa = _dot(h, w_ref[:, c * LANES:c * LANES + group])
        cols = slice(c * LANES, (c + 1) * LANES)
        sub = slice((c * LANES) % group, (c * LANES) % group + LANES)
        tile = pa[:, sub] * scale if c < wqa // LANES else pa[:, sub]
        a1_ref[0, :, cols] = tile.astype(BF16)
        b = c % 2
        scr[b] = tile
        for r in range(4):
            plane = scr[b, pl.ds(r, tm // 4, stride=4), :]
            a4_ref[0, r, :, cols] = plane.astype(BF16)
            scr4[b, r] = plane
        for r in range(16):
            a16_ref[0, r, :, cols] = scr4[b, r % 4, pl.ds(r // 4, tm // 16, stride=4), :].astype(BF16)


def _proj(x1, g, w_in):
    B, S, D = x1.shape
    wa = 3 * N_HEADS_A * HEAD_DIM
    wb = 3 * N_HEADS_B * HEAD_DIM
    tm = TOKEN_TILE
    kern = functools.partial(_proj_kernel, wa=wa, wqa=N_HEADS_A * HEAD_DIM, wqb=N_HEADS_B * HEAD_DIM)
    return pl.pallas_call(
        kern,
        grid=(B, S // tm),
        in_specs=[pl.BlockSpec((1, tm, D), lambda b, i: (b, i, 0)),
                  _resident((1, D)), _resident((D, wa + wb))],
        out_specs=[pl.BlockSpec((1, tm, wa), lambda b, i: (b, i, 0)),
                   pl.BlockSpec((1, 4, tm // 4, wa), lambda b, i: (b, 0, i, 0)),
                   pl.BlockSpec((1, 16, tm // 16, wa), lambda b, i: (b, 0, i, 0)),
                   pl.BlockSpec((1, tm, wb), lambda b, i: (b, i, 0))],
        out_shape=[jax.ShapeDtypeStruct((B, S, wa), BF16),
                   jax.ShapeDtypeStruct((B, 4, S // 4, wa), BF16),
                   jax.ShapeDtypeStruct((B, 16, S // 16, wa), BF16),
                   jax.ShapeDtypeStruct((B, S, wb), BF16)],
        scratch_shapes=[pltpu.VMEM((2, tm, LANES), F32), pltpu.VMEM((2, 4, tm // 4, LANES), F32)],
        compiler_params=pltpu.CompilerParams(dimension_semantics=("arbitrary", "arbitrary"),
                                             vmem_limit_bytes=VMEM_LIMIT),
        name="proj",
    )(x1, g.reshape(1, D), w_in)


def _rel_bucket(dist):
    max_exact = REL_BUCKETS // 2
    n = jnp.maximum(dist, 0)
    nf = jnp.maximum(n, 1).astype(F32)
    large = max_exact + (jnp.log(nf / max_exact) / math.log(REL_MAX_DIST / max_exact)
                         * (REL_BUCKETS - max_exact)).astype(jnp.int32)
    large = jnp.minimum(large, REL_BUCKETS - 1)
    return jnp.where(n < max_exact, n, large)


def _bias_lookup(dist, bias_cols):
    onehot = (_rel_bucket(dist)[..., None] == jnp.arange(REL_BUCKETS)).astype(F32)
    return jnp.einsum("...b,bh->...h", onehot, bias_cols.astype(F32), precision=lax.Precision.HIGHEST)


def _dilated_tables(rel_bias):
    qb = DIL_BLOCK
    i = jnp.arange(qb)[:, None]
    j = jnp.arange(2 * qb)[None, :]
    per_pattern = []
    for window, dil in DILATED_PATTERNS:
        n_steps = window // dil
        d_band = qb + i - j
        ok_band = (d_band >= 0) & (d_band <= n_steps)
        d_own = i - j
        ok_own = (d_own >= 0) & (d_own <= n_steps)
        variants = []
        for delta, ok in ((d_band, ok_band), (d_band, ok_band & (j >= qb)), (d_own, ok_own)):
            bias = _bias_lookup(delta * dil, rel_bias[:, :N_HEADS_A])
            variants.append(jnp.where(ok[:, :, None], bias, NEG_INF))
        per_pattern.append(jnp.stack(variants))
    t = jnp.stack(per_pattern)
    t = t.transpose(4, 0, 1, 2, 3).reshape(N_HEADS_A // 2, 2, len(DILATED_PATTERNS), 3, qb, 2 * qb)
    t = t.transpose(0, 2, 3, 1, 4, 5)
    return t.reshape(N_HEADS_A // 2, len(DILATED_PATTERNS), 3, 2 * qb, 2 * qb).astype(F32)


def _moba_num_tables():
    return -(-(REL_MAX_DIST + MOBA_BLOCK - 1) // MOBA_BLOCK) + 1


def _moba_tables(rel_bias):
    bs = MOBA_BLOCK
    j = jnp.arange(bs)[:, None]
    i = jnp.arange(bs)[None, :]
    tabs = []
    for off in range(_moba_num_tables()):
        n = off * bs + i - j
        bias = _bias_lookup(n, rel_bias[:, N_HEADS_A:])
        tabs.append(jnp.where((n >= 0)[:, :, None], bias, NEG_INF))
    t = jnp.stack(tabs)
    return t.transpose(3, 0, 1, 2).reshape(N_HEADS_B // 2, 2, len(tabs), bs, bs).astype(F32)


def _head_lane_masks(shape):
    lane = lax.broadcasted_iota(jnp.int32, shape, len(shape) - 1)
    return lane < HEAD_DIM, lane >= HEAD_DIM


_DEN_LANE = (HEAD_DIM, 1)
_MAX_LANE = (LANES - 1, 0)


def _dil_kernel(q1, k1, v1, q4, k4, v4, q16, k16, v16, tab_ref, o_ref, vext, oscr, sc_scr, *, seq):
    qb = DIL_BLOCK
    nblk = seq // qb
    qrefs, krefs, vrefs = (q1, q4, q16), (k1, k4, k16), (v1, v4, v16)
    dils = tuple(d for _, d in DILATED_PATTERNS)

    for p in range(3):
        v = vrefs[p][0]
        vext[p, :, :LANES] = v
        vext[p, :, LANES:] = jnp.ones_like(v)

    row2 = lax.broadcasted_iota(jnp.int32, (2 * qb, LANES), 0)
    lane2 = lax.broadcasted_iota(jnp.int32, (2 * qb, LANES), 1)
    own = (row2 < qb) == (lane2 < HEAD_DIM)
    max_lane = jnp.where(row2 < qb, _MAX_LANE[0], _MAX_LANE[1])

    def band_start(blk):
        return pl.multiple_of(jnp.maximum(blk - 1, 0) * qb, qb)

    def score_block(blk, slot):
        q0 = pl.multiple_of(blk * qb, qb)
        for p in range(3):
            q = qrefs[p][0, pl.ds(q0, qb), :]
            q2 = jnp.concatenate([q, q], axis=0)
            qs = jnp.where(own, q2, jnp.zeros_like(q2))
            sc_scr[slot, p] = _dot_nt(qs, krefs[p][0, pl.ds(band_start(blk), 2 * qb), :])

    score_block(0, 0)

    def softmax_block(blk, slot):
        q0 = pl.multiple_of(blk * qb, qb)
        k0 = band_start(blk)
        for p in range(3):
            per = nblk // dils[p]
            variant = jnp.where(blk == 0, 2, jnp.where(blk % per == 0, 1, 0))
            s = sc_scr[slot, p] + tab_ref[0, p, variant]
            m = jnp.max(s, axis=-1, keepdims=True)
            pr = jnp.exp(s - m).astype(BF16)
            pv = _dot(pr, vext[p, pl.ds(k0, 2 * qb), :])
            o = jnp.where(own, pv[:, :LANES], pv[:, LANES:])
            o = jnp.where(lane2 == max_lane, m, o)
            oscr[p, 0, pl.ds(q0, qb), :] = o[:qb]
            oscr[p, 1, pl.ds(q0, qb), :] = o[qb:]

    def block_pair(i, carry):
        b0 = 2 * i
        score_block(b0 + 1, 1)
        softmax_block(b0, 0)
        score_block(jnp.minimum(b0 + 2, nblk - 1), 0)
        softmax_block(b0 + 1, 1)
        return carry

    lax.fori_loop(0, nblk // 2, block_pair, 0)

    n16 = seq // 16
    n4 = seq // 4
    mc_rows = 128

    def merge(r, carry):
        for mc in range(n16 // mc_rows):
            m_lo = mc * mc_rows
            row16 = pl.multiple_of(r * n16 + m_lo, mc_rows)
            row4 = (r % 4) * n4 + r // 4 + 4 * m_lo
            row1 = r + 16 * m_lo
            res = []
            for h in range(2):
                e = (oscr[0, h, pl.ds(row1, mc_rows, stride=16), :],
                     oscr[1, h, pl.ds(row4, mc_rows, stride=4), :],
                     oscr[2, h, pl.ds(row16, mc_rows), :])
                ml = _MAX_LANE[h]
                ms = [x[:, ml:ml + 1] for x in e]
                m_all = jnp.maximum(jnp.maximum(ms[0], ms[1]), ms[2])
                acc = e[0] * jnp.exp(ms[0] - m_all)
                acc = acc + e[1] * jnp.exp(ms[1] - m_all)
                acc = acc + e[2] * jnp.exp(ms[2] - m_all)
                dl = _DEN_LANE[h]
                res.append(acc / acc[:, dl:dl + 1])
            lane = lax.broadcasted_iota(jnp.int32, res[0].shape, 1)
            o_ref[0, pl.ds(row1, mc_rows, stride=16), :] = jnp.where(lane < HEAD_DIM, res[0], res[1])
        return carry

    lax.fori_loop(0, 16, merge, 0)


def _dilated(a1, a4, a16, tabs):
    B, S, wa = a1.shape
    npair = N_HEADS_A // 2
    nq = npair

    def col(off):
        return pl.BlockSpec((1, S, LANES), lambda b, p, off=off: (b, 0, off + p))

    in_specs = []
    for _ in range(3):
        in_specs += [col(0), col(nq), col(2 * nq)]
    in_specs.append(pl.BlockSpec((1,) + tabs.shape[1:], lambda b, p: (p, 0, 0, 0, 0)))
    return pl.pallas_call(
        functools.partial(_dil_kernel, seq=S),
        grid=(B, npair),
        in_specs=in_specs,
        out_specs=pl.BlockSpec((1, S, LANES), lambda b, p: (b, 0, p)),
        out_shape=jax.ShapeDtypeStruct((B, S, npair * LANES), F32),
        scratch_shapes=[pltpu.VMEM((3, S, 2 * LANES), BF16), pltpu.VMEM((3, 2, S, LANES), F32),
                        pltpu.VMEM((2, 3, 2 * DIL_BLOCK, 2 * DIL_BLOCK), F32)],
        compiler_params=pltpu.CompilerParams(dimension_semantics=("arbitrary", "arbitrary"),
                                             vmem_limit_bytes=VMEM_LIMIT),
        name="dilated",
    )(a1, a1, a1, a4, a4, a4, a16, a16, a16, tabs)


MOBA_GROUP = 4


def _moba_kernel(q_ref, k_ref, v_ref, tab_ref, o_ref, vt_ext, s_scr, p_scr, *, seq):
    bs = MOBA_BLOCK
    nb = seq // bs
    ntab = tab_ref.shape[2]

    vt = v_ref[0].astype(F32).T
    row_f = lax.broadcasted_iota(jnp.int32, vt.shape, 0)
    vt_ext[0] = jnp.where(row_f < HEAD_DIM, vt, 1.0).astype(BF16)
    vt_ext[1] = jnp.where(row_f >= HEAD_DIM, vt, 1.0).astype(BF16)

    kmean = jnp.sum(k_ref[0].astype(F32).reshape(nb, bs, LANES), axis=1) * (1.0 / bs)
    km_hi = kmean.astype(BF16)
    km_lo = (kmean - km_hi.astype(F32)).astype(BF16)

    lane_q = lax.broadcasted_iota(jnp.int32, (bs, LANES), 1)
    head_mask = (lane_q < HEAD_DIM, lane_q >= HEAD_DIM)
    blk_id = lax.broadcasted_iota(jnp.int32, (nb, bs), 0)
    row_o = lax.broadcasted_iota(jnp.int32, (LANES, bs), 0)

    def query_block(i, n_keys):
        i0 = pl.multiple_of(i * bs, bs)
        q = q_ref[0, pl.ds(i0, bs), :]
        gates = []
        for h in range(2):
            qh = jnp.where(head_mask[h], q, jnp.zeros_like(q))
            gates.append(_dot_nt(km_hi, qh) + _dot_nt(km_lo, qh))
            s_scr[h, :n_keys, :] = _dot_nt(k_ref[0, :n_keys, :], qh)
        outs = []
        for h in range(2):
            past = blk_id < i
            g = jnp.where(past, gates[h], NEG_INF)
            allow = blk_id == i
            for _ in range(MOBA_TOPK):
                mx = jnp.max(g, axis=0, keepdims=True)
                first = jnp.min(jnp.where(g == mx, blk_id, nb), axis=0, keepdims=True)
                pick = blk_id == first
                allow = allow | (pick & past)
                g = jnp.where(pick, NEG_INF, g)
            blk_bias = jnp.where(allow, 0.0, NEG_INF)

            mx8 = None
            for jt in range(n_keys // bs):
                rows = slice(jt * bs, (jt + 1) * bs)
                off = jnp.clip(i - jt, 0, ntab - 1)
                t = s_scr[h, rows, :] + tab_ref[0, h, off] + blk_bias[jt:jt + 1, :]
                s_scr[h, rows, :] = t
                t8 = jnp.max(t.reshape(bs // 8, 8, bs), axis=0)
                mx8 = t8 if mx8 is None else jnp.maximum(mx8, t8)
            m = jnp.max(mx8, axis=0, keepdims=True)
            for jt in range(n_keys // bs):
                rows = slice(jt * bs, (jt + 1) * bs)
                p_scr[h, rows, :] = jnp.exp(s_scr[h, rows, :] - m).astype(BF16)
            outs.append(_dot(vt_ext[h, :, :n_keys], p_scr[h, :n_keys, :]))
        den0 = outs[0][HEAD_DIM:HEAD_DIM + 1, :]
        den1 = outs[1][0:1, :]
        o_t = jnp.where(row_o < HEAD_DIM, outs[0] / den0, outs[1] / den1)
        o_ref[0, pl.ds(i0, bs), :] = o_t.T

    for g0 in range(0, nb, MOBA_GROUP):
        n_keys = min(g0 + MOBA_GROUP, nb) * bs

        def body(i, carry, n_keys=n_keys):
            query_block(i, n_keys)
            return carry

        lax.fori_loop(g0, min(g0 + MOBA_GROUP, nb), body, 0)


def _moba(b1, tabs):
    B, S, wb = b1.shape
    npair = N_HEADS_B // 2

    def col(off):
        return pl.BlockSpec((1, S, LANES), lambda b, p, off=off: (b, 0, off + p))

    return pl.pallas_call(
        functools.partial(_moba_kernel, seq=S),
        grid=(B, npair),
        in_specs=[col(0), col(npair), col(2 * npair),
                  pl.BlockSpec((1,) + tabs.shape[1:], lambda b, p: (p, 0, 0, 0, 0))],
        out_specs=pl.BlockSpec((1, S, LANES), lambda b, p: (b, 0, p)),
        out_shape=jax.ShapeDtypeStruct((B, S, npair * LANES), F32),
        scratch_shapes=[pltpu.VMEM((2, LANES, S), BF16), pltpu.VMEM((2, S, MOBA_BLOCK), F32),
                        pltpu.VMEM((2, S, MOBA_BLOCK), BF16)],
        compiler_params=pltpu.CompilerParams(dimension_semantics=("arbitrary", "arbitrary"),
                                             vmem_limit_bytes=VMEM_LIMIT),
        name="moba",
    )(b1, b1, b1, tabs)


def _post_kernel(x_ref, oa_ref, ob_ref, kv_ref, ga_ref, gb_ref, wout_ref, gc_ref, wcq_ref, wco_ref,
                 gf2_ref, wgu_ref, wd_ref, gfin_ref, y_ref, *, d_ff, final_norm):
    wa = oa_ref.shape[2]
    d = x_ref.shape[2]
    ya = _rms(oa_ref[0], ga_ref[...]).astype(BF16)
    yb = _rms(ob_ref[0], gb_ref[...]).astype(BF16)
    x = x_ref[0] + _dot(ya, wout_ref[:wa, :]) + _dot(yb, wout_ref[wa:, :])

    h = _rms(x, gc_ref[...]).astype(BF16)
    q = _dot(h, wcq_ref[...])
    dh = d // CROSS_HEADS
    scale = dh ** -0.5
    cross = None
    for c in range(CROSS_HEADS):
        qc = (q[:, c * dh:(c + 1) * dh] * scale).astype(BF16)
        s = _dot_nt(qc, kv_ref[0, :, c * dh:(c + 1) * dh])
        s = s - jnp.max(s, axis=-1, keepdims=True)
        e = jnp.exp(s)
        p = e / jnp.sum(e, axis=-1, keepdims=True)
        oc = _dot(p.astype(BF16), kv_ref[0, :, d + c * dh:d + (c + 1) * dh]).astype(BF16)
        part = _dot(oc, wco_ref[c * dh:(c + 1) * dh, :])
        cross = part if cross is None else cross + part
    x = x + cross

    h = _rms(x, gf2_ref[...]).astype(BF16)
    x = x + 0.5 * _swiglu_half(h, wgu_ref, wd_ref, d_ff)
    y_ref[0] = _rms(x, gfin_ref[...]) if final_norm else x


def _post(x1, oa, ob, kv, g_out_a, g_out_b, w_out, g_cross, w_cq, w_co, g_ffn2, wgu, wd, g_final, final_norm):
    B, S, D = x1.shape
    wa, wb = oa.shape[2], ob.shape[2]
    M = kv.shape[1]
    d_ff = wd.shape[0]
    tm = TOKEN_TILE
    tok = lambda w: pl.BlockSpec((1, tm, w), lambda b, i: (b, i, 0))
    return pl.pallas_call(
        functools.partial(_post_kernel, d_ff=d_ff, final_norm=final_norm),
        grid=(B, S // tm),
        in_specs=[tok(D), tok(wa), tok(wb),
                  pl.BlockSpec((1, M, 2 * D), lambda b, i: (b, 0, 0)),
                  _resident((1, wa)), _resident((1, wb)), _resident((wa + wb, D)),
                  _resident((1, D)), _resident((D, D)), _resident((D, D)),
                  _resident((1, D)), _resident((D, 2 * d_ff)), _resident((d_ff, D)),
                  _resident((1, D))],
        out_specs=tok(D),
        out_shape=jax.ShapeDtypeStruct((B, S, D), F32),
        compiler_params=pltpu.CompilerParams(dimension_semantics=("arbitrary", "arbitrary"),
                                             vmem_limit_bytes=VMEM_LIMIT),
        name="post",
    )(x1, oa, ob, kv, g_out_a.reshape(1, wa), g_out_b.reshape(1, wb), w_out, g_cross.reshape(1, D),
      w_cq, w_co, g_ffn2.reshape(1, D), wgu, wd, g_final.reshape(1, D))


def kernel(x, mem, g_ffn1, w_ffn1_gu, w_ffn1_down, g_mix, w_in, rel_bias, g_out_a, g_out_b, w_out,
           g_cross, g_mem, w_cq, w_ckv, w_co, g_ffn2, w_ffn2_gu, w_ffn2_down, g_final):
    B, S, D = x.shape
    depth = g_ffn1.shape[0]
    dil_tabs = _dilated_tables(rel_bias)
    moba_tabs = _moba_tables(rel_bias)
    bf = lambda w: w.astype(BF16)
    y = None
    for l in range(depth):
        kv = _memkv(mem, g_mem[l], bf(w_ckv[l]))
        x1 = _ffn1(x.reshape(B * S, D), g_ffn1[l], bf(w_ffn1_gu[l]), bf(w_ffn1_down[l])).reshape(B, S, D)
        a1, a4, a16, b1 = _proj(x1, g_mix[l], bf(w_in[l]))
        wa = a1.shape[2]
        oa = _dilated(a1, a4.reshape(B, S, wa), a16.reshape(B, S, wa), dil_tabs)
        ob = _moba(b1, moba_tabs)
        y = _post(x1, oa, ob, kv, g_out_a[l], g_out_b[l], bf(w_out[l]), g_cross[l], bf(w_cq[l]),
                  bf(w_co[l]), g_ffn2[l], bf(w_ffn2_gu[l]), bf(w_ffn2_down[l]), g_final,
                  final_norm=(l == depth - 1))
        x = y
    return y
```

```python
import functools
import math

import numpy as np
import jax
import jax.numpy as jnp
from jax import lax
from jax.experimental import pallas as pl
from jax.experimental.pallas import tpu as pltpu

F32 = jnp.float32
BF16 = jnp.bfloat16

HEAD_DIM = 64
LANES = 128
N_HEADS_A = 12
N_HEADS_B = 4
DILATED_PATTERNS = ((128, 1), (512, 4), (2048, 16))
DIL_BLOCK = 128
MOBA_BLOCK = 256
MOBA_TOPK = 3
CROSS_HEADS = 4
REL_BUCKETS = 32
REL_MAX_DIST = 2048
NORM_EPS = 1e-6
NEG_INF = -jnp.inf
VMEM_LIMIT = 56 * 1024 * 1024

FFN_CHUNK = 256
TOKEN_TILE = 512


def _rms(x, g):
    xf = x.astype(F32)
    return xf * lax.rsqrt(jnp.mean(xf * xf, axis=-1, keepdims=True) + NORM_EPS) * g


def _dot(a, b):
    return jnp.dot(a, b, preferred_element_type=F32)


def _dot_nt(a, b):
    return lax.dot_general(a, b, (((1,), (1,)), ((), ())), preferred_element_type=F32)


def _resident(shape):
    nd = len(shape)
    return pl.BlockSpec(shape, lambda *_: (0,) * nd, pipeline_mode=pl.Buffered(1))


def _swiglu_half(h, wgu_ref, wd_ref, d_ff):
    acc = None
    for c in range(d_ff // FFN_CHUNK):
        lo = c * FFN_CHUNK
        g = _dot(h, wgu_ref[:, lo:lo + FFN_CHUNK])
        u = _dot(h, wgu_ref[:, d_ff + lo:d_ff + lo + FFN_CHUNK])
        a = (g * (1.0 / (1.0 + jnp.exp(-g))) * u).astype(BF16)
        part = _dot(a, wd_ref[lo:lo + FFN_CHUNK, :])
        acc = part if acc is None else acc + part
    return acc


def _memkv_kernel(mem_ref, g_ref, w_ref, kv_ref):
    h = _rms(mem_ref[0], g_ref[...]).astype(BF16)
    kv_ref[0] = _dot(h, w_ref[...]).astype(BF16)


def _memkv(mem, g_mem, w_ckv):
    B, M, D = mem.shape
    N = w_ckv.shape[1]
    return pl.pallas_call(
        _memkv_kernel,
        grid=(B,),
        in_specs=[pl.BlockSpec((1, M, D), lambda b: (b, 0, 0)),
                  _resident((1, D)), _resident((D, N))],
        out_specs=pl.BlockSpec((1, M, N), lambda b: (b, 0, 0)),
        out_shape=jax.ShapeDtypeStruct((B, M, N), BF16),
        compiler_params=pltpu.CompilerParams(dimension_semantics=("arbitrary",),
                                             vmem_limit_bytes=VMEM_LIMIT),
        name="memkv",
    )(mem, g_mem.reshape(1, D), w_ckv)


def _ffn1_kernel(x_ref, g_ref, wgu_ref, wd_ref, o_ref, *, d_ff):
    x = x_ref[...]
    h = _rms(x, g_ref[...]).astype(BF16)
    o_ref[...] = x + 0.5 * _swiglu_half(h, wgu_ref, wd_ref, d_ff)


def _ffn1(x2d, g, wgu, wd):
    T, D = x2d.shape
    d_ff = wd.shape[0]
    tm = TOKEN_TILE
    return pl.pallas_call(
        functools.partial(_ffn1_kernel, d_ff=d_ff),
        grid=(T // tm,),
        in_specs=[pl.BlockSpec((tm, D), lambda i: (i, 0)),
                  _resident((1, D)), _resident((D, 2 * d_ff)), _resident((d_ff, D))],
        out_specs=pl.BlockSpec((tm, D), lambda i: (i, 0)),
        out_shape=jax.ShapeDtypeStruct((T, D), F32),
        compiler_params=pltpu.CompilerParams(dimension_semantics=("arbitrary",),
                                             vmem_limit_bytes=VMEM_LIMIT),
        name="ffn1",
    )(x2d, g.reshape(1, D), wgu, wd)


def _proj_kernel(x_ref, g_ref, w_ref, a1_ref, a4_ref, a16_ref, b_ref, scr, scr4, *, wa, wqa, wqb):
    tm = x_ref.shape[1]
    h = _rms(x_ref[0], g_ref[...]).astype(BF16)
    scale = HEAD_DIM ** -0.5
    pb = _dot(h, w_ref[:, wa:])
    b_ref[0, :, :wqb] = (pb[:, :wqb] * scale).astype(BF16)
    b_ref[0, :, wqb:] = pb[:, wqb:].astype(BF16)
    group = 2 * LANES
    for c in range(wa // LANES):
        if c % (group // LANES) == 0:
            pa = _dot(h, w_ref[:, c * LANES:c * LANES + group])
        cols = slice(c * LANES, (c + 1) * LANES)
        sub = slice((c * LANES) % group, (c * LANES) % group + LANES)
        tile = pa[:, sub] * scale if c < wqa // LANES else pa[:, sub]
        a1_ref[0, :, cols] = tile.astype(BF16)
        b = c % 2
        scr[b] = tile
        for r in range(4):
            plane = scr[b, pl.ds(r, tm // 4, stride=4), :]
            a4_ref[0, r, :, cols] = plane.astype(BF16)
            scr4[b, r] = plane
        for r in range(16):
            a16_ref[0, r, :, cols] = scr4[b, r % 4, pl.ds(r // 4, tm // 16, stride=4), :].astype(BF16)


def _proj(x1, g, w_in):
    B, S, D = x1.shape
    wa = 3 * N_HEADS_A * HEAD_DIM
    wb = 3 * N_HEADS_B * HEAD_DIM
    tm = TOKEN_TILE
    kern = functools.partial(_proj_kernel, wa=wa, wqa=N_HEADS_A * HEAD_DIM, wqb=N_HEADS_B * HEAD_DIM)
    return pl.pallas_call(
        kern,
        grid=(B, S // tm),
        in_specs=[pl.BlockSpec((1, tm, D), lambda b, i: (b, i, 0)),
                  _resident((1, D)), _resident((D, wa + wb))],
        out_specs=[pl.BlockSpec((1, tm, wa), lambda b, i: (b, i, 0)),
                   pl.BlockSpec((1, 4, tm // 4, wa), lambda b, i: (b, 0, i, 0)),
                   pl.BlockSpec((1, 16, tm // 16, wa), lambda b, i: (b, 0, i, 0)),
                   pl.BlockSpec((1, tm, wb), lambda b, i: (b, i, 0))],
        out_shape=[jax.ShapeDtypeStruct((B, S, wa), BF16),
                   jax.ShapeDtypeStruct((B, 4, S // 4, wa), BF16),
                   jax.ShapeDtypeStruct((B, 16, S // 16, wa), BF16),
                   jax.ShapeDtypeStruct((B, S, wb), BF16)],
        scratch_shapes=[pltpu.VMEM((2, tm, LANES), F32), pltpu.VMEM((2, 4, tm // 4, LANES), F32)],
        compiler_params=pltpu.CompilerParams(dimension_semantics=("arbitrary", "arbitrary"),
                                             vmem_limit_bytes=VMEM_LIMIT),
        name="proj",
    )(x1, g.reshape(1, D), w_in)


def _rel_bucket(dist):
    max_exact = REL_BUCKETS // 2
    n = jnp.maximum(dist, 0)
    nf = jnp.maximum(n, 1).astype(F32)
    large = max_exact + (jnp.log(nf / max_exact) / math.log(REL_MAX_DIST / max_exact)
                         * (REL_BUCKETS - max_exact)).astype(jnp.int32)
    large = jnp.minimum(large, REL_BUCKETS - 1)
    return jnp.where(n < max_exact, n, large)


def _bias_lookup(dist, bias_cols):
    onehot = (_rel_bucket(dist)[..., None] == jnp.arange(REL_BUCKETS)).astype(F32)
    return jnp.einsum("...b,bh->...h", onehot, bias_cols.astype(F32), precision=lax.Precision.HIGHEST)


def _dilated_tables(rel_bias):
    qb = DIL_BLOCK
    i = jnp.arange(qb)[:, None]
    j = jnp.arange(2 * qb)[None, :]
    per_pattern = []
    for window, dil in DILATED_PATTERNS:
        n_steps = window // dil
        d_band = qb + i - j
        ok_band = (d_band >= 0) & (d_band <= n_steps)
        d_own = i - j
        ok_own = (d_own >= 0) & (d_own <= n_steps)
        variants = []
        for delta, ok in ((d_band, ok_band), (d_band, ok_band & (j >= qb)), (d_own, ok_own)):
            bias = _bias_lookup(delta * dil, rel_bias[:, :N_HEADS_A])
            variants.append(jnp.where(ok[:, :, None], bias, NEG_INF))
        per_pattern.append(jnp.stack(variants))
    t = jnp.stack(per_pattern)
    t = t.transpose(4, 0, 1, 2, 3).reshape(N_HEADS_A // 2, 2, len(DILATED_PATTERNS), 3, qb, 2 * qb)
    t = t.transpose(0, 2, 3, 1, 4, 5)
    return t.reshape(N_HEADS_A // 2, len(DILATED_PATTERNS), 3, 2 * qb, 2 * qb).astype(F32)


def _moba_num_tables():
    return -(-(REL_MAX_DIST + MOBA_BLOCK - 1) // MOBA_BLOCK) + 1


def _moba_tables(rel_bias):
    bs = MOBA_BLOCK
    j = jnp.arange(bs)[:, None]
    i = jnp.arange(bs)[None, :]
    tabs = []
    for off in range(_moba_num_tables()):
        n = off * bs + i - j
        bias = _bias_lookup(n, rel_bias[:, N_HEADS_A:])
        tabs.append(jnp.where((n >= 0)[:, :, None], bias, NEG_INF))
    t = jnp.stack(tabs)
    return t.transpose(3, 0, 1, 2).reshape(N_HEADS_B // 2, 2, len(tabs), bs, bs).astype(F32)


def _dil_kernel(q1, k1, v1, q4, k4, v4, q16, k16, v16, tab_ref, o_ref, vext, oscr, sc_scr, *, seq):
    qb = DIL_BLOCK
    nblk = seq // qb
    qrefs, krefs, vrefs = (q1, q4, q16), (k1, k4, k16), (v1, v4, v16)
    dils = tuple(d for _, d in DILATED_PATTERNS)

    for p in range(3):
        v = vrefs[p][0]
        vext[p, :, :LANES] = v
        vext[p, :, LANES:] = jnp.ones_like(v)

    row2 = lax.broadcasted_iota(jnp.int32, (2 * qb, LANES), 0)
    lane2 = lax.broadcasted_iota(jnp.int32, (2 * qb, LANES), 1)
    own = (row2 < qb) == (lane2 < HEAD_DIM)
    lane1 = lax.broadcasted_iota(jnp.int32, (qb, LANES), 1)

    def band_start(blk):
        return pl.multiple_of(jnp.maximum(blk - 1, 0) * qb, qb)

    def score_block(blk, slot):
        q0 = pl.multiple_of(blk * qb, qb)
        for p in range(3):
            q = qrefs[p][0, pl.ds(q0, qb), :]
            q2 = jnp.concatenate([q, q], axis=0)
            qs = jnp.where(own, q2, jnp.zeros_like(q2))
            sc_scr[slot, p] = _dot_nt(qs, krefs[p][0, pl.ds(band_start(blk), 2 * qb), :])

    score_block(0, 0)

    def softmax_block(blk, slot):
        q0 = pl.multiple_of(blk * qb, qb)
        k0 = band_start(blk)
        for p in range(3):
            per = nblk // dils[p]
            variant = jnp.where(blk == 0, 2, jnp.where(blk % per == 0, 1, 0))
            s = sc_scr[slot, p] + tab_ref[0, p, variant]
            m = jnp.max(s, axis=-1, keepdims=True)
            pr = jnp.exp(s - m).astype(BF16)
            pv = _dot(pr, vext[p, pl.ds(k0, 2 * qb), :])
            den = pv[:, LANES:]
            on = pv[:, :LANES] / den
            lse = m + jnp.log(den)
            oscr[p, 0, pl.ds(q0, qb), :] = jnp.where(lane1 < HEAD_DIM, on[:qb], on[qb:])
            oscr[p, 1, pl.ds(q0, qb), :] = jnp.where(lane1 == 0, lse[:qb], lse[qb:])

    def block_pair(i, carry):
        b0 = 2 * i
        score_block(b0 + 1, 1)
        softmax_block(b0, 0)
        score_block(jnp.minimum(b0 + 2, nblk - 1), 0)
        softmax_block(b0 + 1, 1)
        return carry

    lax.fori_loop(0, nblk // 2, block_pair, 0)

    n16 = seq // 16
    n4 = seq // 4
    mc_rows = 128
    head_of_lane = (lax.broadcasted_iota(jnp.int32, (mc_rows, LANES), 1) >= HEAD_DIM).astype(jnp.int32)

    def merge(r, carry):
        for mc in range(n16 // mc_rows):
            m_lo = mc * mc_rows
            rows = (pl.ds(r + 16 * m_lo, mc_rows, stride=16),
                    pl.ds((r % 4) * n4 + r // 4 + 4 * m_lo, mc_rows, stride=4),
                    pl.ds(pl.multiple_of(r * n16 + m_lo, mc_rows), mc_rows))
            lse = [oscr[p, 1, rows[p], :] for p in range(3)]
            top = jnp.maximum(jnp.maximum(lse[0], lse[1]), lse[2])
            w = [jnp.exp(x - top) for x in lse]
            inv = 1.0 / (w[0] + w[1] + w[2])
            out = None
            for p in range(3):
                wl = jnp.take_along_axis(w[p] * inv, head_of_lane, axis=1)
                term = wl * oscr[p, 0, rows[p], :]
                out = term if out is None else out + term
            o_ref[0, rows[0], :] = out
        return carry

    lax.fori_loop(0, 16, merge, 0)


def _dilated(a1, a4, a16, tabs):
    B, S, wa = a1.shape
    npair = N_HEADS_A // 2
    nq = npair

    def col(off):
        return pl.BlockSpec((1, S, LANES), lambda b, p, off=off: (b, 0, off + p))

    in_specs = []
    for _ in range(3):
        in_specs += [col(0), col(nq), col(2 * nq)]
    in_specs.append(pl.BlockSpec((1,) + tabs.shape[1:], lambda b, p: (p, 0, 0, 0, 0)))
    return pl.pallas_call(
        functools.partial(_dil_kernel, seq=S),
        grid=(B, npair),
        in_specs=in_specs,
        out_specs=pl.BlockSpec((1, S, LANES), lambda b, p: (b, 0, p)),
        out_shape=jax.ShapeDtypeStruct((B, S, npair * LANES), F32),
        scratch_shapes=[pltpu.VMEM((3, S, 2 * LANES), BF16), pltpu.VMEM((3, 2, S, LANES), F32),
                        pltpu.VMEM((2, 3, 2 * DIL_BLOCK, 2 * DIL_BLOCK), F32)],
        compiler_params=pltpu.CompilerParams(dimension_semantics=("arbitrary", "arbitrary"),
                                             vmem_limit_bytes=VMEM_LIMIT),
        name="dilated",
    )(a1, a1, a1, a4, a4, a4, a16, a16, a16, tabs)


def _moba_kernel(q_ref, k_ref, v_ref, tab_ref, o_ref, vt_ext, s_scr, p_scr, g_scr, *, seq):
    bs = MOBA_BLOCK
    nb = seq // bs
    ntab = tab_ref.shape[2]

    vt = v_ref[0].astype(F32).T
    row_f = lax.broadcasted_iota(jnp.int32, vt.shape, 0)
    vt_ext[0] = jnp.where(row_f < HEAD_DIM, vt, 1.0).astype(BF16)
    vt_ext[1] = jnp.where(row_f >= HEAD_DIM, vt, 1.0).astype(BF16)

    kmean = jnp.sum(k_ref[0].astype(F32).reshape(nb, bs, LANES), axis=1) * (1.0 / bs)
    km_hi = kmean.astype(BF16)
    km_lo = (kmean - km_hi.astype(F32)).astype(BF16)

    lane_q = lax.broadcasted_iota(jnp.int32, (bs, LANES), 1)
    head_mask = (lane_q < HEAD_DIM, lane_q >= HEAD_DIM)
    blk_id = lax.broadcasted_iota(jnp.int32, (nb, bs), 0)
    row_o = lax.broadcasted_iota(jnp.int32, (LANES, bs), 0)

    def score_block(i):
        q = q_ref[0, i * bs:(i + 1) * bs, :]
        n_keys = (i + 1) * bs
        for h in range(2):
            qh = jnp.where(head_mask[h], q, jnp.zeros_like(q))
            g_scr[i % 2, h] = _dot_nt(km_hi, qh) + _dot_nt(km_lo, qh)
            s_scr[i % 2, h, :n_keys, :] = _dot_nt(k_ref[0, :n_keys, :], qh)

    def attend_block(i):
        slot = i % 2
        n_keys = (i + 1) * bs
        outs = []
        for h in range(2):
            past = blk_id < i
            g = jnp.where(past, g_scr[slot, h], NEG_INF)
            allow = blk_id == i
            for _ in range(MOBA_TOPK):
                mx = jnp.max(g, axis=0, keepdims=True)
                first = jnp.min(jnp.where(g == mx, blk_id, nb), axis=0, keepdims=True)
                pick = blk_id == first
                allow = allow | (pick & past)
                g = jnp.where(pick, NEG_INF, g)
            blk_bias = jnp.where(allow, 0.0, NEG_INF)

            mx8 = None
            for jt in range(i + 1):
                rows = slice(jt * bs, (jt + 1) * bs)
                t = s_scr[slot, h, rows, :] + tab_ref[0, h, min(i - jt, ntab - 1)] + blk_bias[jt:jt + 1, :]
                s_scr[slot, h, rows, :] = t
                t8 = jnp.max(t.reshape(bs // 8, 8, bs), axis=0)
                mx8 = t8 if mx8 is None else jnp.maximum(mx8, t8)
            m = jnp.max(mx8, axis=0, keepdims=True)
            for jt in range(i + 1):
                rows = slice(jt * bs, (jt + 1) * bs)
                p_scr[h, rows, :] = jnp.exp(s_scr[slot, h, rows, :] - m).astype(BF16)
            outs.append(_dot(vt_ext[h, :, :n_keys], p_scr[h, :n_keys, :]))
        den0 = outs[0][HEAD_DIM:HEAD_DIM + 1, :]
        den1 = outs[1][0:1, :]
        o_t = jnp.where(row_o < HEAD_DIM, outs[0] / den0, outs[1] / den1)
        o_ref[0, i * bs:(i + 1) * bs, :] = o_t.T

    score_block(0)
    for i in range(nb):
        if i + 1 < nb:
            score_block(i + 1)
        attend_block(i)


def _moba(b1, tabs):
    B, S, wb = b1.shape
    npair = N_HEADS_B // 2

    def col(off):
        return pl.BlockSpec((1, S, LANES), lambda b, p, off=off: (b, 0, off + p))

    return pl.pallas_call(
        functools.partial(_moba_kernel, seq=S),
        grid=(B, npair),
        in_specs=[col(0), col(npair), col(2 * npair),
                  pl.BlockSpec((1,) + tabs.shape[1:], lambda b, p: (p, 0, 0, 0, 0))],
        out_specs=pl.BlockSpec((1, S, LANES), lambda b, p: (b, 0, p)),
        out_shape=jax.ShapeDtypeStruct((B, S, npair * LANES), F32),
        scratch_shapes=[pltpu.VMEM((2, LANES, S), BF16), pltpu.VMEM((2, 2, S, MOBA_BLOCK), F32),
                        pltpu.VMEM((2, S, MOBA_BLOCK), BF16),
                        pltpu.VMEM((2, 2, S // MOBA_BLOCK, MOBA_BLOCK), F32)],
        compiler_params=pltpu.CompilerParams(dimension_semantics=("arbitrary", "arbitrary"),
                                             vmem_limit_bytes=VMEM_LIMIT),
        name="moba",
    )(b1, b1, b1, tabs)


def _post_kernel(x_ref, oa_ref, ob_ref, kv_ref, ga_ref, gb_ref, wout_ref, gc_ref, wcq_ref, wco_ref,
                 gf2_ref, wgu_ref, wd_ref, gfin_ref, y_ref, *, d_ff, final_norm):
    wa = oa_ref.shape[2]
    d = x_ref.shape[2]
    ya = _rms(oa_ref[0], ga_ref[...]).astype(BF16)
    yb = _rms(ob_ref[0], gb_ref[...]).astype(BF16)
    x = x_ref[0] + _dot(ya, wout_ref[:wa, :]) + _dot(yb, wout_ref[wa:, :])

    h = _rms(x, gc_ref[...]).astype(BF16)
    q = _dot(h, wcq_ref[...])
    dh = d // CROSS_HEADS
    scale = dh ** -0.5
    cross = None
    for c in range(CROSS_HEADS):
        qc = (q[:, c * dh:(c + 1) * dh] * scale).astype(BF16)
        s = _dot_nt(qc, kv_ref[0, :, c * dh:(c + 1) * dh])
        s = s - jnp.max(s, axis=-1, keepdims=True)
        e = jnp.exp(s)
        p = e / jnp.sum(e, axis=-1, keepdims=True)
        oc = _dot(p.astype(BF16), kv_ref[0, :, d + c * dh:d + (c + 1) * dh]).astype(BF16)
        part = _dot(oc, wco_ref[c * dh:(c + 1) * dh, :])
        cross = part if cross is None else cross + part
    x = x + cross

    h = _rms(x, gf2_ref[...]).astype(BF16)
    x = x + 0.5 * _swiglu_half(h, wgu_ref, wd_ref, d_ff)
    y_ref[0] = _rms(x, gfin_ref[...]) if final_norm else x


def _post(x1, oa, ob, kv, g_out_a, g_out_b, w_out, g_cross, w_cq, w_co, g_ffn2, wgu, wd, g_final, final_norm):
    B, S, D = x1.shape
    wa, wb = oa.shape[2], ob.shape[2]
    M = kv.shape[1]
    d_ff = wd.shape[0]
    tm = TOKEN_TILE
    tok = lambda w: pl.BlockSpec((1, tm, w), lambda b, i: (b, i, 0))
    return pl.pallas_call(
        functools.partial(_post_kernel, d_ff=d_ff, final_norm=final_norm),
        grid=(B, S // tm),
        in_specs=[tok(D), tok(wa), tok(wb),
                  pl.BlockSpec((1, M, 2 * D), lambda b, i: (b, 0, 0)),
                  _resident((1, wa)), _resident((1, wb)), _resident((wa + wb, D)),
                  _resident((1, D)), _resident((D, D)), _resident((D, D)),
                  _resident((1, D)), _resident((D, 2 * d_ff)), _resident((d_ff, D)),
                  _resident((1, D))],
        out_specs=tok(D),
        out_shape=jax.ShapeDtypeStruct((B, S, D), F32),
        compiler_params=pltpu.CompilerParams(dimension_semantics=("arbitrary", "arbitrary"),
                                             vmem_limit_bytes=VMEM_LIMIT),
        name="post",
    )(x1, oa, ob, kv, g_out_a.reshape(1, wa), g_out_b.reshape(1, wb), w_out, g_cross.reshape(1, D),
      w_cq, w_co, g_ffn2.reshape(1, D), wgu, wd, g_final.reshape(1, D))


def kernel(x, mem, g_ffn1, w_ffn1_gu, w_ffn1_down, g_mix, w_in, rel_bias, g_out_a, g_out_b, w_out,
           g_cross, g_mem, w_cq, w_ckv, w_co, g_ffn2, w_ffn2_gu, w_ffn2_down, g_final):
    B, S, D = x.shape
    depth = g_ffn1.shape[0]
    dil_tabs = _dilated_tables(rel_bias)
    moba_tabs = _moba_tables(rel_bias)
    bf = lambda w: w.astype(BF16)
    y = None
    for l in range(depth):
        kv = _memkv(mem, g_mem[l], bf(w_ckv[l]))
        x1 = _ffn1(x.reshape(B * S, D), g_ffn1[l], bf(w_ffn1_gu[l]), bf(w_ffn1_down[l])).reshape(B, S, D)
        a1, a4, a16, b1 = _proj(x1, g_mix[l], bf(w_in[l]))
        wa = a1.shape[2]
        oa = _dilated(a1, a4.reshape(B, S, wa), a16.reshape(B, S, wa), dil_tabs)
        ob = _moba(b1, moba_tabs)
        y = _post(x1, oa, ob, kv, g_out_a[l], g_out_b[l], bf(w_out[l]), g_cross[l], bf(w_cq[l]),
                  bf(w_co[l]), g_ffn2[l], bf(w_ffn2_gu[l]), bf(w_ffn2_down[l]), g_final,
                  final_norm=(l == depth - 1))
        x = y
    return y
```

```python
import functools
import math

import numpy as np
import jax
import jax.numpy as jnp
from jax import lax
from jax.experimental import pallas as pl
from jax.experimental.pallas import tpu as pltpu

F32 = jnp.float32
BF16 = jnp.bfloat16

HEAD_DIM = 64
LANES = 128
N_HEADS_A = 12
N_HEADS_B = 4
DILATED_PATTERNS = ((128, 1), (512, 4), (2048, 16))
DIL_BLOCK = 128
MOBA_BLOCK = 256
MOBA_TOPK = 3
CROSS_HEADS = 4
REL_BUCKETS = 32
REL_MAX_DIST = 2048
NORM_EPS = 1e-6
NEG_INF = -jnp.inf
VMEM_LIMIT = 56 * 1024 * 1024

FFN_CHUNK = 256
TOKEN_TILE = 512


def _rms(x, g):
    xf = x.astype(F32)
    return xf * lax.rsqrt(jnp.mean(xf * xf, axis=-1, keepdims=True) + NORM_EPS) * g


def _dot(a, b):
    return jnp.dot(a, b, preferred_element_type=F32)


def _dot_nt(a, b):
    return lax.dot_general(a, b, (((1,), (1,)), ((), ())), preferred_element_type=F32)


def _resident(shape):
    nd = len(shape)
    return pl.BlockSpec(shape, lambda *_: (0,) * nd, pipeline_mode=pl.Buffered(1))


def _swiglu_half(h, wgu_ref, wd_ref, d_ff):
    acc = None
    for c in range(d_ff // FFN_CHUNK):
        lo = c * FFN_CHUNK
        g = _dot(h, wgu_ref[:, lo:lo + FFN_CHUNK])
        u = _dot(h, wgu_ref[:, d_ff + lo:d_ff + lo + FFN_CHUNK])
        a = (g * (1.0 / (1.0 + jnp.exp(-g))) * u).astype(BF16)
        part = _dot(a, wd_ref[lo:lo + FFN_CHUNK, :])
        acc = part if acc is None else acc + part
    return acc


def _memkv_kernel(mem_ref, g_ref, w_ref, kv_ref):
    h = _rms(mem_ref[0], g_ref[...]).astype(BF16)
    kv_ref[0] = _dot(h, w_ref[...]).astype(BF16)


def _memkv(mem, g_mem, w_ckv):
    B, M, D = mem.shape
    N = w_ckv.shape[1]
    return pl.pallas_call(
        _memkv_kernel,
        grid=(B,),
        in_specs=[pl.BlockSpec((1, M, D), lambda b: (b, 0, 0)),
                  _resident((1, D)), _resident((D, N))],
        out_specs=pl.BlockSpec((1, M, N), lambda b: (b, 0, 0)),
        out_shape=jax.ShapeDtypeStruct((B, M, N), BF16),
        compiler_params=pltpu.CompilerParams(dimension_semantics=("arbitrary",),
                                             vmem_limit_bytes=VMEM_LIMIT),
        name="memkv",
    )(mem, g_mem.reshape(1, D), w_ckv)


def _ffn1_kernel(x_ref, g_ref, wgu_ref, wd_ref, o_ref, *, d_ff):
    x = x_ref[...]
    h = _rms(x, g_ref[...]).astype(BF16)
    o_ref[...] = x + 0.5 * _swiglu_half(h, wgu_ref, wd_ref, d_ff)


def _ffn1(x2d, g, wgu, wd):
    T, D = x2d.shape
    d_ff = wd.shape[0]
    tm = TOKEN_TILE
    return pl.pallas_call(
        functools.partial(_ffn1_kernel, d_ff=d_ff),
        grid=(T // tm,),
        in_specs=[pl.BlockSpec((tm, D), lambda i: (i, 0)),
                  _resident((1, D)), _resident((D, 2 * d_ff)), _resident((d_ff, D))],
        out_specs=pl.BlockSpec((tm, D), lambda i: (i, 0)),
        out_shape=jax.ShapeDtypeStruct((T, D), F32),
        compiler_params=pltpu.CompilerParams(dimension_semantics=("arbitrary",),
                                             vmem_limit_bytes=VMEM_LIMIT),
        name="ffn1",
    )(x2d, g.reshape(1, D), wgu, wd)


def _proj_kernel(x_ref, g_ref, w_ref, a1_ref, a4_ref, a16_ref, b_ref, scr, scr4, *, wa, wqa, wqb):
    tm = x_ref.shape[1]
    h = _rms(x_ref[0], g_ref[...]).astype(BF16)
    scale = HEAD_DIM ** -0.5
    pb = _dot(h, w_ref[:, wa:])
    b_ref[0, :, :wqb] = (pb[:, :wqb] * scale).astype(BF16)
    b_ref[0, :, wqb:] = pb[:, wqb:].astype(BF16)
    group = 2 * LANES
    for c in range(wa // LANES):
        if c % (group // LANES) == 0:
            pa = _dot(h, w_ref[:, c * LANES:c * LANES + group])
        cols = slice(c * LANES, (c + 1) * LANES)
        sub = slice((c * LANES) % group, (c * LANES) % group + LANES)
        tile = pa[:, sub] * scale if c < wqa // LANES else pa[:, sub]
        a1_ref[0, :, cols] = tile.astype(BF16)
        b = c % 2
        scr[b] = tile
        for r in range(4):
            plane = scr[b, pl.ds(r, tm // 4, stride=4), :]
            a4_ref[0, r, :, cols] = plane.astype(BF16)
            scr4[b, r] = plane
        for r in range(16):
            a16_ref[0, r, :, cols] = scr4[b, r % 4, pl.ds(r // 4, tm // 16, stride=4), :].astype(BF16)


def _proj(x1, g, w_in):
    B, S, D = x1.shape
    wa = 3 * N_HEADS_A * HEAD_DIM
    wb = 3 * N_HEADS_B * HEAD_DIM
    tm = TOKEN_TILE
    kern = functools.partial(_proj_kernel, wa=wa, wqa=N_HEADS_A * HEAD_DIM, wqb=N_HEADS_B * HEAD_DIM)
    return pl.pallas_call(
        kern,
        grid=(B, S // tm),
        in_specs=[pl.BlockSpec((1, tm, D), lambda b, i: (b, i, 0)),
                  _resident((1, D)), _resident((D, wa + wb))],
        out_specs=[pl.BlockSpec((1, tm, wa), lambda b, i: (b, i, 0)),
                   pl.BlockSpec((1, 4, tm // 4, wa), lambda b, i: (b, 0, i, 0)),
                   pl.BlockSpec((1, 16, tm // 16, wa), lambda b, i: (b, 0, i, 0)),
                   pl.BlockSpec((1, tm, wb), lambda b, i: (b, i, 0))],
        out_shape=[jax.ShapeDtypeStruct((B, S, wa), BF16),
                   jax.ShapeDtypeStruct((B, 4, S // 4, wa), BF16),
                   jax.ShapeDtypeStruct((B, 16, S // 16, wa), BF16),
                   jax.ShapeDtypeStruct((B, S, wb), BF16)],
        scratch_shapes=[pltpu.VMEM((2, tm, LANES), F32), pltpu.VMEM((2, 4, tm // 4, LANES), F32)],
        compiler_params=pltpu.CompilerParams(dimension_semantics=("arbitrary", "arbitrary"),
                                             vmem_limit_bytes=VMEM_LIMIT),
        name="proj",
    )(x1, g.reshape(1, D), w_in)


def _rel_bucket(dist):
    max_exact = REL_BUCKETS // 2
    n = jnp.maximum(dist, 0)
    nf = jnp.maximum(n, 1).astype(F32)
    large = max_exact + (jnp.log(nf / max_exact) / math.log(REL_MAX_DIST / max_exact)
                         * (REL_BUCKETS - max_exact)).astype(jnp.int32)
    large = jnp.minimum(large, REL_BUCKETS - 1)
    return jnp.where(n < max_exact, n, large)


def _bias_lookup(dist, bias_cols, spec):
    onehot = (_rel_bucket(dist)[..., None] == jnp.arange(REL_BUCKETS)).astype(F32)
    return jnp.einsum(spec, onehot, bias_cols.astype(F32), precision=lax.Precision.HIGHEST)


def _dilated_tables(rel_bias):
    qb = DIL_BLOCK
    npat = len(DILATED_PATTERNS)
    dil = jnp.array([d for _, d in DILATED_PATTERNS], jnp.int32)[:, None, None, None]
    n_steps = jnp.array([w // d for w, d in DILATED_PATTERNS], jnp.int32)[:, None, None, None]
    variant = jnp.arange(3)[None, :, None, None]
    i = jnp.arange(qb)[None, None, :, None]
    j = jnp.arange(2 * qb)[None, None, None, :]
    delta = jnp.where(variant == 2, i - j, qb + i - j)
    ok = (delta >= 0) & (delta <= n_steps) & ((variant != 1) | (j >= qb))
    bias_cols = rel_bias[:, :N_HEADS_A].reshape(REL_BUCKETS, N_HEADS_A // 2, 2)
    t = _bias_lookup(delta * dil, bias_cols, "pvijb,bnk->npvkij")
    t = jnp.where(ok[None, :, :, None, :, :], t, NEG_INF)
    return t.reshape(N_HEADS_A // 2, npat, 3, 2 * qb, 2 * qb)


def _moba_num_tables():
    return -(-(REL_MAX_DIST + MOBA_BLOCK - 1) // MOBA_BLOCK) + 1


def _moba_tables(rel_bias):
    bs = MOBA_BLOCK
    off = jnp.arange(_moba_num_tables())[:, None, None]
    j = jnp.arange(bs)[None, :, None]
    i = jnp.arange(bs)[None, None, :]
    n = off * bs + i - j
    bias_cols = rel_bias[:, N_HEADS_A:].reshape(REL_BUCKETS, N_HEADS_B // 2, 2)
    t = _bias_lookup(n, bias_cols, "vijb,bnk->nkvij")
    return jnp.where((n >= 0)[None, None], t, NEG_INF)


def _dil_kernel(q1, k1, v1, q4, k4, v4, q16, k16, v16, tab_ref, o_ref, vext, oscr, sc_scr, *, seq):
    qb = DIL_BLOCK
    nblk = seq // qb
    qrefs, krefs, vrefs = (q1, q4, q16), (k1, k4, k16), (v1, v4, v16)
    dils = tuple(d for _, d in DILATED_PATTERNS)

    for p in range(3):
        v = vrefs[p][0]
        vext[p, :, :LANES] = v
        vext[p, :, LANES:] = jnp.ones_like(v)

    row2 = lax.broadcasted_iota(jnp.int32, (2 * qb, LANES), 0)
    lane2 = lax.broadcasted_iota(jnp.int32, (2 * qb, LANES), 1)
    own = (row2 < qb) == (lane2 < HEAD_DIM)
    lane1 = lax.broadcasted_iota(jnp.int32, (qb, LANES), 1)

    def band_start(blk):
        return pl.multiple_of(jnp.maximum(blk - 1, 0) * qb, qb)

    def score_block(blk, slot):
        q0 = pl.multiple_of(blk * qb, qb)
        for p in range(3):
            q = qrefs[p][0, pl.ds(q0, qb), :]
            q2 = jnp.concatenate([q, q], axis=0)
            qs = jnp.where(own, q2, jnp.zeros_like(q2))
            sc_scr[slot, p] = _dot_nt(qs, krefs[p][0, pl.ds(band_start(blk), 2 * qb), :])

    score_block(0, 0)

    def softmax_block(blk, slot):
        q0 = pl.multiple_of(blk * qb, qb)
        k0 = band_start(blk)
        for p in range(3):
            per = nblk // dils[p]
            variant = jnp.where(blk == 0, 2, jnp.where(blk % per == 0, 1, 0))
            s = sc_scr[slot, p] + tab_ref[0, p, variant]
            m = jnp.max(s, axis=-1, keepdims=True)
            pr = jnp.exp(s - m).astype(BF16)
            pv = _dot(pr, vext[p, pl.ds(k0, 2 * qb), :])
            den = pv[:, LANES:]
            on = pv[:, :LANES] / den
            lse = m + jnp.log(den)
            oscr[p, 0, pl.ds(q0, qb), :] = jnp.where(lane1 < HEAD_DIM, on[:qb], on[qb:])
            oscr[p, 1, pl.ds(q0, qb), :] = jnp.where(lane1 == 0, lse[:qb], lse[qb:])

    def block_pair(i, carry):
        b0 = 2 * i
        score_block(b0 + 1, 1)
        softmax_block(b0, 0)
        score_block(jnp.minimum(b0 + 2, nblk - 1), 0)
        softmax_block(b0 + 1, 1)
        return carry

    lax.fori_loop(0, nblk // 2, block_pair, 0)

    n16 = seq // 16
    n4 = seq // 4
    mc_rows = 128
    head_of_lane = (lax.broadcasted_iota(jnp.int32, (mc_rows, LANES), 1) >= HEAD_DIM).astype(jnp.int32)

    def merge(r, carry):
        for mc in range(n16 // mc_rows):
            m_lo = mc * mc_rows
            rows = (pl.ds(r + 16 * m_lo, mc_rows, stride=16),
                    pl.ds((r % 4) * n4 + r // 4 + 4 * m_lo, mc_rows, stride=4),
                    pl.ds(pl.multiple_of(r * n16 + m_lo, mc_rows), mc_rows))
            lse = [oscr[p, 1, rows[p], :] for p in range(3)]
            top = jnp.maximum(jnp.maximum(lse[0], lse[1]), lse[2])
            w = [jnp.exp(x - top) for x in lse]
            inv = 1.0 / (w[0] + w[1] + w[2])
            out = None
            for p in range(3):
                wl = jnp.take_along_axis(w[p] * inv, head_of_lane, axis=1)
                term = wl * oscr[p, 0, rows[p], :]
                out = term if out is None else out + term
            o_ref[0, rows[0], :] = out
        return carry

    lax.fori_loop(0, 16, merge, 0)


def _dilated(a1, a4, a16, tabs):
    B, S, wa = a1.shape
    npair = N_HEADS_A // 2
    nq = npair

    def col(off):
        return pl.BlockSpec((1, S, LANES), lambda b, p, off=off: (b, 0, off + p))

    in_specs = []
    for _ in range(3):
        in_specs += [col(0), col(nq), col(2 * nq)]
    in_specs.append(pl.BlockSpec((1,) + tabs.shape[1:], lambda b, p: (p, 0, 0, 0, 0)))
    return pl.pallas_call(
        functools.partial(_dil_kernel, seq=S),
        grid=(B, npair),
        in_specs=in_specs,
        out_specs=pl.BlockSpec((1, S, LANES), lambda b, p: (b, 0, p)),
        out_shape=jax.ShapeDtypeStruct((B, S, npair * LANES), F32),
        scratch_shapes=[pltpu.VMEM((3, S, 2 * LANES), BF16), pltpu.VMEM((3, 2, S, LANES), F32),
                        pltpu.VMEM((2, 3, 2 * DIL_BLOCK, 2 * DIL_BLOCK), F32)],
        compiler_params=pltpu.CompilerParams(dimension_semantics=("arbitrary", "arbitrary"),
                                             vmem_limit_bytes=VMEM_LIMIT),
        name="dilated",
    )(a1, a1, a1, a4, a4, a4, a16, a16, a16, tabs)


def _moba_kernel(q_ref, k_ref, v_ref, tab_ref, o_ref, vt_ext, s_scr, p_scr, g_scr, *, seq):
    bs = MOBA_BLOCK
    nb = seq // bs
    ntab = tab_ref.shape[2]

    vt = v_ref[0].astype(F32).T
    row_f = lax.broadcasted_iota(jnp.int32, vt.shape, 0)
    vt_ext[0] = jnp.where(row_f < HEAD_DIM, vt, 1.0).astype(BF16)
    vt_ext[1] = jnp.where(row_f >= HEAD_DIM, vt, 1.0).astype(BF16)

    kmean = jnp.sum(k_ref[0].astype(F32).reshape(nb, bs, LANES), axis=1) * (1.0 / bs)
    km_hi = kmean.astype(BF16)
    km_lo = (kmean - km_hi.astype(F32)).astype(BF16)

    lane_q = lax.broadcasted_iota(jnp.int32, (bs, LANES), 1)
    head_mask = (lane_q < HEAD_DIM, lane_q >= HEAD_DIM)
    blk_id = lax.broadcasted_iota(jnp.int32, (nb, bs), 0)
    row_o = lax.broadcasted_iota(jnp.int32, (LANES, bs), 0)

    def score_block(i):
        q = q_ref[0, i * bs:(i + 1) * bs, :]
        n_keys = (i + 1) * bs
        for h in range(2):
            qh = jnp.where(head_mask[h], q, jnp.zeros_like(q))
            g_scr[i % 2, h] = _dot_nt(km_hi, qh) + _dot_nt(km_lo, qh)
            s_scr[i % 2, h, :n_keys, :] = _dot_nt(k_ref[0, :n_keys, :], qh)

    def attend_block(i):
        slot = i % 2
        n_keys = (i + 1) * bs
        outs = []
        for h in range(2):
            past = blk_id < i
            g = jnp.where(past, g_scr[slot, h], NEG_INF)
            allow = blk_id == i
            for _ in range(MOBA_TOPK):
                mx = jnp.max(g, axis=0, keepdims=True)
                first = jnp.min(jnp.where(g == mx, blk_id, nb), axis=0, keepdims=True)
                pick = blk_id == first
                allow = allow | (pick & past)
                g = jnp.where(pick, NEG_INF, g)
            blk_bias = jnp.where(allow, 0.0, NEG_INF)

            mx8 = None
            for jt in range(i + 1):
                rows = slice(jt * bs, (jt + 1) * bs)
                t = s_scr[slot, h, rows, :] + tab_ref[0, h, min(i - jt, ntab - 1)] + blk_bias[jt:jt + 1, :]
                s_scr[slot, h, rows, :] = t
                t8 = jnp.max(t.reshape(bs // 8, 8, bs), axis=0)
                mx8 = t8 if mx8 is None else jnp.maximum(mx8, t8)
            m = jnp.max(mx8, axis=0, keepdims=True)
            for jt in range(i + 1):
                rows = slice(jt * bs, (jt + 1) * bs)
                p_scr[h, rows, :] = jnp.exp(s_scr[slot, h, rows, :] - m).astype(BF16)
            outs.append(_dot(vt_ext[h, :, :n_keys], p_scr[h, :n_keys, :]))
        den0 = outs[0][HEAD_DIM:HEAD_DIM + 1, :]
        den1 = outs[1][0:1, :]
        o_t = jnp.where(row_o < HEAD_DIM, outs[0] / den0, outs[1] / den1)
        o_ref[0, i * bs:(i + 1) * bs, :] = o_t.T

    score_block(0)
    for i in range(nb):
        if i + 1 < nb:
            score_block(i + 1)
        attend_block(i)


def _moba(b1, tabs):
    B, S, wb = b1.shape
    npair = N_HEADS_B // 2

    def col(off):
        return pl.BlockSpec((1, S, LANES), lambda b, p, off=off: (b, 0, off + p))

    return pl.pallas_call(
        functools.partial(_moba_kernel, seq=S),
        grid=(B, npair),
        in_specs=[col(0), col(npair), col(2 * npair),
                  pl.BlockSpec((1,) + tabs.shape[1:], lambda b, p: (p, 0, 0, 0, 0))],
        out_specs=pl.BlockSpec((1, S, LANES), lambda b, p: (b, 0, p)),
        out_shape=jax.ShapeDtypeStruct((B, S, npair * LANES), F32),
        scratch_shapes=[pltpu.VMEM((2, LANES, S), BF16), pltpu.VMEM((2, 2, S, MOBA_BLOCK), F32),
                        pltpu.VMEM((2, S, MOBA_BLOCK), BF16),
                        pltpu.VMEM((2, 2, S // MOBA_BLOCK, MOBA_BLOCK), F32)],
        compiler_params=pltpu.CompilerParams(dimension_semantics=("arbitrary", "arbitrary"),
                                             vmem_limit_bytes=VMEM_LIMIT),
        name="moba",
    )(b1, b1, b1, tabs)


def _post_kernel(x_ref, oa_ref, ob_ref, kv_ref, ga_ref, gb_ref, wout_ref, gc_ref, wcq_ref, wco_ref,
                 gf2_ref, wgu_ref, wd_ref, gfin_ref, y_ref, *, d_ff, final_norm):
    wa = oa_ref.shape[2]
    tm, d = x_ref.shape[1], x_ref.shape[2]
    dh = d // CROSS_HEADS
    scale = dh ** -0.5
    halves = [slice(k * tm // 2, (k + 1) * tm // 2) for k in range(2)]
    ya = [_rms(oa_ref[0, r, :], ga_ref[...]).astype(BF16) for r in halves]
    yb = [_rms(ob_ref[0, r, :], gb_ref[...]).astype(BF16) for r in halves]
    xs = [x_ref[0, r, :] + _dot(ya[k], wout_ref[:wa, :]) + _dot(yb[k], wout_ref[wa:, :])
          for k, r in enumerate(halves)]
    hs = [_rms(xk, gc_ref[...]).astype(BF16) for xk in xs]
    qs = [_dot(hk, wcq_ref[...]) for hk in hs]
    cross = [None, None]
    for c in range(CROSS_HEADS):
        cols = slice(c * dh, (c + 1) * dh)
        ss = [_dot_nt((qk[:, cols] * scale).astype(BF16), kv_ref[0, :, cols]) for qk in qs]
        ocs = []
        for s in ss:
            e = jnp.exp(s - jnp.max(s, axis=-1, keepdims=True))
            p = e / jnp.sum(e, axis=-1, keepdims=True)
            ocs.append(_dot(p.astype(BF16), kv_ref[0, :, d + c * dh:d + (c + 1) * dh]).astype(BF16))
        for k in range(2):
            part = _dot(ocs[k], wco_ref[cols, :])
            cross[k] = part if cross[k] is None else cross[k] + part
    x = jnp.concatenate([xs[k] + cross[k] for k in range(2)], axis=0)

    h = _rms(x, gf2_ref[...]).astype(BF16)
    x = x + 0.5 * _swiglu_half(h, wgu_ref, wd_ref, d_ff)
    y_ref[0] = _rms(x, gfin_ref[...]) if final_norm else x


def _post(x1, oa, ob, kv, g_out_a, g_out_b, w_out, g_cross, w_cq, w_co, g_ffn2, wgu, wd, g_final, final_norm):
    B, S, D = x1.shape
    wa, wb = oa.shape[2], ob.shape[2]
    M = kv.shape[1]
    d_ff = wd.shape[0]
    tm = TOKEN_TILE
    tok = lambda w: pl.BlockSpec((1, tm, w), lambda b, i: (b, i, 0))
    return pl.pallas_call(
        functools.partial(_post_kernel, d_ff=d_ff, final_norm=final_norm),
        grid=(B, S // tm),
        in_specs=[tok(D), tok(wa), tok(wb),
                  pl.BlockSpec((1, M, 2 * D), lambda b, i: (b, 0, 0)),
                  _resident((1, wa)), _resident((1, wb)), _resident((wa + wb, D)),
                  _resident((1, D)), _resident((D, D)), _resident((D, D)),
                  _resident((1, D)), _resident((D, 2 * d_ff)), _resident((d_ff, D)),
                  _resident((1, D))],
        out_specs=tok(D),
        out_shape=jax.ShapeDtypeStruct((B, S, D), F32),
        compiler_params=pltpu.CompilerParams(dimension_semantics=("arbitrary", "arbitrary"),
                                             vmem_limit_bytes=VMEM_LIMIT),
        name="post",
    )(x1, oa, ob, kv, g_out_a.reshape(1, wa), g_out_b.reshape(1, wb), w_out, g_cross.reshape(1, D),
      w_cq, w_co, g_ffn2.reshape(1, D), wgu, wd, g_final.reshape(1, D))


def kernel(x, mem, g_ffn1, w_ffn1_gu, w_ffn1_down, g_mix, w_in, rel_bias, g_out_a, g_out_b, w_out,
           g_cross, g_mem, w_cq, w_ckv, w_co, g_ffn2, w_ffn2_gu, w_ffn2_down, g_final):
    B, S, D = x.shape
    depth = g_ffn1.shape[0]
    dil_tabs = _dilated_tables(rel_bias)
    moba_tabs = _moba_tables(rel_bias)
    bf = lambda w: w.astype(BF16)
    y = None
    for l in range(depth):
        kv = _memkv(mem, g_mem[l], bf(w_ckv[l]))
        x1 = _ffn1(x.reshape(B * S, D), g_ffn1[l], bf(w_ffn1_gu[l]), bf(w_ffn1_down[l])).reshape(B, S, D)
        a1, a4, a16, b1 = _proj(x1, g_mix[l], bf(w_in[l]))
        wa = a1.shape[2]
        oa = _dilated(a1, a4.reshape(B, S, wa), a16.reshape(B, S, wa), dil_tabs)
        ob = _moba(b1, moba_tabs)
        y = _post(x1, oa, ob, kv, g_out_a[l], g_out_b[l], bf(w_out[l]), g_cross[l], bf(w_cq[l]),
                  bf(w_co[l]), g_ffn2[l], bf(w_ffn2_gu[l]), bf(w_ffn2_down[l]), g_final,
                  final_norm=(l == depth - 1))
        x = y
    return y
```

```python
import functools
import math

import numpy as np
import jax
import jax.numpy as jnp
from jax import lax
from jax.experimental import pallas as pl
from jax.experimental.pallas import tpu as pltpu

F32 = jnp.float32
BF16 = jnp.bfloat16

HEAD_DIM = 64
LANES = 128
N_HEADS_A = 12
N_HEADS_B = 4
DILATED_PATTERNS = ((128, 1), (512, 4), (2048, 16))
DIL_BLOCK = 128
MOBA_BLOCK = 256
MOBA_TOPK = 3
CROSS_HEADS = 4
REL_BUCKETS = 32
REL_MAX_DIST = 2048
NORM_EPS = 1e-6
NEG_INF = -jnp.inf
VMEM_LIMIT = 56 * 1024 * 1024

FFN_CHUNK = 256
TOKEN_TILE = 512


def _rms(x, g):
    xf = x.astype(F32)
    return xf * lax.rsqrt(jnp.mean(xf * xf, axis=-1, keepdims=True) + NORM_EPS) * g


def _dot(a, b):
    return jnp.dot(a, b, preferred_element_type=F32)


def _dot_nt(a, b):
    return lax.dot_general(a, b, (((1,), (1,)), ((), ())), preferred_element_type=F32)


def _resident(shape):
    nd = len(shape)
    return pl.BlockSpec(shape, lambda *_: (0,) * nd, pipeline_mode=pl.Buffered(1))


def _swiglu_half(h, wgu_ref, wd_ref, d_ff):
    acc = None
    for c in range(d_ff // FFN_CHUNK):
        lo = c * FFN_CHUNK
        g = _dot(h, wgu_ref[:, lo:lo + FFN_CHUNK])
        u = _dot(h, wgu_ref[:, d_ff + lo:d_ff + lo + FFN_CHUNK])
        a = (g * (1.0 / (1.0 + jnp.exp(-g))) * u).astype(BF16)
        part = _dot(a, wd_ref[lo:lo + FFN_CHUNK, :])
        acc = part if acc is None else acc + part
    return acc


def _memkv_kernel(mem_ref, g_ref, w_ref, kv_ref):
    h = _rms(mem_ref[0], g_ref[...]).astype(BF16)
    kv_ref[0] = _dot(h, w_ref[...]).astype(BF16)


def _memkv(mem, g_mem, w_ckv):
    B, M, D = mem.shape
    N = w_ckv.shape[1]
    return pl.pallas_call(
        _memkv_kernel,
        grid=(B,),
        in_specs=[pl.BlockSpec((1, M, D), lambda b: (b, 0, 0)),
                  _resident((1, D)), _resident((D, N))],
        out_specs=pl.BlockSpec((1, M, N), lambda b: (b, 0, 0)),
        out_shape=jax.ShapeDtypeStruct((B, M, N), BF16),
        compiler_params=pltpu.CompilerParams(dimension_semantics=("arbitrary",),
                                             vmem_limit_bytes=VMEM_LIMIT),
        name="memkv",
    )(mem, g_mem.reshape(1, D), w_ckv)


def _ffn1_kernel(x_ref, g_ref, wgu_ref, wd_ref, o_ref, *, d_ff):
    x = x_ref[...]
    h = _rms(x, g_ref[...]).astype(BF16)
    o_ref[...] = x + 0.5 * _swiglu_half(h, wgu_ref, wd_ref, d_ff)


def _ffn1(x2d, g, wgu, wd):
    T, D = x2d.shape
    d_ff = wd.shape[0]
    tm = TOKEN_TILE
    return pl.pallas_call(
        functools.partial(_ffn1_kernel, d_ff=d_ff),
        grid=(T // tm,),
        in_specs=[pl.BlockSpec((tm, D), lambda i: (i, 0)),
                  _resident((1, D)), _resident((D, 2 * d_ff)), _resident((d_ff, D))],
        out_specs=pl.BlockSpec((tm, D), lambda i: (i, 0)),
        out_shape=jax.ShapeDtypeStruct((T, D), F32),
        compiler_params=pltpu.CompilerParams(dimension_semantics=("arbitrary",),
                                             vmem_limit_bytes=VMEM_LIMIT),
        name="ffn1",
    )(x2d, g.reshape(1, D), wgu, wd)


def _proj_kernel(x_ref, g_ref, w_ref, a1_ref, a4_ref, a16_ref, b_ref, scr, scr4, *, wa, wqa, wqb):
    tm = x_ref.shape[1]
    h = _rms(x_ref[0], g_ref[...]).astype(BF16)
    scale = HEAD_DIM ** -0.5
    pb = _dot(h, w_ref[:, wa:])
    for c in range(pb.shape[1] // LANES):
        tile = pb[:, c * LANES:(c + 1) * LANES]
        b_ref[0, c] = (tile * scale if c < wqb // LANES else tile).astype(BF16)
    group = 2 * LANES
    for c in range(wa // LANES):
        if c % (group // LANES) == 0:
            pa = _dot(h, w_ref[:, c * LANES:c * LANES + group])
        sub = slice((c * LANES) % group, (c * LANES) % group + LANES)
        tile = pa[:, sub] * scale if c < wqa // LANES else pa[:, sub]
        a1_ref[0, c] = tile.astype(BF16)
        b = c % 2
        scr[b] = tile
        for r in range(4):
            plane = scr[b, pl.ds(r, tm // 4, stride=4), :]
            a4_ref[0, c, r] = plane.astype(BF16)
            scr4[b, r] = plane
        for r in range(16):
            a16_ref[0, c, r] = scr4[b, r % 4, pl.ds(r // 4, tm // 16, stride=4), :].astype(BF16)


def _proj(x1, g, w_in):
    B, S, D = x1.shape
    wa = 3 * N_HEADS_A * HEAD_DIM
    wb = 3 * N_HEADS_B * HEAD_DIM
    ta, tb = wa // LANES, wb // LANES
    tm = TOKEN_TILE
    kern = functools.partial(_proj_kernel, wa=wa, wqa=N_HEADS_A * HEAD_DIM, wqb=N_HEADS_B * HEAD_DIM)
    a1, a4, a16, b1 = pl.pallas_call(
        kern,
        grid=(B, S // tm),
        in_specs=[pl.BlockSpec((1, tm, D), lambda b, i: (b, i, 0)),
                  _resident((1, D)), _resident((D, wa + wb))],
        out_specs=[pl.BlockSpec((1, ta, tm, LANES), lambda b, i: (b, 0, i, 0)),
                   pl.BlockSpec((1, ta, 4, tm // 4, LANES), lambda b, i: (b, 0, 0, i, 0)),
                   pl.BlockSpec((1, ta, 16, tm // 16, LANES), lambda b, i: (b, 0, 0, i, 0)),
                   pl.BlockSpec((1, tb, tm, LANES), lambda b, i: (b, 0, i, 0))],
        out_shape=[jax.ShapeDtypeStruct((B, ta, S, LANES), BF16),
                   jax.ShapeDtypeStruct((B, ta, 4, S // 4, LANES), BF16),
                   jax.ShapeDtypeStruct((B, ta, 16, S // 16, LANES), BF16),
                   jax.ShapeDtypeStruct((B, tb, S, LANES), BF16)],
        scratch_shapes=[pltpu.VMEM((2, tm, LANES), F32), pltpu.VMEM((2, 4, tm // 4, LANES), F32)],
        compiler_params=pltpu.CompilerParams(dimension_semantics=("arbitrary", "arbitrary"),
                                             vmem_limit_bytes=VMEM_LIMIT),
        name="proj",
    )(x1, g.reshape(1, D), w_in)
    return a1, a4.reshape(a1.shape), a16.reshape(a1.shape), b1


def _rel_bucket(dist):
    max_exact = REL_BUCKETS // 2
    n = jnp.maximum(dist, 0)
    nf = jnp.maximum(n, 1).astype(F32)
    large = max_exact + (jnp.log(nf / max_exact) / math.log(REL_MAX_DIST / max_exact)
                         * (REL_BUCKETS - max_exact)).astype(jnp.int32)
    large = jnp.minimum(large, REL_BUCKETS - 1)
    return jnp.where(n < max_exact, n, large)


def _bias_lookup(dist, bias_cols, spec):
    onehot = (_rel_bucket(dist)[..., None] == jnp.arange(REL_BUCKETS)).astype(F32)
    return jnp.einsum(spec, onehot, bias_cols.astype(F32), precision=lax.Precision.HIGHEST)


def _dilated_tables(rel_bias):
    qb = DIL_BLOCK
    npat = len(DILATED_PATTERNS)
    dil = jnp.array([d for _, d in DILATED_PATTERNS], jnp.int32)[:, None, None, None]
    n_steps = jnp.array([w // d for w, d in DILATED_PATTERNS], jnp.int32)[:, None, None, None]
    variant = jnp.arange(3)[None, :, None, None]
    i = jnp.arange(qb)[None, None, :, None]
    j = jnp.arange(2 * qb)[None, None, None, :]
    delta = jnp.where(variant == 2, i - j, qb + i - j)
    ok = (delta >= 0) & (delta <= n_steps) & ((variant != 1) | (j >= qb))
    bias_cols = rel_bias[:, :N_HEADS_A].reshape(REL_BUCKETS, N_HEADS_A // 2, 2)
    t = _bias_lookup(delta * dil, bias_cols, "pvijb,bnk->npvkij")
    t = jnp.where(ok[None, :, :, None, :, :], t, NEG_INF)
    return t.reshape(N_HEADS_A // 2, npat, 3, 2 * qb, 2 * qb)


def _moba_num_tables():
    return -(-(REL_MAX_DIST + MOBA_BLOCK - 1) // MOBA_BLOCK) + 1


def _moba_tables(rel_bias):
    bs = MOBA_BLOCK
    off = jnp.arange(_moba_num_tables())[:, None, None]
    j = jnp.arange(bs)[None, :, None]
    i = jnp.arange(bs)[None, None, :]
    n = off * bs + i - j
    bias_cols = rel_bias[:, N_HEADS_A:].reshape(REL_BUCKETS, N_HEADS_B // 2, 2)
    t = _bias_lookup(n, bias_cols, "vijb,bnk->nkvij")
    return jnp.where((n >= 0)[None, None], t, NEG_INF)


def _dil_kernel(q1, k1, v1, q4, k4, v4, q16, k16, v16, tab_ref, o_ref, vext, oscr, sc_scr, *, seq):
    qb = DIL_BLOCK
    nblk = seq // qb
    qrefs, krefs, vrefs = (q1, q4, q16), (k1, k4, k16), (v1, v4, v16)
    dils = tuple(d for _, d in DILATED_PATTERNS)

    for p in range(3):
        v = vrefs[p][0]
        vext[p, :, :LANES] = v
        vext[p, :, LANES:] = jnp.ones_like(v)

    row2 = lax.broadcasted_iota(jnp.int32, (2 * qb, LANES), 0)
    lane2 = lax.broadcasted_iota(jnp.int32, (2 * qb, LANES), 1)
    own = (row2 < qb) == (lane2 < HEAD_DIM)
    lane1 = lax.broadcasted_iota(jnp.int32, (qb, LANES), 1)

    def band_start(blk):
        return pl.multiple_of(jnp.maximum(blk - 1, 0) * qb, qb)

    def score_block(blk, slot):
        q0 = pl.multiple_of(blk * qb, qb)
        for p in range(3):
            q = qrefs[p][0, pl.ds(q0, qb), :]
            q2 = jnp.concatenate([q, q], axis=0)
            qs = jnp.where(own, q2, jnp.zeros_like(q2))
            sc_scr[slot, p] = _dot_nt(qs, krefs[p][0, pl.ds(band_start(blk), 2 * qb), :])

    score_block(0, 0)

    def softmax_block(blk, slot):
        q0 = pl.multiple_of(blk * qb, qb)
        k0 = band_start(blk)
        for p in range(3):
            per = nblk // dils[p]
            variant = jnp.where(blk == 0, 2, jnp.where(blk % per == 0, 1, 0))
            s = sc_scr[slot, p] + tab_ref[0, p, variant]
            m = jnp.max(s, axis=-1, keepdims=True)
            pr = jnp.exp(s - m).astype(BF16)
            pv = _dot(pr, vext[p, pl.ds(k0, 2 * qb), :])
            den = pv[:, LANES:]
            on = pv[:, :LANES] / den
            lse = m + jnp.log(den)
            oscr[p, 0, pl.ds(q0, qb), :] = jnp.where(lane1 < HEAD_DIM, on[:qb], on[qb:])
            oscr[p, 1, pl.ds(q0, qb), :] = jnp.where(lane1 == 0, lse[:qb], lse[qb:])

    def block_pair(i, carry):
        b0 = 2 * i
        score_block(b0 + 1, 1)
        softmax_block(b0, 0)
        score_block(jnp.minimum(b0 + 2, nblk - 1), 0)
        softmax_block(b0 + 1, 1)
        return carry

    lax.fori_loop(0, nblk // 2, block_pair, 0)

    n16 = seq // 16
    n4 = seq // 4
    mc_rows = 128
    head_of_lane = (lax.broadcasted_iota(jnp.int32, (mc_rows, LANES), 1) >= HEAD_DIM).astype(jnp.int32)

    def merge(r, carry):
        for mc in range(n16 // mc_rows):
            m_lo = mc * mc_rows
            rows = (pl.ds(r + 16 * m_lo, mc_rows, stride=16),
                    pl.ds((r % 4) * n4 + r // 4 + 4 * m_lo, mc_rows, stride=4),
                    pl.ds(pl.multiple_of(r * n16 + m_lo, mc_rows), mc_rows))
            lse = [oscr[p, 1, rows[p], :] for p in range(3)]
            top = jnp.maximum(jnp.maximum(lse[0], lse[1]), lse[2])
            w = [jnp.exp(x - top) for x in lse]
            inv = 1.0 / (w[0] + w[1] + w[2])
            out = None
            for p in range(3):
                wl = jnp.take_along_axis(w[p] * inv, head_of_lane, axis=1)
                term = wl * oscr[p, 0, rows[p], :]
                out = term if out is None else out + term
            o_ref[0, rows[0], :] = out
        return carry

    lax.fori_loop(0, 16, merge, 0)


def _dilated(a1, a4, a16, tabs):
    B, _, S, _ = a1.shape
    npair = N_HEADS_A // 2
    nq = npair

    def col(off):
        return pl.BlockSpec((1, None, S, LANES), lambda b, p, off=off: (b, off + p, 0, 0))

    in_specs = []
    for _ in range(3):
        in_specs += [col(0), col(nq), col(2 * nq)]
    in_specs.append(pl.BlockSpec((1,) + tabs.shape[1:], lambda b, p: (p, 0, 0, 0, 0)))
    return pl.pallas_call(
        functools.partial(_dil_kernel, seq=S),
        grid=(B, npair),
        in_specs=in_specs,
        out_specs=pl.BlockSpec((1, None, S, LANES), lambda b, p: (b, p, 0, 0)),
        out_shape=jax.ShapeDtypeStruct((B, npair, S, LANES), F32),
        scratch_shapes=[pltpu.VMEM((3, S, 2 * LANES), BF16), pltpu.VMEM((3, 2, S, LANES), F32),
                        pltpu.VMEM((2, 3, 2 * DIL_BLOCK, 2 * DIL_BLOCK), F32)],
        compiler_params=pltpu.CompilerParams(dimension_semantics=("arbitrary", "arbitrary"),
                                             vmem_limit_bytes=VMEM_LIMIT),
        name="dilated",
    )(a1, a1, a1, a4, a4, a4, a16, a16, a16, tabs)


def _moba_kernel(q_ref, k_ref, v_ref, tab_ref, o_ref, vt_ext, s_scr, p_scr, g_scr, *, seq):
    bs = MOBA_BLOCK
    nb = seq // bs
    ntab = tab_ref.shape[2]

    vt = v_ref[0].astype(F32).T
    row_f = lax.broadcasted_iota(jnp.int32, vt.shape, 0)
    vt_ext[0] = jnp.where(row_f < HEAD_DIM, vt, 1.0).astype(BF16)
    vt_ext[1] = jnp.where(row_f >= HEAD_DIM, vt, 1.0).astype(BF16)

    kmean = jnp.sum(k_ref[0].astype(F32).reshape(nb, bs, LANES), axis=1) * (1.0 / bs)
    km_hi = kmean.astype(BF16)
    km_lo = (kmean - km_hi.astype(F32)).astype(BF16)

    lane_q = lax.broadcasted_iota(jnp.int32, (bs, LANES), 1)
    head_mask = (lane_q < HEAD_DIM, lane_q >= HEAD_DIM)
    blk_id = lax.broadcasted_iota(jnp.int32, (nb, bs), 0)
    row_o = lax.broadcasted_iota(jnp.int32, (LANES, bs), 0)

    def score_block(i):
        q = q_ref[0, i * bs:(i + 1) * bs, :]
        n_keys = (i + 1) * bs
        for h in range(2):
            qh = jnp.where(head_mask[h], q, jnp.zeros_like(q))
            g_scr[i % 2, h] = _dot_nt(km_hi, qh) + _dot_nt(km_lo, qh)
            s_scr[i % 2, h, :n_keys, :] = _dot_nt(k_ref[0, :n_keys, :], qh)

    def attend_block(i):
        slot = i % 2
        n_keys = (i + 1) * bs
        outs = []
        for h in range(2):
            past = blk_id < i
            g = jnp.where(past, g_scr[slot, h], NEG_INF)
            allow = blk_id == i
            for _ in range(MOBA_TOPK):
                mx = jnp.max(g, axis=0, keepdims=True)
                first = jnp.min(jnp.where(g == mx, blk_id, nb), axis=0, keepdims=True)
                pick = blk_id == first
                allow = allow | (pick & past)
                g = jnp.where(pick, NEG_INF, g)
            blk_bias = jnp.where(allow, 0.0, NEG_INF)

            mx8 = None
            for jt in range(i + 1):
                rows = slice(jt * bs, (jt + 1) * bs)
                t = s_scr[slot, h, rows, :] + tab_ref[0, h, min(i - jt, ntab - 1)] + blk_bias[jt:jt + 1, :]
                s_scr[slot, h, rows, :] = t
                t8 = jnp.max(t.reshape(bs // 8, 8, bs), axis=0)
                mx8 = t8 if mx8 is None else jnp.maximum(mx8, t8)
            m = jnp.max(mx8, axis=0, keepdims=True)
            for jt in range(i + 1):
                rows = slice(jt * bs, (jt + 1) * bs)
                p_scr[h, rows, :] = jnp.exp(s_scr[slot, h, rows, :] - m).astype(BF16)
            outs.append(_dot(vt_ext[h, :, :n_keys], p_scr[h, :n_keys, :]))
        den0 = outs[0][HEAD_DIM:HEAD_DIM + 1, :]
        den1 = outs[1][0:1, :]
        o_t = jnp.where(row_o < HEAD_DIM, outs[0] / den0, outs[1] / den1)
        o_ref[0, i * bs:(i + 1) * bs, :] = o_t.T

    score_block(0)
    for i in range(nb):
        if i + 1 < nb:
            score_block(i + 1)
        attend_block(i)


def _moba(b1, tabs):
    B, _, S, _ = b1.shape
    npair = N_HEADS_B // 2

    def col(off):
        return pl.BlockSpec((1, None, S, LANES), lambda b, p, off=off: (b, off + p, 0, 0))

    return pl.pallas_call(
        functools.partial(_moba_kernel, seq=S),
        grid=(B, npair),
        in_specs=[col(0), col(npair), col(2 * npair),
                  pl.BlockSpec((1,) + tabs.shape[1:], lambda b, p: (p, 0, 0, 0, 0))],
        out_specs=pl.BlockSpec((1, None, S, LANES), lambda b, p: (b, p, 0, 0)),
        out_shape=jax.ShapeDtypeStruct((B, npair, S, LANES), F32),
        scratch_shapes=[pltpu.VMEM((2, LANES, S), BF16), pltpu.VMEM((2, 2, S, MOBA_BLOCK), F32),
                        pltpu.VMEM((2, S, MOBA_BLOCK), BF16),
                        pltpu.VMEM((2, 2, S // MOBA_BLOCK, MOBA_BLOCK), F32)],
        compiler_params=pltpu.CompilerParams(dimension_semantics=("arbitrary", "arbitrary"),
                                             vmem_limit_bytes=VMEM_LIMIT),
        name="moba",
    )(b1, b1, b1, tabs)


def _post_kernel(x_ref, oa_ref, ob_ref, kv_ref, ga_ref, gb_ref, wout_ref, gc_ref, wcq_ref, wco_ref,
                 gf2_ref, wgu_ref, wd_ref, gfin_ref, y_ref, *, d_ff, final_norm):
    d = x_ref.shape[2]
    oa = jnp.concatenate([oa_ref[0, c] for c in range(oa_ref.shape[1])], axis=1)
    ob = jnp.concatenate([ob_ref[0, c] for c in range(ob_ref.shape[1])], axis=1)
    wa = oa.shape[1]
    ya = _rms(oa, ga_ref[...]).astype(BF16)
    yb = _rms(ob, gb_ref[...]).astype(BF16)
    x = x_ref[0] + _dot(ya, wout_ref[:wa, :]) + _dot(yb, wout_ref[wa:, :])

    h = _rms(x, gc_ref[...]).astype(BF16)
    q = _dot(h, wcq_ref[...])
    dh = d // CROSS_HEADS
    scale = dh ** -0.5
    cross = None
    for c in range(CROSS_HEADS):
        qc = (q[:, c * dh:(c + 1) * dh] * scale).astype(BF16)
        s = _dot_nt(qc, kv_ref[0, :, c * dh:(c + 1) * dh])
        s = s - jnp.max(s, axis=-1, keepdims=True)
        e = jnp.exp(s)
        p = e / jnp.sum(e, axis=-1, keepdims=True)
        oc = _dot(p.astype(BF16), kv_ref[0, :, d + c * dh:d + (c + 1) * dh]).astype(BF16)
        part = _dot(oc, wco_ref[c * dh:(c + 1) * dh, :])
        cross = part if cross is None else cross + part
    x = x + cross

    h = _rms(x, gf2_ref[...]).astype(BF16)
    x = x + 0.5 * _swiglu_half(h, wgu_ref, wd_ref, d_ff)
    y_ref[0] = _rms(x, gfin_ref[...]) if final_norm else x


def _post(x1, oa, ob, kv, g_out_a, g_out_b, w_out, g_cross, w_cq, w_co, g_ffn2, wgu, wd, g_final, final_norm):
    B, S, D = x1.shape
    ta, tb = oa.shape[1], ob.shape[1]
    wa, wb = ta * LANES, tb * LANES
    M = kv.shape[1]
    d_ff = wd.shape[0]
    tm = TOKEN_TILE
    tok = lambda w: pl.BlockSpec((1, tm, w), lambda b, i: (b, i, 0))
    slab = lambda t: pl.BlockSpec((1, t, tm, LANES), lambda b, i: (b, 0, i, 0))
    return pl.pallas_call(
        functools.partial(_post_kernel, d_ff=d_ff, final_norm=final_norm),
        grid=(B, S // tm),
        in_specs=[tok(D), slab(ta), slab(tb),
                  pl.BlockSpec((1, M, 2 * D), lambda b, i: (b, 0, 0)),
                  _resident((1, wa)), _resident((1, wb)), _resident((wa + wb, D)),
                  _resident((1, D)), _resident((D, D)), _resident((D, D)),
                  _resident((1, D)), _resident((D, 2 * d_ff)), _resident((d_ff, D)),
                  _resident((1, D))],
        out_specs=tok(D),
        out_shape=jax.ShapeDtypeStruct((B, S, D), F32),
        compiler_params=pltpu.CompilerParams(dimension_semantics=("arbitrary", "arbitrary"),
                                             vmem_limit_bytes=VMEM_LIMIT),
        name="post",
    )(x1, oa, ob, kv, g_out_a.reshape(1, wa), g_out_b.reshape(1, wb), w_out, g_cross.reshape(1, D),
      w_cq, w_co, g_ffn2.reshape(1, D), wgu, wd, g_final.reshape(1, D))


def kernel(x, mem, g_ffn1, w_ffn1_gu, w_ffn1_down, g_mix, w_in, rel_bias, g_out_a, g_out_b, w_out,
           g_cross, g_mem, w_cq, w_ckv, w_co, g_ffn2, w_ffn2_gu, w_ffn2_down, g_final):
    B, S, D = x.shape
    depth = g_ffn1.shape[0]
    dil_tabs = _dilated_tables(rel_bias)
    moba_tabs = _moba_tables(rel_bias)
    bf = lambda w: w.astype(BF16)
    y = None
    for l in range(depth):
        kv = _memkv(mem, g_mem[l], bf(w_ckv[l]))
        x1 = _ffn1(x.reshape(B * S, D), g_ffn1[l], bf(w_ffn1_gu[l]), bf(w_ffn1_down[l])).reshape(B, S, D)
        a1, a4, a16, b1 = _proj(x1, g_mix[l], bf(w_in[l]))
        oa = _dilated(a1, a4, a16, dil_tabs)
        ob = _moba(b1, moba_tabs)
        y = _post(x1, oa, ob, kv, g_out_a[l], g_out_b[l], bf(w_out[l]), g_cross[l], bf(w_cq[l]),
                  bf(w_co[l]), g_ffn2[l], bf(w_ffn2_gu[l]), bf(w_ffn2_down[l]), g_final,
                  final_norm=(l == depth - 1))
        x = y
    return y
```

```python
import functools
import math

import numpy as np
import jax
import jax.numpy as jnp
from jax import lax
from jax.experimental import pallas as pl
from jax.experimental.pallas import tpu as pltpu

F32 = jnp.float32
BF16 = jnp.bfloat16

HEAD_DIM = 64
LANES = 128
N_HEADS_A = 12
N_HEADS_B = 4
DILATED_PATTERNS = ((128, 1), (512, 4), (2048, 16))
DIL_BLOCK = 128
MOBA_BLOCK = 256
MOBA_TOPK = 3
CROSS_HEADS = 4
REL_BUCKETS = 32
REL_MAX_DIST = 2048
NORM_EPS = 1e-6
NEG_INF = -jnp.inf
LOG2_E = math.log2(math.e)
VMEM_LIMIT = 56 * 1024 * 1024

FFN_CHUNK = 256
TOKEN_TILE = 512


def _rms(x, g):
    xf = x.astype(F32)
    return xf * lax.rsqrt(jnp.mean(xf * xf, axis=-1, keepdims=True) + NORM_EPS) * g


def _dot(a, b):
    return jnp.dot(a, b, preferred_element_type=F32)


def _dot_nt(a, b):
    return lax.dot_general(a, b, (((1,), (1,)), ((), ())), preferred_element_type=F32)


def _resident(shape):
    nd = len(shape)
    return pl.BlockSpec(shape, lambda *_: (0,) * nd, pipeline_mode=pl.Buffered(1))


def _swiglu_half(h, wgu_ref, wd_ref, d_ff):
    acc = None
    for c in range(d_ff // FFN_CHUNK):
        lo = c * FFN_CHUNK
        g = _dot(h, wgu_ref[:, lo:lo + FFN_CHUNK])
        u = _dot(h, wgu_ref[:, d_ff + lo:d_ff + lo + FFN_CHUNK])
        a = (g * (1.0 / (1.0 + jnp.exp(-g))) * u).astype(BF16)
        part = _dot(a, wd_ref[lo:lo + FFN_CHUNK, :])
        acc = part if acc is None else acc + part
    return acc


def _memkv_kernel(mem_ref, g_ref, w_ref, kv_ref):
    h = _rms(mem_ref[0], g_ref[...]).astype(BF16)
    kv_ref[0] = _dot(h, w_ref[...]).astype(BF16)


def _memkv(mem, g_mem, w_ckv):
    B, M, D = mem.shape
    N = w_ckv.shape[1]
    return pl.pallas_call(
        _memkv_kernel,
        grid=(B,),
        in_specs=[pl.BlockSpec((1, M, D), lambda b: (b, 0, 0)),
                  _resident((1, D)), _resident((D, N))],
        out_specs=pl.BlockSpec((1, M, N), lambda b: (b, 0, 0)),
        out_shape=jax.ShapeDtypeStruct((B, M, N), BF16),
        compiler_params=pltpu.CompilerParams(dimension_semantics=("arbitrary",),
                                             vmem_limit_bytes=VMEM_LIMIT),
        name="memkv",
    )(mem, g_mem.reshape(1, D), w_ckv)


def _ffn1_kernel(x_ref, g_ref, wgu_ref, wd_ref, o_ref, *, d_ff):
    x = x_ref[...]
    h = _rms(x, g_ref[...]).astype(BF16)
    o_ref[...] = x + 0.5 * _swiglu_half(h, wgu_ref, wd_ref, d_ff)


def _ffn1(x2d, g, wgu, wd):
    T, D = x2d.shape
    d_ff = wd.shape[0]
    tm = TOKEN_TILE
    return pl.pallas_call(
        functools.partial(_ffn1_kernel, d_ff=d_ff),
        grid=(T // tm,),
        in_specs=[pl.BlockSpec((tm, D), lambda i: (i, 0)),
                  _resident((1, D)), _resident((D, 2 * d_ff)), _resident((d_ff, D))],
        out_specs=pl.BlockSpec((tm, D), lambda i: (i, 0)),
        out_shape=jax.ShapeDtypeStruct((T, D), F32),
        compiler_params=pltpu.CompilerParams(dimension_semantics=("arbitrary",),
                                             vmem_limit_bytes=VMEM_LIMIT),
        name="ffn1",
    )(x2d, g.reshape(1, D), wgu, wd)


def _proj_kernel(x_ref, g_ref, w_ref, a1_ref, a4_ref, a16_ref, b_ref, scr, scr4, *, wa, wqa, wqb):
    tm = x_ref.shape[1]
    h = _rms(x_ref[0], g_ref[...]).astype(BF16)
    scale = HEAD_DIM ** -0.5
    pb = _dot(h, w_ref[:, wa:])
    for c in range(pb.shape[1] // LANES):
        tile = pb[:, c * LANES:(c + 1) * LANES]
        b_ref[0, c] = (tile * (scale * LOG2_E) if c < wqb // LANES else tile).astype(BF16)
    group = 2 * LANES
    for c in range(wa // LANES):
        if c % (group // LANES) == 0:
            pa = _dot(h, w_ref[:, c * LANES:c * LANES + group])
        sub = slice((c * LANES) % group, (c * LANES) % group + LANES)
        tile = pa[:, sub] * scale if c < wqa // LANES else pa[:, sub]
        a1_ref[0, c] = tile.astype(BF16)
        b = c % 2
        scr[b] = tile
        for r in range(4):
            plane = scr[b, pl.ds(r, tm // 4, stride=4), :]
            a4_ref[0, c, r] = plane.astype(BF16)
            scr4[b, r] = plane
        for r in range(16):
            a16_ref[0, c, r] = scr4[b, r % 4, pl.ds(r // 4, tm // 16, stride=4), :].astype(BF16)


def _proj(x1, g, w_in):
    B, S, D = x1.shape
    wa = 3 * N_HEADS_A * HEAD_DIM
    wb = 3 * N_HEADS_B * HEAD_DIM
    ta, tb = wa // LANES, wb // LANES
    tm = TOKEN_TILE
    kern = functools.partial(_proj_kernel, wa=wa, wqa=N_HEADS_A * HEAD_DIM, wqb=N_HEADS_B * HEAD_DIM)
    a1, a4, a16, b1 = pl.pallas_call(
        kern,
        grid=(B, S // tm),
        in_specs=[pl.BlockSpec((1, tm, D), lambda b, i: (b, i, 0)),
                  _resident((1, D)), _resident((D, wa + wb))],
        out_specs=[pl.BlockSpec((1, ta, tm, LANES), lambda b, i: (b, 0, i, 0)),
                   pl.BlockSpec((1, ta, 4, tm // 4, LANES), lambda b, i: (b, 0, 0, i, 0)),
                   pl.BlockSpec((1, ta, 16, tm // 16, LANES), lambda b, i: (b, 0, 0, i, 0)),
                   pl.BlockSpec((1, tb, tm, LANES), lambda b, i: (b, 0, i, 0))],
        out_shape=[jax.ShapeDtypeStruct((B, ta, S, LANES), BF16),
                   jax.ShapeDtypeStruct((B, ta, 4, S // 4, LANES), BF16),
                   jax.ShapeDtypeStruct((B, ta, 16, S // 16, LANES), BF16),
                   jax.ShapeDtypeStruct((B, tb, S, LANES), BF16)],
        scratch_shapes=[pltpu.VMEM((2, tm, LANES), F32), pltpu.VMEM((2, 4, tm // 4, LANES), F32)],
        compiler_params=pltpu.CompilerParams(dimension_semantics=("arbitrary", "arbitrary"),
                                             vmem_limit_bytes=VMEM_LIMIT),
        name="proj",
    )(x1, g.reshape(1, D), w_in)
    return a1, a4.reshape(a1.shape), a16.reshape(a1.shape), b1


def _rel_bucket(dist):
    max_exact = REL_BUCKETS // 2
    n = jnp.maximum(dist, 0)
    nf = jnp.maximum(n, 1).astype(F32)
    large = max_exact + (jnp.log(nf / max_exact) / math.log(REL_MAX_DIST / max_exact)
                         * (REL_BUCKETS - max_exact)).astype(jnp.int32)
    large = jnp.minimum(large, REL_BUCKETS - 1)
    return jnp.where(n < max_exact, n, large)


def _bias_lookup(dist, bias_cols, spec):
    onehot = (_rel_bucket(dist)[..., None] == jnp.arange(REL_BUCKETS)).astype(F32)
    return jnp.einsum(spec, onehot, bias_cols.astype(F32), precision=lax.Precision.HIGHEST)


def _dilated_tables(rel_bias):
    qb = DIL_BLOCK
    npat = len(DILATED_PATTERNS)
    dil = jnp.array([d for _, d in DILATED_PATTERNS], jnp.int32)[:, None, None, None]
    n_steps = jnp.array([w // d for w, d in DILATED_PATTERNS], jnp.int32)[:, None, None, None]
    variant = jnp.arange(3)[None, :, None, None]
    i = jnp.arange(qb)[None, None, :, None]
    j = jnp.arange(2 * qb)[None, None, None, :]
    delta = jnp.where(variant == 2, i - j, qb + i - j)
    ok = (delta >= 0) & (delta <= n_steps) & ((variant != 1) | (j >= qb))
    bias_cols = rel_bias[:, :N_HEADS_A].reshape(REL_BUCKETS, N_HEADS_A // 2, 2)
    t = _bias_lookup(delta * dil, bias_cols, "pvijb,bnk->npvkij")
    t = jnp.where(ok[None, :, :, None, :, :], t, NEG_INF)
    return t.reshape(N_HEADS_A // 2, npat, 3, 2 * qb, 2 * qb)


def _moba_num_tables():
    return -(-(REL_MAX_DIST + MOBA_BLOCK - 1) // MOBA_BLOCK) + 1


def _moba_tables(rel_bias):
    bs = MOBA_BLOCK
    off = jnp.arange(_moba_num_tables())[:, None, None]
    j = jnp.arange(bs)[None, :, None]
    i = jnp.arange(bs)[None, None, :]
    n = off * bs + i - j
    bias_cols = rel_bias[:, N_HEADS_A:].reshape(REL_BUCKETS, N_HEADS_B // 2, 2)
    t = _bias_lookup(n, bias_cols, "vijb,bnk->nkvij") * LOG2_E
    return jnp.where((n >= 0)[None, None], t, NEG_INF)


def _dil_kernel(q1, k1, v1, q4, k4, v4, q16, k16, v16, tab_ref, o_ref, vext, oscr, sc_scr, *, seq):
    qb = DIL_BLOCK
    nblk = seq // qb
    qrefs, krefs, vrefs = (q1, q4, q16), (k1, k4, k16), (v1, v4, v16)
    dils = tuple(d for _, d in DILATED_PATTERNS)

    for p in range(3):
        v = vrefs[p][0]
        vext[p, :, :LANES] = v
        vext[p, :, LANES:] = jnp.ones_like(v)

    row2 = lax.broadcasted_iota(jnp.int32, (2 * qb, LANES), 0)
    lane2 = lax.broadcasted_iota(jnp.int32, (2 * qb, LANES), 1)
    own = (row2 < qb) == (lane2 < HEAD_DIM)
    lane1 = lax.broadcasted_iota(jnp.int32, (qb, LANES), 1)

    def band_start(blk):
        return pl.multiple_of(jnp.maximum(blk - 1, 0) * qb, qb)

    def score_block(blk, slot):
        q0 = pl.multiple_of(blk * qb, qb)
        for p in range(3):
            q = qrefs[p][0, pl.ds(q0, qb), :]
            q2 = jnp.concatenate([q, q], axis=0)
            qs = jnp.where(own, q2, jnp.zeros_like(q2))
            sc_scr[slot, p] = _dot_nt(qs, krefs[p][0, pl.ds(band_start(blk), 2 * qb), :])

    score_block(0, 0)

    def softmax_block(blk, slot):
        q0 = pl.multiple_of(blk * qb, qb)
        k0 = band_start(blk)
        for p in range(3):
            per = nblk // dils[p]
            variant = jnp.where(blk == 0, 2, jnp.where(blk % per == 0, 1, 0))
            s = sc_scr[slot, p] + tab_ref[0, p, variant]
            m = jnp.max(s, axis=-1, keepdims=True)
            pr = jnp.exp(s - m).astype(BF16)
            pv = _dot(pr, vext[p, pl.ds(k0, 2 * qb), :])
            den = pv[:, LANES:]
            oscr[p, 0, pl.ds(q0, qb), :] = jnp.where(lane1 < HEAD_DIM, pv[:qb, :LANES], pv[qb:, :LANES])
            oscr[p, 1, pl.ds(q0, qb), :] = jnp.where(
                lane1 == 0, m[:qb], jnp.where(lane1 == 1, m[qb:], jnp.where(lane1 == 2, den[:qb], den[qb:])))

    def block_pair(i, carry):
        b0 = 2 * i
        score_block(b0 + 1, 1)
        softmax_block(b0, 0)
        score_block(jnp.minimum(b0 + 2, nblk - 1), 0)
        softmax_block(b0 + 1, 1)
        return carry

    lax.fori_loop(0, nblk // 2, block_pair, 0)

    n16 = seq // 16
    n4 = seq // 4
    mc_rows = 128
    head_of_lane = (lax.broadcasted_iota(jnp.int32, (mc_rows, LANES), 1) >= HEAD_DIM).astype(jnp.int32)

    def merge(r, carry):
        for mc in range(n16 // mc_rows):
            m_lo = mc * mc_rows
            rows = (pl.ds(r + 16 * m_lo, mc_rows, stride=16),
                    pl.ds((r % 4) * n4 + r // 4 + 4 * m_lo, mc_rows, stride=4),
                    pl.ds(pl.multiple_of(r * n16 + m_lo, mc_rows), mc_rows))
            st = [oscr[p, 1, rows[p], :] for p in range(3)]
            top = jnp.maximum(jnp.maximum(st[0], st[1]), st[2])
            w = [jnp.exp(x - top) for x in st]
            ls = [pltpu.roll(x, LANES - 2, axis=1) for x in st]
            inv = 1.0 / (w[0] * ls[0] + w[1] * ls[1] + w[2] * ls[2])
            out = None
            for p in range(3):
                wl = jnp.take_along_axis(w[p] * inv, head_of_lane, axis=1)
                term = wl * oscr[p, 0, rows[p], :]
                out = term if out is None else out + term
            o_ref[0, rows[0], :] = out
        return carry

    lax.fori_loop(0, 16, merge, 0)


def _dilated(a1, a4, a16, tabs):
    B, _, S, _ = a1.shape
    npair = N_HEADS_A // 2
    nq = npair

    def col(off):
        return pl.BlockSpec((1, None, S, LANES), lambda b, p, off=off: (b, off + p, 0, 0))

    in_specs = []
    for _ in range(3):
        in_specs += [col(0), col(nq), col(2 * nq)]
    in_specs.append(pl.BlockSpec((1,) + tabs.shape[1:], lambda b, p: (p, 0, 0, 0, 0)))
    return pl.pallas_call(
        functools.partial(_dil_kernel, seq=S),
        grid=(B, npair),
        in_specs=in_specs,
        out_specs=pl.BlockSpec((1, None, S, LANES), lambda b, p: (b, p, 0, 0)),
        out_shape=jax.ShapeDtypeStruct((B, npair, S, LANES), F32),
        scratch_shapes=[pltpu.VMEM((3, S, 2 * LANES), BF16), pltpu.VMEM((3, 2, S, LANES), F32),
                        pltpu.VMEM((2, 3, 2 * DIL_BLOCK, 2 * DIL_BLOCK), F32)],
        compiler_params=pltpu.CompilerParams(dimension_semantics=("arbitrary", "arbitrary"),
                                             vmem_limit_bytes=VMEM_LIMIT),
        name="dilated",
    )(a1, a1, a1, a4, a4, a4, a16, a16, a16, tabs)


def _moba_kernel(q_ref, k_ref, v_ref, tab_ref, o_ref, vt_ext, s_scr, p_scr, g_scr, *, seq):
    bs = MOBA_BLOCK
    nb = seq // bs
    ntab = tab_ref.shape[2]

    vt = v_ref[0].astype(F32).T
    row_f = lax.broadcasted_iota(jnp.int32, vt.shape, 0)
    vt_ext[0] = jnp.where(row_f < HEAD_DIM, vt, 1.0).astype(BF16)
    vt_ext[1] = jnp.where(row_f >= HEAD_DIM, vt, 1.0).astype(BF16)

    kmean = jnp.sum(k_ref[0].astype(F32).reshape(nb, bs, LANES), axis=1) * (1.0 / bs)
    km_hi = kmean.astype(BF16)
    km_lo = (kmean - km_hi.astype(F32)).astype(BF16)

    lane_q = lax.broadcasted_iota(jnp.int32, (bs, LANES), 1)
    head_mask = (lane_q < HEAD_DIM, lane_q >= HEAD_DIM)
    blk_id = lax.broadcasted_iota(jnp.int32, (nb, bs), 0)
    row_o = lax.broadcasted_iota(jnp.int32, (LANES, bs), 0)

    def score_block(i):
        q = q_ref[0, i * bs:(i + 1) * bs, :]
        n_keys = (i + 1) * bs
        for h in range(2):
            qh = jnp.where(head_mask[h], q, jnp.zeros_like(q))
            g_scr[i % 2, h] = _dot_nt(km_hi, qh) + _dot_nt(km_lo, qh)
            s_scr[i % 2, h, :n_keys, :] = _dot_nt(k_ref[0, :n_keys, :], qh)

    def attend_block(i):
        slot = i % 2
        n_keys = (i + 1) * bs
        outs = []
        for h in range(2):
            past = blk_id < i
            g = jnp.where(past, g_scr[slot, h], NEG_INF)
            allow = blk_id == i
            for _ in range(MOBA_TOPK):
                mx = jnp.max(g, axis=0, keepdims=True)
                first = jnp.min(jnp.where(g == mx, blk_id, nb), axis=0, keepdims=True)
                pick = blk_id == first
                allow = allow | (pick & past)
                g = jnp.where(pick, NEG_INF, g)
            blk_bias = jnp.where(allow, 0.0, NEG_INF)

            mx8 = None
            for jt in range(i + 1):
                rows = slice(jt * bs, (jt + 1) * bs)
                t = s_scr[slot, h, rows, :] + tab_ref[0, h, min(i - jt, ntab - 1)]
                s_scr[slot, h, rows, :] = t
                t8 = jnp.max(t.reshape(bs // 8, 8, bs), axis=0) + blk_bias[jt:jt + 1, :]
                mx8 = t8 if mx8 is None else jnp.maximum(mx8, t8)
            m = jnp.max(mx8, axis=0, keepdims=True)
            for jt in range(i + 1):
                rows = slice(jt * bs, (jt + 1) * bs)
                shift = m - blk_bias[jt:jt + 1, :]
                p_scr[h, rows, :] = jnp.exp2(s_scr[slot, h, rows, :] - shift).astype(BF16)
            outs.append(_dot(vt_ext[h, :, :n_keys], p_scr[h, :n_keys, :]))
        den0 = outs[0][HEAD_DIM:HEAD_DIM + 1, :]
        den1 = outs[1][0:1, :]
        o_t = jnp.where(row_o < HEAD_DIM, outs[0] / den0, outs[1] / den1)
        o_ref[0, i * bs:(i + 1) * bs, :] = o_t.T

    score_block(0)
    for i in range(nb):
        if i + 1 < nb:
            score_block(i + 1)
        attend_block(i)


def _moba(b1, tabs):
    B, _, S, _ = b1.shape
    npair = N_HEADS_B // 2

    def col(off):
        return pl.BlockSpec((1, None, S, LANES), lambda b, p, off=off: (b, off + p, 0, 0))

    return pl.pallas_call(
        functools.partial(_moba_kernel, seq=S),
        grid=(B, npair),
        in_specs=[col(0), col(npair), col(2 * npair),
                  pl.BlockSpec((1,) + tabs.shape[1:], lambda b, p: (p, 0, 0, 0, 0))],
        out_specs=pl.BlockSpec((1, None, S, LANES), lambda b, p: (b, p, 0, 0)),
        out_shape=jax.ShapeDtypeStruct((B, npair, S, LANES), F32),
        scratch_shapes=[pltpu.VMEM((2, LANES, S), BF16), pltpu.VMEM((2, 2, S, MOBA_BLOCK), F32),
                        pltpu.VMEM((2, S, MOBA_BLOCK), BF16),
                        pltpu.VMEM((2, 2, S // MOBA_BLOCK, MOBA_BLOCK), F32)],
        compiler_params=pltpu.CompilerParams(dimension_semantics=("arbitrary", "arbitrary"),
                                             vmem_limit_bytes=VMEM_LIMIT),
        name="moba",
    )(b1, b1, b1, tabs)


def _post_kernel(x_ref, oa_ref, ob_ref, kv_ref, ga_ref, gb_ref, wout_ref, gc_ref, wcq_ref, wco_ref,
                 gf2_ref, wgu_ref, wd_ref, gfin_ref, y_ref, *, d_ff, final_norm):
    d = x_ref.shape[2]
    oa = jnp.concatenate([oa_ref[0, c] for c in range(oa_ref.shape[1])], axis=1)
    ob = jnp.concatenate([ob_ref[0, c] for c in range(ob_ref.shape[1])], axis=1)
    wa = oa.shape[1]
    ya = _rms(oa, ga_ref[...]).astype(BF16)
    yb = _rms(ob, gb_ref[...]).astype(BF16)
    x = x_ref[0] + _dot(ya, wout_ref[:wa, :]) + _dot(yb, wout_ref[wa:, :])

    h = _rms(x, gc_ref[...]).astype(BF16)
    q = _dot(h, wcq_ref[...])
    dh = d // CROSS_HEADS
    scale = dh ** -0.5
    scores = [_dot_nt((q[:, c * dh:(c + 1) * dh] * scale).astype(BF16), kv_ref[0, :, c * dh:(c + 1) * dh])
              for c in range(CROSS_HEADS)]
    heads = []
    for c, s in enumerate(scores):
        e = jnp.exp(s - jnp.max(s, axis=-1, keepdims=True))
        p = e / jnp.sum(e, axis=-1, keepdims=True)
        heads.append(_dot(p.astype(BF16), kv_ref[0, :, d + c * dh:d + (c + 1) * dh]).astype(BF16))
    x = x + _dot(jnp.concatenate(heads, axis=1), wco_ref[...])

    h = _rms(x, gf2_ref[...]).astype(BF16)
    x = x + 0.5 * _swiglu_half(h, wgu_ref, wd_ref, d_ff)
    y_ref[0] = _rms(x, gfin_ref[...]) if final_norm else x


def _post(x1, oa, ob, kv, g_out_a, g_out_b, w_out, g_cross, w_cq, w_co, g_ffn2, wgu, wd, g_final, final_norm):
    B, S, D = x1.shape
    ta, tb = oa.shape[1], ob.shape[1]
    wa, wb = ta * LANES, tb * LANES
    M = kv.shape[1]
    d_ff = wd.shape[0]
    tm = TOKEN_TILE
    tok = lambda w: pl.BlockSpec((1, tm, w), lambda b, i: (b, i, 0))
    slab = lambda t: pl.BlockSpec((1, t, tm, LANES), lambda b, i: (b, 0, i, 0))
    return pl.pallas_call(
        functools.partial(_post_kernel, d_ff=d_ff, final_norm=final_norm),
        grid=(B, S // tm),
        in_specs=[tok(D), slab(ta), slab(tb),
                  pl.BlockSpec((1, M, 2 * D), lambda b, i: (b, 0, 0)),
                  _resident((1, wa)), _resident((1, wb)), _resident((wa + wb, D)),
                  _resident((1, D)), _resident((D, D)), _resident((D, D)),
                  _resident((1, D)), _resident((D, 2 * d_ff)), _resident((d_ff, D)),
                  _resident((1, D))],
        out_specs=tok(D),
        out_shape=jax.ShapeDtypeStruct((B, S, D), F32),
        compiler_params=pltpu.CompilerParams(dimension_semantics=("arbitrary", "arbitrary"),
                                             vmem_limit_bytes=VMEM_LIMIT),
        name="post",
    )(x1, oa, ob, kv, g_out_a.reshape(1, wa), g_out_b.reshape(1, wb), w_out, g_cross.reshape(1, D),
      w_cq, w_co, g_ffn2.reshape(1, D), wgu, wd, g_final.reshape(1, D))


def kernel(x, mem, g_ffn1, w_ffn1_gu, w_ffn1_down, g_mix, w_in, rel_bias, g_out_a, g_out_b, w_out,
           g_cross, g_mem, w_cq, w_ckv, w_co, g_ffn2, w_ffn2_gu, w_ffn2_down, g_final):
    B, S, D = x.shape
    depth = g_ffn1.shape[0]
    dil_tabs = _dilated_tables(rel_bias)
    moba_tabs = _moba_tables(rel_bias)
    bf = lambda w: w.astype(BF16)
    y = None
    for l in range(depth):
        kv = _memkv(mem, g_mem[l], bf(w_ckv[l]))
        x1 = _ffn1(x.reshape(B * S, D), g_ffn1[l], bf(w_ffn1_gu[l]), bf(w_ffn1_down[l])).reshape(B, S, D)
        a1, a4, a16, b1 = _proj(x1, g_mix[l], bf(w_in[l]))
        oa = _dilated(a1, a4, a16, dil_tabs)
        ob = _moba(b1, moba_tabs)
        y = _post(x1, oa, ob, kv, g_out_a[l], g_out_b[l], bf(w_out[l]), g_cross[l], bf(w_cq[l]),
                  bf(w_co[l]), g_ffn2[l], bf(w_ffn2_gu[l]), bf(w_ffn2_down[l]), g_final,
                  final_norm=(l == depth - 1))
        x = y
    return y
```

```python
import functools
import math

import numpy as np
import jax
import jax.numpy as jnp
from jax import lax
from jax.experimental import pallas as pl
from jax.experimental.pallas import tpu as pltpu

F32 = jnp.float32
BF16 = jnp.bfloat16

HEAD_DIM = 64
LANES = 128
N_HEADS_A = 12
N_HEADS_B = 4
DILATED_PATTERNS = ((128, 1), (512, 4), (2048, 16))
DIL_BLOCK = 128
MOBA_BLOCK = 256
MOBA_TOPK = 3
CROSS_HEADS = 4
REL_BUCKETS = 32
REL_MAX_DIST = 2048
NORM_EPS = 1e-6
NEG_INF = -jnp.inf
VMEM_LIMIT = 56 * 1024 * 1024

FFN_CHUNK = 256
TOKEN_TILE = 512


def _rms(x, g):
    xf = x.astype(F32)
    return xf * lax.rsqrt(jnp.mean(xf * xf, axis=-1, keepdims=True) + NORM_EPS) * g


def _dot(a, b):
    return jnp.dot(a, b, preferred_element_type=F32)


def _dot_nt(a, b):
    return lax.dot_general(a, b, (((1,), (1,)), ((), ())), preferred_element_type=F32)


def _resident(shape):
    nd = len(shape)
    return pl.BlockSpec(shape, lambda *_: (0,) * nd, pipeline_mode=pl.Buffered(1))


def _swiglu_half(h, wgu_ref, wd_ref, d_ff):
    acc = None
    for c in range(d_ff // FFN_CHUNK):
        lo = c * FFN_CHUNK
        g = _dot(h, wgu_ref[:, lo:lo + FFN_CHUNK])
        u = _dot(h, wgu_ref[:, d_ff + lo:d_ff + lo + FFN_CHUNK])
        a = (g * (1.0 / (1.0 + jnp.exp(-g))) * u).astype(BF16)
        part = _dot(a, wd_ref[lo:lo + FFN_CHUNK, :])
        acc = part if acc is None else acc + part
    return acc


def _memkv_kernel(mem_ref, g_ref, w_ref, kv_ref):
    h = _rms(mem_ref[0], g_ref[...]).astype(BF16)
    kv_ref[0] = _dot(h, w_ref[...]).astype(BF16)


def _memkv(mem, g_mem, w_ckv):
    B, M, D = mem.shape
    N = w_ckv.shape[1]
    return pl.pallas_call(
        _memkv_kernel,
        grid=(B,),
        in_specs=[pl.BlockSpec((1, M, D), lambda b: (b, 0, 0)),
                  _resident((1, D)), _resident((D, N))],
        out_specs=pl.BlockSpec((1, M, N), lambda b: (b, 0, 0)),
        out_shape=jax.ShapeDtypeStruct((B, M, N), BF16),
        compiler_params=pltpu.CompilerParams(dimension_semantics=("arbitrary",),
                                             vmem_limit_bytes=VMEM_LIMIT),
        name="memkv",
    )(mem, g_mem.reshape(1, D), w_ckv)


def _ffn1_kernel(x_ref, g_ref, wgu_ref, wd_ref, o_ref, *, d_ff):
    x = x_ref[...]
    h = _rms(x, g_ref[...]).astype(BF16)
    o_ref[...] = x + 0.5 * _swiglu_half(h, wgu_ref, wd_ref, d_ff)


def _ffn1(x2d, g, wgu, wd):
    T, D = x2d.shape
    d_ff = wd.shape[0]
    tm = TOKEN_TILE
    return pl.pallas_call(
        functools.partial(_ffn1_kernel, d_ff=d_ff),
        grid=(T // tm,),
        in_specs=[pl.BlockSpec((tm, D), lambda i: (i, 0)),
                  _resident((1, D)), _resident((D, 2 * d_ff)), _resident((d_ff, D))],
        out_specs=pl.BlockSpec((tm, D), lambda i: (i, 0)),
        out_shape=jax.ShapeDtypeStruct((T, D), F32),
        compiler_params=pltpu.CompilerParams(dimension_semantics=("arbitrary",),
                                             vmem_limit_bytes=VMEM_LIMIT),
        name="ffn1",
    )(x2d, g.reshape(1, D), wgu, wd)


def _proj_kernel(x_ref, g_ref, w_ref, a1_ref, a4_ref, a16_ref, b_ref, scr, scr4, *, wa):
    tm = x_ref.shape[1]
    h = _rms(x_ref[0], g_ref[...]).astype(BF16)
    pb = _dot(h, w_ref[:, wa:])
    for c in range(pb.shape[1] // LANES):
        b_ref[0, c] = pb[:, c * LANES:(c + 1) * LANES].astype(BF16)
    group = 2 * LANES
    for c in range(wa // LANES):
        if c % (group // LANES) == 0:
            pa = _dot(h, w_ref[:, c * LANES:c * LANES + group])
        sub = slice((c * LANES) % group, (c * LANES) % group + LANES)
        tile = pa[:, sub]
        a1_ref[0, c] = tile.astype(BF16)
        b = c % 2
        scr[b] = tile
        for r in range(4):
            plane = scr[b, pl.ds(r, tm // 4, stride=4), :]
            a4_ref[0, c, r] = plane.astype(BF16)
            scr4[b, r] = plane
        for r in range(16):
            a16_ref[0, c, r] = scr4[b, r % 4, pl.ds(r // 4, tm // 16, stride=4), :].astype(BF16)


def _proj(x1, g, w_in):
    B, S, D = x1.shape
    wa = 3 * N_HEADS_A * HEAD_DIM
    wb = 3 * N_HEADS_B * HEAD_DIM
    ta, tb = wa // LANES, wb // LANES
    tm = TOKEN_TILE
    kern = functools.partial(_proj_kernel, wa=wa)
    a1, a4, a16, b1 = pl.pallas_call(
        kern,
        grid=(B, S // tm),
        in_specs=[pl.BlockSpec((1, tm, D), lambda b, i: (b, i, 0)),
                  _resident((1, D)), _resident((D, wa + wb))],
        out_specs=[pl.BlockSpec((1, ta, tm, LANES), lambda b, i: (b, 0, i, 0)),
                   pl.BlockSpec((1, ta, 4, tm // 4, LANES), lambda b, i: (b, 0, 0, i, 0)),
                   pl.BlockSpec((1, ta, 16, tm // 16, LANES), lambda b, i: (b, 0, 0, i, 0)),
                   pl.BlockSpec((1, tb, tm, LANES), lambda b, i: (b, 0, i, 0))],
        out_shape=[jax.ShapeDtypeStruct((B, ta, S, LANES), BF16),
                   jax.ShapeDtypeStruct((B, ta, 4, S // 4, LANES), BF16),
                   jax.ShapeDtypeStruct((B, ta, 16, S // 16, LANES), BF16),
                   jax.ShapeDtypeStruct((B, tb, S, LANES), BF16)],
        scratch_shapes=[pltpu.VMEM((2, tm, LANES), F32), pltpu.VMEM((2, 4, tm // 4, LANES), F32)],
        compiler_params=pltpu.CompilerParams(dimension_semantics=("arbitrary", "arbitrary"),
                                             vmem_limit_bytes=VMEM_LIMIT),
        name="proj",
    )(x1, g.reshape(1, D), w_in)
    return a1, a4.reshape(a1.shape), a16.reshape(a1.shape), b1


def _rel_bucket(dist):
    max_exact = REL_BUCKETS // 2
    n = jnp.maximum(dist, 0)
    nf = jnp.maximum(n, 1).astype(F32)
    large = max_exact + (jnp.log(nf / max_exact) / math.log(REL_MAX_DIST / max_exact)
                         * (REL_BUCKETS - max_exact)).astype(jnp.int32)
    large = jnp.minimum(large, REL_BUCKETS - 1)
    return jnp.where(n < max_exact, n, large)


def _bias_lookup(dist, bias_cols, spec):
    onehot = (_rel_bucket(dist)[..., None] == jnp.arange(REL_BUCKETS)).astype(F32)
    return jnp.einsum(spec, onehot, bias_cols.astype(F32), precision=lax.Precision.HIGHEST)


def _dilated_tables(rel_bias):
    qb = DIL_BLOCK
    npat = len(DILATED_PATTERNS)
    dil = jnp.array([d for _, d in DILATED_PATTERNS], jnp.int32)[:, None, None, None]
    n_steps = jnp.array([w // d for w, d in DILATED_PATTERNS], jnp.int32)[:, None, None, None]
    variant = jnp.arange(3)[None, :, None, None]
    i = jnp.arange(qb)[None, None, :, None]
    j = jnp.arange(2 * qb)[None, None, None, :]
    delta = jnp.where(variant == 2, i - j, qb + i - j)
    ok = (delta >= 0) & (delta <= n_steps) & ((variant != 1) | (j >= qb))
    bias_cols = rel_bias[:, :N_HEADS_A].reshape(REL_BUCKETS, N_HEADS_A // 2, 2)
    t = _bias_lookup(delta * dil, bias_cols, "pvijb,bnk->npvkij")
    t = jnp.where(ok[None, :, :, None, :, :], t, NEG_INF)
    return t.reshape(N_HEADS_A // 2, npat, 3, 2 * qb, 2 * qb)


def _moba_num_tables():
    return -(-(REL_MAX_DIST + MOBA_BLOCK - 1) // MOBA_BLOCK) + 1


def _moba_tables(rel_bias):
    bs = MOBA_BLOCK
    off = jnp.arange(_moba_num_tables())[:, None, None]
    j = jnp.arange(bs)[None, :, None]
    i = jnp.arange(bs)[None, None, :]
    n = off * bs + i - j
    bias_cols = rel_bias[:, N_HEADS_A:].reshape(REL_BUCKETS, N_HEADS_B // 2, 2)
    t = _bias_lookup(n, bias_cols, "vijb,bnk->nkvij")
    return jnp.where((n >= 0)[None, None], t, NEG_INF)


def _dil_kernel(q1, k1, v1, q4, k4, v4, q16, k16, v16, tab_ref, o_ref, vext, oscr, sc_scr, *, seq):
    qb = DIL_BLOCK
    nblk = seq // qb
    qrefs, krefs, vrefs = (q1, q4, q16), (k1, k4, k16), (v1, v4, v16)
    dils = tuple(d for _, d in DILATED_PATTERNS)

    for p in range(3):
        v = vrefs[p][0]
        vext[p, :, :LANES] = v
        vext[p, :, LANES:] = jnp.ones_like(v)

    row2 = lax.broadcasted_iota(jnp.int32, (2 * qb, LANES), 0)
    lane2 = lax.broadcasted_iota(jnp.int32, (2 * qb, LANES), 1)
    own = (row2 < qb) == (lane2 < HEAD_DIM)
    lane1 = lax.broadcasted_iota(jnp.int32, (qb, LANES), 1)

    def band_start(blk):
        return pl.multiple_of(jnp.maximum(blk - 1, 0) * qb, qb)

    def score_block(blk, slot):
        q0 = pl.multiple_of(blk * qb, qb)
        for p in range(3):
            q = qrefs[p][0, pl.ds(q0, qb), :]
            q2 = jnp.concatenate([q, q], axis=0)
            qs = jnp.where(own, q2, jnp.zeros_like(q2))
            sc_scr[slot, p] = _dot_nt(qs, krefs[p][0, pl.ds(band_start(blk), 2 * qb), :])

    score_block(0, 0)

    def softmax_block(blk, slot):
        q0 = pl.multiple_of(blk * qb, qb)
        k0 = band_start(blk)
        for p in range(3):
            per = nblk // dils[p]
            variant = jnp.where(blk == 0, 2, jnp.where(blk % per == 0, 1, 0))
            s = sc_scr[slot, p] + tab_ref[0, p, variant]
            m = jnp.max(s, axis=-1, keepdims=True)
            pr = jnp.exp(s - m).astype(BF16)
            pv = _dot(pr, vext[p, pl.ds(k0, 2 * qb), :])
            den = pv[:, LANES:]
            on = pv[:, :LANES] / den
            lse = m + jnp.log(den)
            oscr[p, 0, pl.ds(q0, qb), :] = jnp.where(lane1 < HEAD_DIM, on[:qb], on[qb:])
            oscr[p, 1, pl.ds(q0, qb), :] = jnp.where(lane1 == 0, lse[:qb], lse[qb:])

    def block_pair(i, carry):
        b0 = 2 * i
        score_block(b0 + 1, 1)
        softmax_block(b0, 0)
        score_block(jnp.minimum(b0 + 2, nblk - 1), 0)
        softmax_block(b0 + 1, 1)
        return carry

    lax.fori_loop(0, nblk // 2, block_pair, 0)

    n16 = seq // 16
    n4 = seq // 4
    mc_rows = 128
    head_of_lane = (lax.broadcasted_iota(jnp.int32, (mc_rows, LANES), 1) >= HEAD_DIM).astype(jnp.int32)

    def merge(r, carry):
        for mc in range(n16 // mc_rows):
            m_lo = mc * mc_rows
            rows = (pl.ds(r + 16 * m_lo, mc_rows, stride=16),
                    pl.ds((r % 4) * n4 + r // 4 + 4 * m_lo, mc_rows, stride=4),
                    pl.ds(pl.multiple_of(r * n16 + m_lo, mc_rows), mc_rows))
            lse = [oscr[p, 1, rows[p], :] for p in range(3)]
            top = jnp.maximum(jnp.maximum(lse[0], lse[1]), lse[2])
            w = [jnp.exp(x - top) for x in lse]
            inv = 1.0 / (w[0] + w[1] + w[2])
            out = None
            for p in range(3):
                wl = jnp.take_along_axis(w[p] * inv, head_of_lane, axis=1)
                term = wl * oscr[p, 0, rows[p], :]
                out = term if out is None else out + term
            o_ref[0, rows[0], :] = out
        return carry

    lax.fori_loop(0, 16, merge, 0)


def _dilated(a1, a4, a16, tabs):
    B, _, S, _ = a1.shape
    npair = N_HEADS_A // 2
    nq = npair

    def col(off):
        return pl.BlockSpec((1, None, S, LANES), lambda b, p, off=off: (b, off + p, 0, 0))

    in_specs = []
    for _ in range(3):
        in_specs += [col(0), col(nq), col(2 * nq)]
    in_specs.append(pl.BlockSpec((1,) + tabs.shape[1:], lambda b, p: (p, 0, 0, 0, 0)))
    return pl.pallas_call(
        functools.partial(_dil_kernel, seq=S),
        grid=(B, npair),
        in_specs=in_specs,
        out_specs=pl.BlockSpec((1, None, S, LANES), lambda b, p: (b, p, 0, 0)),
        out_shape=jax.ShapeDtypeStruct((B, npair, S, LANES), F32),
        scratch_shapes=[pltpu.VMEM((3, S, 2 * LANES), BF16), pltpu.VMEM((3, 2, S, LANES), F32),
                        pltpu.VMEM((2, 3, 2 * DIL_BLOCK, 2 * DIL_BLOCK), F32)],
        compiler_params=pltpu.CompilerParams(dimension_semantics=("arbitrary", "arbitrary"),
                                             vmem_limit_bytes=VMEM_LIMIT),
        name="dilated",
    )(a1, a1, a1, a4, a4, a4, a16, a16, a16, tabs)


def _moba_kernel(q_ref, k_ref, v_ref, tab_ref, o_ref, vt_ext, s_scr, p_scr, g_scr, *, seq):
    bs = MOBA_BLOCK
    nb = seq // bs
    ntab = tab_ref.shape[2]

    vt = v_ref[0].astype(F32).T
    row_f = lax.broadcasted_iota(jnp.int32, vt.shape, 0)
    vt_ext[0] = jnp.where(row_f < HEAD_DIM, vt, 1.0).astype(BF16)
    vt_ext[1] = jnp.where(row_f >= HEAD_DIM, vt, 1.0).astype(BF16)

    kmean = jnp.sum(k_ref[0].astype(F32).reshape(nb, bs, LANES), axis=1) * (1.0 / bs)
    km_hi = kmean.astype(BF16)
    km_lo = (kmean - km_hi.astype(F32)).astype(BF16)

    lane_q = lax.broadcasted_iota(jnp.int32, (bs, LANES), 1)
    head_mask = (lane_q < HEAD_DIM, lane_q >= HEAD_DIM)
    blk_id = lax.broadcasted_iota(jnp.int32, (nb, bs), 0)
    row_o = lax.broadcasted_iota(jnp.int32, (LANES, bs), 0)

    def score_block(i):
        q = q_ref[0, i * bs:(i + 1) * bs, :]
        n_keys = (i + 1) * bs
        for h in range(2):
            qh = jnp.where(head_mask[h], q, jnp.zeros_like(q))
            g_scr[i % 2, h] = _dot_nt(km_hi, qh) + _dot_nt(km_lo, qh)
            s_scr[i % 2, h, :n_keys, :] = _dot_nt(k_ref[0, :n_keys, :], qh)

    def attend_block(i):
        slot = i % 2
        n_keys = (i + 1) * bs
        outs = []
        for h in range(2):
            past = blk_id < i
            g = jnp.where(past, g_scr[slot, h], NEG_INF)
            allow = blk_id == i
            for _ in range(MOBA_TOPK):
                mx = jnp.max(g, axis=0, keepdims=True)
                first = jnp.min(jnp.where(g == mx, blk_id, nb), axis=0, keepdims=True)
                pick = blk_id == first
                allow = allow | (pick & past)
                g = jnp.where(pick, NEG_INF, g)
            blk_bias = jnp.where(allow, 0.0, NEG_INF)

            mx8 = None
            for jt in range(i + 1):
                rows = slice(jt * bs, (jt + 1) * bs)
                t = s_scr[slot, h, rows, :] + tab_ref[0, h, min(i - jt, ntab - 1)] + blk_bias[jt:jt + 1, :]
                s_scr[slot, h, rows, :] = t
                t8 = jnp.max(t.reshape(bs // 8, 8, bs), axis=0)
                mx8 = t8 if mx8 is None else jnp.maximum(mx8, t8)
            m = jnp.max(mx8, axis=0, keepdims=True)
            for jt in range(i + 1):
                rows = slice(jt * bs, (jt + 1) * bs)
                p_scr[h, rows, :] = jnp.exp(s_scr[slot, h, rows, :] - m).astype(BF16)
            outs.append(_dot(vt_ext[h, :, :n_keys], p_scr[h, :n_keys, :]))
        den0 = outs[0][HEAD_DIM:HEAD_DIM + 1, :]
        den1 = outs[1][0:1, :]
        o_t = jnp.where(row_o < HEAD_DIM, outs[0] / den0, outs[1] / den1)
        o_ref[0, i * bs:(i + 1) * bs, :] = o_t.T

    score_block(0)
    for i in range(nb):
        if i + 1 < nb:
            score_block(i + 1)
        attend_block(i)


def _moba(b1, tabs):
    B, _, S, _ = b1.shape
    npair = N_HEADS_B // 2

    def col(off):
        return pl.BlockSpec((1, None, S, LANES), lambda b, p, off=off: (b, off + p, 0, 0))

    return pl.pallas_call(
        functools.partial(_moba_kernel, seq=S),
        grid=(B, npair),
        in_specs=[col(0), col(npair), col(2 * npair),
                  pl.BlockSpec((1,) + tabs.shape[1:], lambda b, p: (p, 0, 0, 0, 0))],
        out_specs=pl.BlockSpec((1, None, S, LANES), lambda b, p: (b, p, 0, 0)),
        out_shape=jax.ShapeDtypeStruct((B, npair, S, LANES), F32),
        scratch_shapes=[pltpu.VMEM((2, LANES, S), BF16), pltpu.VMEM((2, 2, S, MOBA_BLOCK), F32),
                        pltpu.VMEM((2, S, MOBA_BLOCK), BF16),
                        pltpu.VMEM((2, 2, S // MOBA_BLOCK, MOBA_BLOCK), F32)],
        compiler_params=pltpu.CompilerParams(dimension_semantics=("arbitrary", "arbitrary"),
                                             vmem_limit_bytes=VMEM_LIMIT),
        name="moba",
    )(b1, b1, b1, tabs)


def _post_kernel(x_ref, oa_ref, ob_ref, kv_ref, ga_ref, gb_ref, wout_ref, gc_ref, wcq_ref, wco_ref,
                 gf2_ref, wgu_ref, wd_ref, gfin_ref, y_ref, *, d_ff, final_norm):
    d = x_ref.shape[2]
    oa = jnp.concatenate([oa_ref[0, c] for c in range(oa_ref.shape[1])], axis=1)
    ob = jnp.concatenate([ob_ref[0, c] for c in range(ob_ref.shape[1])], axis=1)
    wa = oa.shape[1]
    ya = _rms(oa, ga_ref[...]).astype(BF16)
    yb = _rms(ob, gb_ref[...]).astype(BF16)
    x = x_ref[0] + _dot(ya, wout_ref[:wa, :]) + _dot(yb, wout_ref[wa:, :])

    h = _rms(x, gc_ref[...]).astype(BF16)
    q = _dot(h, wcq_ref[...])
    dh = d // CROSS_HEADS
    scale = dh ** -0.5
    scores = [_dot_nt((q[:, c * dh:(c + 1) * dh] * scale).astype(BF16), kv_ref[0, :, c * dh:(c + 1) * dh])
              for c in range(CROSS_HEADS)]
    heads = []
    for c, s in enumerate(scores):
        e = jnp.exp(s - jnp.max(s, axis=-1, keepdims=True))
        p = e / jnp.sum(e, axis=-1, keepdims=True)
        heads.append(_dot(p.astype(BF16), kv_ref[0, :, d + c * dh:d + (c + 1) * dh]).astype(BF16))
    x = x + _dot(jnp.concatenate(heads, axis=1), wco_ref[...])

    h = _rms(x, gf2_ref[...]).astype(BF16)
    x = x + 0.5 * _swiglu_half(h, wgu_ref, wd_ref, d_ff)
    y_ref[0] = _rms(x, gfin_ref[...]) if final_norm else x


def _post(x1, oa, ob, kv, g_out_a, g_out_b, w_out, g_cross, w_cq, w_co, g_ffn2, wgu, wd, g_final, final_norm):
    B, S, D = x1.shape
    ta, tb = oa.shape[1], ob.shape[1]
    wa, wb = ta * LANES, tb * LANES
    M = kv.shape[1]
    d_ff = wd.shape[0]
    tm = TOKEN_TILE
    tok = lambda w: pl.BlockSpec((1, tm, w), lambda b, i: (b, i, 0))
    slab = lambda t: pl.BlockSpec((1, t, tm, LANES), lambda b, i: (b, 0, i, 0))
    return pl.pallas_call(
        functools.partial(_post_kernel, d_ff=d_ff, final_norm=final_norm),
        grid=(B, S // tm),
        in_specs=[tok(D), slab(ta), slab(tb),
                  pl.BlockSpec((1, M, 2 * D), lambda b, i: (b, 0, 0)),
                  _resident((1, wa)), _resident((1, wb)), _resident((wa + wb, D)),
                  _resident((1, D)), _resident((D, D)), _resident((D, D)),
                  _resident((1, D)), _resident((D, 2 * d_ff)), _resident((d_ff, D)),
                  _resident((1, D))],
        out_specs=tok(D),
        out_shape=jax.ShapeDtypeStruct((B, S, D), F32),
        compiler_params=pltpu.CompilerParams(dimension_semantics=("arbitrary", "arbitrary"),
                                             vmem_limit_bytes=VMEM_LIMIT),
        name="post",
    )(x1, oa, ob, kv, g_out_a.reshape(1, wa), g_out_b.reshape(1, wb), w_out, g_cross.reshape(1, D),
      w_cq, w_co, g_ffn2.reshape(1, D), wgu, wd, g_final.reshape(1, D))


def kernel(x, mem, g_ffn1, w_ffn1_gu, w_ffn1_down, g_mix, w_in, rel_bias, g_out_a, g_out_b, w_out,
           g_cross, g_mem, w_cq, w_ckv, w_co, g_ffn2, w_ffn2_gu, w_ffn2_down, g_final):
    B, S, D = x.shape
    depth = g_ffn1.shape[0]
    dil_tabs = _dilated_tables(rel_bias)
    moba_tabs = _moba_tables(rel_bias)
    bf = lambda w: w.astype(BF16)
    wqa, wka = N_HEADS_A * HEAD_DIM, 3 * N_HEADS_A * HEAD_DIM
    col = jnp.arange(w_in.shape[-1])
    is_q = (col < wqa) | ((col >= wka) & (col < wka + N_HEADS_B * HEAD_DIM))
    q_scale = jnp.where(is_q, HEAD_DIM ** -0.5, 1.0).astype(F32)
    y = None
    for l in range(depth):
        kv = _memkv(mem, g_mem[l], bf(w_ckv[l]))
        x1 = _ffn1(x.reshape(B * S, D), g_ffn1[l], bf(w_ffn1_gu[l]), bf(w_ffn1_down[l])).reshape(B, S, D)
        a1, a4, a16, b1 = _proj(x1, g_mix[l], bf(w_in[l] * q_scale))
        oa = _dilated(a1, a4, a16, dil_tabs)
        ob = _moba(b1, moba_tabs)
        y = _post(x1, oa, ob, kv, g_out_a[l], g_out_b[l], bf(w_out[l]), g_cross[l], bf(w_cq[l]),
                  bf(w_co[l]), g_ffn2[l], bf(w_ffn2_gu[l]), bf(w_ffn2_down[l]), g_final,
                  final_norm=(l == depth - 1))
        x = y
    return y
```

```python
import functools
import math

import numpy as np
import jax
import jax.numpy as jnp
from jax import lax
from jax.experimental import pallas as pl
from jax.experimental.pallas import tpu as pltpu

F32 = jnp.float32
BF16 = jnp.bfloat16

HEAD_DIM = 64
LANES = 128
N_HEADS_A = 12
N_HEADS_B = 4
DILATED_PATTERNS = ((128, 1), (512, 4), (2048, 16))
DIL_BLOCK = 128
MOBA_BLOCK = 256
MOBA_TOPK = 3
CROSS_HEADS = 4
REL_BUCKETS = 32
REL_MAX_DIST = 2048
NORM_EPS = 1e-6
NEG_INF = -jnp.inf
VMEM_LIMIT = 56 * 1024 * 1024

FFN_CHUNK = 256
TOKEN_TILE = 512


def _rms(x, g):
    xf = x.astype(F32)
    return xf * lax.rsqrt(jnp.mean(xf * xf, axis=-1, keepdims=True) + NORM_EPS) * g


def _dot(a, b):
    return jnp.dot(a, b, preferred_element_type=F32)


def _dot_nt(a, b):
    return lax.dot_general(a, b, (((1,), (1,)), ((), ())), preferred_element_type=F32)


def _resident(shape):
    nd = len(shape)
    return pl.BlockSpec(shape, lambda *_: (0,) * nd, pipeline_mode=pl.Buffered(1))


def _swiglu_half(h, wgu_ref, wd_ref, d_ff):
    acc = None
    for c in range(d_ff // FFN_CHUNK):
        lo = c * FFN_CHUNK
        g = _dot(h, wgu_ref[:, lo:lo + FFN_CHUNK])
        u = _dot(h, wgu_ref[:, d_ff + lo:d_ff + lo + FFN_CHUNK])
        a = (g * (1.0 / (1.0 + jnp.exp(-g))) * u).astype(BF16)
        part = _dot(a, wd_ref[lo:lo + FFN_CHUNK, :])
        acc = part if acc is None else acc + part
    return acc


def _memkv_kernel(mem_ref, g_ref, w_ref, kv_ref):
    h = _rms(mem_ref[0], g_ref[...]).astype(BF16)
    kv_ref[0] = _dot(h, w_ref[...]).astype(BF16)


def _memkv(mem, g_mem, w_ckv):
    B, M, D = mem.shape
    N = w_ckv.shape[1]
    return pl.pallas_call(
        _memkv_kernel,
        grid=(B,),
        in_specs=[pl.BlockSpec((1, M, D), lambda b: (b, 0, 0)),
                  _resident((1, D)), _resident((D, N))],
        out_specs=pl.BlockSpec((1, M, N), lambda b: (b, 0, 0)),
        out_shape=jax.ShapeDtypeStruct((B, M, N), BF16),
        compiler_params=pltpu.CompilerParams(dimension_semantics=("arbitrary",),
                                             vmem_limit_bytes=VMEM_LIMIT),
        name="memkv",
    )(mem, g_mem.reshape(1, D), w_ckv)


def _ffn1_kernel(x_ref, g_ref, wgu_ref, wd_ref, o_ref, *, d_ff):
    x = x_ref[...]
    h = _rms(x, g_ref[...]).astype(BF16)
    o_ref[...] = x + 0.5 * _swiglu_half(h, wgu_ref, wd_ref, d_ff)


def _ffn1(x2d, g, wgu, wd):
    T, D = x2d.shape
    d_ff = wd.shape[0]
    tm = TOKEN_TILE
    return pl.pallas_call(
        functools.partial(_ffn1_kernel, d_ff=d_ff),
        grid=(T // tm,),
        in_specs=[pl.BlockSpec((tm, D), lambda i: (i, 0)),
                  _resident((1, D)), _resident((D, 2 * d_ff)), _resident((d_ff, D))],
        out_specs=pl.BlockSpec((tm, D), lambda i: (i, 0)),
        out_shape=jax.ShapeDtypeStruct((T, D), F32),
        compiler_params=pltpu.CompilerParams(dimension_semantics=("arbitrary",),
                                             vmem_limit_bytes=VMEM_LIMIT),
        name="ffn1",
    )(x2d, g.reshape(1, D), wgu, wd)


def _proj_kernel(x_ref, g_ref, w_ref, a1_ref, a4_ref, a16_ref, b_ref, scr, scr4, *, wa):
    tm = x_ref.shape[1]
    h = _rms(x_ref[0], g_ref[...]).astype(BF16)
    pb = _dot(h, w_ref[:, wa:])
    for c in range(pb.shape[1] // LANES):
        b_ref[0, c] = pb[:, c * LANES:(c + 1) * LANES].astype(BF16)
    group = 2 * LANES
    for c in range(wa // LANES):
        if c % (group // LANES) == 0:
            pa = _dot(h, w_ref[:, c * LANES:c * LANES + group])
        sub = slice((c * LANES) % group, (c * LANES) % group + LANES)
        tile = pa[:, sub]
        a1_ref[0, c] = tile.astype(BF16)
        b = c % 2
        scr[b] = tile
        for r in range(4):
            plane = scr[b, pl.ds(r, tm // 4, stride=4), :]
            a4_ref[0, c, r] = plane.astype(BF16)
            scr4[b, r] = plane
        for r in range(16):
            a16_ref[0, c, r] = scr4[b, r % 4, pl.ds(r // 4, tm // 16, stride=4), :].astype(BF16)


def _proj(x1, g, w_in):
    B, S, D = x1.shape
    wa = 3 * N_HEADS_A * HEAD_DIM
    wb = 3 * N_HEADS_B * HEAD_DIM
    ta, tb = wa // LANES, wb // LANES
    tm = TOKEN_TILE
    kern = functools.partial(_proj_kernel, wa=wa)
    a1, a4, a16, b1 = pl.pallas_call(
        kern,
        grid=(B, S // tm),
        in_specs=[pl.BlockSpec((1, tm, D), lambda b, i: (b, i, 0)),
                  _resident((1, D)), _resident((D, wa + wb))],
        out_specs=[pl.BlockSpec((1, ta, tm, LANES), lambda b, i: (b, 0, i, 0)),
                   pl.BlockSpec((1, ta, 4, tm // 4, LANES), lambda b, i: (b, 0, 0, i, 0)),
                   pl.BlockSpec((1, ta, 16, tm // 16, LANES), lambda b, i: (b, 0, 0, i, 0)),
                   pl.BlockSpec((1, tb, tm, LANES), lambda b, i: (b, 0, i, 0))],
        out_shape=[jax.ShapeDtypeStruct((B, ta, S, LANES), BF16),
                   jax.ShapeDtypeStruct((B, ta, 4, S // 4, LANES), BF16),
                   jax.ShapeDtypeStruct((B, ta, 16, S // 16, LANES), BF16),
                   jax.ShapeDtypeStruct((B, tb, S, LANES), BF16)],
        scratch_shapes=[pltpu.VMEM((2, tm, LANES), F32), pltpu.VMEM((2, 4, tm // 4, LANES), F32)],
        compiler_params=pltpu.CompilerParams(dimension_semantics=("arbitrary", "arbitrary"),
                                             vmem_limit_bytes=VMEM_LIMIT),
        name="proj",
    )(x1, g.reshape(1, D), w_in)
    return a1, a4.reshape(a1.shape), a16.reshape(a1.shape), b1


def _rel_bucket(dist):
    max_exact = REL_BUCKETS // 2
    n = jnp.maximum(dist, 0)
    nf = jnp.maximum(n, 1).astype(F32)
    large = max_exact + (jnp.log(nf / max_exact) / math.log(REL_MAX_DIST / max_exact)
                         * (REL_BUCKETS - max_exact)).astype(jnp.int32)
    large = jnp.minimum(large, REL_BUCKETS - 1)
    return jnp.where(n < max_exact, n, large)


def _bias_lookup(dist, bias_cols, spec):
    onehot = (_rel_bucket(dist)[..., None] == jnp.arange(REL_BUCKETS)).astype(F32)
    return jnp.einsum(spec, onehot, bias_cols.astype(F32), precision=lax.Precision.HIGHEST)


def _dilated_tables(rel_bias):
    qb = DIL_BLOCK
    npat = len(DILATED_PATTERNS)
    dil = jnp.array([d for _, d in DILATED_PATTERNS], jnp.int32)[:, None, None, None]
    n_steps = jnp.array([w // d for w, d in DILATED_PATTERNS], jnp.int32)[:, None, None, None]
    variant = jnp.arange(3)[None, :, None, None]
    i = jnp.arange(qb)[None, None, :, None]
    j = jnp.arange(2 * qb)[None, None, None, :]
    delta = jnp.where(variant == 2, i - j, qb + i - j)
    ok = (delta >= 0) & (delta <= n_steps) & ((variant != 1) | (j >= qb))
    bias_cols = rel_bias[:, :N_HEADS_A].reshape(REL_BUCKETS, N_HEADS_A // 2, 2)
    t = _bias_lookup(delta * dil, bias_cols, "pvijb,bnk->npvkij")
    t = jnp.where(ok[None, :, :, None, :, :], t, NEG_INF)
    return t.reshape(N_HEADS_A // 2, npat, 3, 2 * qb, 2 * qb)


def _moba_num_tables():
    return -(-(REL_MAX_DIST + MOBA_BLOCK - 1) // MOBA_BLOCK) + 1


def _moba_tables(rel_bias):
    bs = MOBA_BLOCK
    off = jnp.arange(_moba_num_tables())[:, None, None]
    j = jnp.arange(bs)[None, :, None]
    i = jnp.arange(bs)[None, None, :]
    n = off * bs + i - j
    bias_cols = rel_bias[:, N_HEADS_A:].reshape(REL_BUCKETS, N_HEADS_B // 2, 2)
    t = _bias_lookup(n, bias_cols, "vijb,bnk->nkvij")
    return jnp.where((n >= 0)[None, None], t, NEG_INF)


def _dil_kernel(q1, k1, v1, q4, k4, v4, q16, k16, v16, tab_ref, o_ref, oscr, sc_scr, *, seq):
    qb = DIL_BLOCK
    nblk = seq // qb
    qrefs, krefs, vrefs = (q1, q4, q16), (k1, k4, k16), (v1, v4, v16)
    dils = tuple(d for _, d in DILATED_PATTERNS)
    ones_tile = jnp.ones((2 * qb, LANES), BF16)

    row2 = lax.broadcasted_iota(jnp.int32, (2 * qb, LANES), 0)
    lane2 = lax.broadcasted_iota(jnp.int32, (2 * qb, LANES), 1)
    own = (row2 < qb) == (lane2 < HEAD_DIM)
    lane1 = lax.broadcasted_iota(jnp.int32, (qb, LANES), 1)

    def band_start(blk):
        return pl.multiple_of(jnp.maximum(blk - 1, 0) * qb, qb)

    def score_block(blk, slot):
        q0 = pl.multiple_of(blk * qb, qb)
        for p in range(3):
            q = qrefs[p][0, pl.ds(q0, qb), :]
            q2 = jnp.concatenate([q, q], axis=0)
            qs = jnp.where(own, q2, jnp.zeros_like(q2))
            sc_scr[slot, p] = _dot_nt(qs, krefs[p][0, pl.ds(band_start(blk), 2 * qb), :])

    score_block(0, 0)

    def softmax_block(blk, slot):
        q0 = pl.multiple_of(blk * qb, qb)
        k0 = band_start(blk)
        for p in range(3):
            per = nblk // dils[p]
            variant = jnp.where(blk == 0, 2, jnp.where(blk % per == 0, 1, 0))
            s = sc_scr[slot, p] + tab_ref[0, p, variant]
            m = jnp.max(s, axis=-1, keepdims=True)
            pr = jnp.exp(s - m).astype(BF16)
            pv = _dot(pr, jnp.concatenate([vrefs[p][0, pl.ds(k0, 2 * qb), :], ones_tile], axis=1))
            den = pv[:, LANES:]
            on = pv[:, :LANES] / den
            lse = m + jnp.log(den)
            oscr[p, 0, pl.ds(q0, qb), :] = jnp.where(lane1 < HEAD_DIM, on[:qb], on[qb:])
            oscr[p, 1, pl.ds(q0, qb), :] = jnp.where(lane1 == 0, lse[:qb], lse[qb:])

    def block_pair(i, carry):
        b0 = 2 * i
        score_block(b0 + 1, 1)
        softmax_block(b0, 0)
        score_block(jnp.minimum(b0 + 2, nblk - 1), 0)
        softmax_block(b0 + 1, 1)
        return carry

    lax.fori_loop(0, nblk // 2, block_pair, 0)

    n16 = seq // 16
    n4 = seq // 4
    mc_rows = 128
    head_of_lane = (lax.broadcasted_iota(jnp.int32, (mc_rows, LANES), 1) >= HEAD_DIM).astype(jnp.int32)

    def merge(r, carry):
        for mc in range(n16 // mc_rows):
            m_lo = mc * mc_rows
            rows = (pl.ds(r + 16 * m_lo, mc_rows, stride=16),
                    pl.ds((r % 4) * n4 + r // 4 + 4 * m_lo, mc_rows, stride=4),
                    pl.ds(pl.multiple_of(r * n16 + m_lo, mc_rows), mc_rows))
            lse = [oscr[p, 1, rows[p], :] for p in range(3)]
            top = jnp.maximum(jnp.maximum(lse[0], lse[1]), lse[2])
            w = [jnp.exp(x - top) for x in lse]
            inv = 1.0 / (w[0] + w[1] + w[2])
            out = None
            for p in range(3):
                wl = jnp.take_along_axis(w[p] * inv, head_of_lane, axis=1)
                term = wl * oscr[p, 0, rows[p], :]
                out = term if out is None else out + term
            o_ref[0, rows[0], :] = out
        return carry

    lax.fori_loop(0, 16, merge, 0)


def _dilated(a1, a4, a16, tabs):
    B, _, S, _ = a1.shape
    npair = N_HEADS_A // 2
    nq = npair

    def col(off):
        return pl.BlockSpec((1, None, S, LANES), lambda b, p, off=off: (b, off + p, 0, 0))

    in_specs = []
    for _ in range(3):
        in_specs += [col(0), col(nq), col(2 * nq)]
    in_specs.append(pl.BlockSpec((1,) + tabs.shape[1:], lambda b, p: (p, 0, 0, 0, 0)))
    return pl.pallas_call(
        functools.partial(_dil_kernel, seq=S),
        grid=(B, npair),
        in_specs=in_specs,
        out_specs=pl.BlockSpec((1, None, S, LANES), lambda b, p: (b, p, 0, 0)),
        out_shape=jax.ShapeDtypeStruct((B, npair, S, LANES), F32),
        scratch_shapes=[pltpu.VMEM((3, 2, S, LANES), F32),
                        pltpu.VMEM((2, 3, 2 * DIL_BLOCK, 2 * DIL_BLOCK), F32)],
        compiler_params=pltpu.CompilerParams(dimension_semantics=("arbitrary", "arbitrary"),
                                             vmem_limit_bytes=VMEM_LIMIT),
        name="dilated",
    )(a1, a1, a1, a4, a4, a4, a16, a16, a16, tabs)


def _moba_kernel(q_ref, k_ref, v_ref, tab_ref, o_ref, vt_ext, s_scr, p_scr, g_scr, *, seq):
    bs = MOBA_BLOCK
    nb = seq // bs
    ntab = tab_ref.shape[2]

    vt = v_ref[0].astype(F32).T
    row_f = lax.broadcasted_iota(jnp.int32, vt.shape, 0)
    vt_ext[0] = jnp.where(row_f < HEAD_DIM, vt, 1.0).astype(BF16)
    vt_ext[1] = jnp.where(row_f >= HEAD_DIM, vt, 1.0).astype(BF16)

    kmean = jnp.sum(k_ref[0].astype(F32).reshape(nb, bs, LANES), axis=1) * (1.0 / bs)
    km_hi = kmean.astype(BF16)
    km_lo = (kmean - km_hi.astype(F32)).astype(BF16)

    lane_q = lax.broadcasted_iota(jnp.int32, (bs, LANES), 1)
    head_mask = (lane_q < HEAD_DIM, lane_q >= HEAD_DIM)
    blk_id = lax.broadcasted_iota(jnp.int32, (nb, bs), 0)
    row_o = lax.broadcasted_iota(jnp.int32, (LANES, bs), 0)

    def score_block(i):
        q = q_ref[0, i * bs:(i + 1) * bs, :]
        n_keys = (i + 1) * bs
        for h in range(2):
            qh = jnp.where(head_mask[h], q, jnp.zeros_like(q))
            g_scr[i % 2, h] = _dot_nt(km_hi, qh) + _dot_nt(km_lo, qh)
            s_scr[i % 2, h, :n_keys, :] = _dot_nt(k_ref[0, :n_keys, :], qh)

    def attend_block(i):
        slot = i % 2
        n_keys = (i + 1) * bs
        outs = []
        for h in range(2):
            past = blk_id < i
            g = jnp.where(past, g_scr[slot, h], NEG_INF)
            allow = blk_id == i
            for _ in range(MOBA_TOPK):
                mx = jnp.max(g, axis=0, keepdims=True)
                first = jnp.min(jnp.where(g == mx, blk_id, nb), axis=0, keepdims=True)
                pick = blk_id == first
                allow = allow | (pick & past)
                g = jnp.where(pick, NEG_INF, g)
            blk_bias = jnp.where(allow, 0.0, NEG_INF)

            mx8 = None
            for jt in range(i + 1):
                rows = slice(jt * bs, (jt + 1) * bs)
                t = s_scr[slot, h, rows, :] + tab_ref[0, h, min(i - jt, ntab - 1)] + blk_bias[jt:jt + 1, :]
                s_scr[slot, h, rows, :] = t
                t8 = jnp.max(t.reshape(bs // 8, 8, bs), axis=0)
                mx8 = t8 if mx8 is None else jnp.maximum(mx8, t8)
            m = jnp.max(mx8, axis=0, keepdims=True)
            for jt in range(i + 1):
                rows = slice(jt * bs, (jt + 1) * bs)
                p_scr[h, rows, :] = jnp.exp(s_scr[slot, h, rows, :] - m).astype(BF16)
            outs.append(_dot(vt_ext[h, :, :n_keys], p_scr[h, :n_keys, :]))
        den0 = outs[0][HEAD_DIM:HEAD_DIM + 1, :]
        den1 = outs[1][0:1, :]
        o_t = jnp.where(row_o < HEAD_DIM, outs[0] / den0, outs[1] / den1)
        o_ref[0, i * bs:(i + 1) * bs, :] = o_t.T

    score_block(0)
    for i in range(nb):
        if i + 1 < nb:
            score_block(i + 1)
        attend_block(i)


def _moba(b1, tabs):
    B, _, S, _ = b1.shape
    npair = N_HEADS_B // 2

    def col(off):
        return pl.BlockSpec((1, None, S, LANES), lambda b, p, off=off: (b, off + p, 0, 0))

    return pl.pallas_call(
        functools.partial(_moba_kernel, seq=S),
        grid=(B, npair),
        in_specs=[col(0), col(npair), col(2 * npair),
                  pl.BlockSpec((1,) + tabs.shape[1:], lambda b, p: (p, 0, 0, 0, 0))],
        out_specs=pl.BlockSpec((1, None, S, LANES), lambda b, p: (b, p, 0, 0)),
        out_shape=jax.ShapeDtypeStruct((B, npair, S, LANES), F32),
        scratch_shapes=[pltpu.VMEM((2, LANES, S), BF16), pltpu.VMEM((2, 2, S, MOBA_BLOCK), F32),
                        pltpu.VMEM((2, S, MOBA_BLOCK), BF16),
                        pltpu.VMEM((2, 2, S // MOBA_BLOCK, MOBA_BLOCK), F32)],
        compiler_params=pltpu.CompilerParams(dimension_semantics=("arbitrary", "arbitrary"),
                                             vmem_limit_bytes=VMEM_LIMIT),
        name="moba",
    )(b1, b1, b1, tabs)


def _post_kernel(x_ref, oa_ref, ob_ref, kv_ref, ga_ref, gb_ref, wout_ref, gc_ref, wcq_ref, wco_ref,
                 gf2_ref, wgu_ref, wd_ref, gfin_ref, y_ref, *, d_ff, final_norm):
    d = x_ref.shape[2]
    oa = jnp.concatenate([oa_ref[0, c] for c in range(oa_ref.shape[1])], axis=1)
    ob = jnp.concatenate([ob_ref[0, c] for c in range(ob_ref.shape[1])], axis=1)
    wa = oa.shape[1]
    ya = _rms(oa, ga_ref[...]).astype(BF16)
    yb = _rms(ob, gb_ref[...]).astype(BF16)
    x = x_ref[0] + _dot(ya, wout_ref[:wa, :]) + _dot(yb, wout_ref[wa:, :])

    h = _rms(x, gc_ref[...]).astype(BF16)
    q = _dot(h, wcq_ref[...])
    dh = d // CROSS_HEADS
    scale = dh ** -0.5
    scores = [_dot_nt((q[:, c * dh:(c + 1) * dh] * scale).astype(BF16), kv_ref[0, :, c * dh:(c + 1) * dh])
              for c in range(CROSS_HEADS)]
    heads = []
    for c, s in enumerate(scores):
        e = jnp.exp(s - jnp.max(s, axis=-1, keepdims=True))
        p = e / jnp.sum(e, axis=-1, keepdims=True)
        heads.append(_dot(p.astype(BF16), kv_ref[0, :, d + c * dh:d + (c + 1) * dh]).astype(BF16))
    x = x + _dot(jnp.concatenate(heads, axis=1), wco_ref[...])

    h = _rms(x, gf2_ref[...]).astype(BF16)
    x = x + 0.5 * _swiglu_half(h, wgu_ref, wd_ref, d_ff)
    y_ref[0] = _rms(x, gfin_ref[...]) if final_norm else x


def _post(x1, oa, ob, kv, g_out_a, g_out_b, w_out, g_cross, w_cq, w_co, g_ffn2, wgu, wd, g_final, final_norm):
    B, S, D = x1.shape
    ta, tb = oa.shape[1], ob.shape[1]
    wa, wb = ta * LANES, tb * LANES
    M = kv.shape[1]
    d_ff = wd.shape[0]
    tm = TOKEN_TILE
    tok = lambda w: pl.BlockSpec((1, tm, w), lambda b, i: (b, i, 0))
    slab = lambda t: pl.BlockSpec((1, t, tm, LANES), lambda b, i: (b, 0, i, 0))
    return pl.pallas_call(
        functools.partial(_post_kernel, d_ff=d_ff, final_norm=final_norm),
        grid=(B, S // tm),
        in_specs=[tok(D), slab(ta), slab(tb),
                  pl.BlockSpec((1, M, 2 * D), lambda b, i: (b, 0, 0)),
                  _resident((1, wa)), _resident((1, wb)), _resident((wa + wb, D)),
                  _resident((1, D)), _resident((D, D)), _resident((D, D)),
                  _resident((1, D)), _resident((D, 2 * d_ff)), _resident((d_ff, D)),
                  _resident((1, D))],
        out_specs=tok(D),
        out_shape=jax.ShapeDtypeStruct((B, S, D), F32),
        compiler_params=pltpu.CompilerParams(dimension_semantics=("arbitrary", "arbitrary"),
                                             vmem_limit_bytes=VMEM_LIMIT),
        name="post",
    )(x1, oa, ob, kv, g_out_a.reshape(1, wa), g_out_b.reshape(1, wb), w_out, g_cross.reshape(1, D),
      w_cq, w_co, g_ffn2.reshape(1, D), wgu, wd, g_final.reshape(1, D))


def kernel(x, mem, g_ffn1, w_ffn1_gu, w_ffn1_down, g_mix, w_in, rel_bias, g_out_a, g_out_b, w_out,
           g_cross, g_mem, w_cq, w_ckv, w_co, g_ffn2, w_ffn2_gu, w_ffn2_down, g_final):
    B, S, D = x.shape
    depth = g_ffn1.shape[0]
    dil_tabs = _dilated_tables(rel_bias)
    moba_tabs = _moba_tables(rel_bias)
    bf = lambda w: w.astype(BF16)
    wqa, wka = N_HEADS_A * HEAD_DIM, 3 * N_HEADS_A * HEAD_DIM
    col = jnp.arange(w_in.shape[-1])
    is_q = (col < wqa) | ((col >= wka) & (col < wka + N_HEADS_B * HEAD_DIM))
    q_scale = jnp.where(is_q, HEAD_DIM ** -0.5, 1.0).astype(F32)
    y = None
    for l in range(depth):
        kv = _memkv(mem, g_mem[l], bf(w_ckv[l]))
        x1 = _ffn1(x.reshape(B * S, D), g_ffn1[l], bf(w_ffn1_gu[l]), bf(w_ffn1_down[l])).reshape(B, S, D)
        a1, a4, a16, b1 = _proj(x1, g_mix[l], bf(w_in[l] * q_scale))
        oa = _dilated(a1, a4, a16, dil_tabs)
        ob = _moba(b1, moba_tabs)
        y = _post(x1, oa, ob, kv, g_out_a[l], g_out_b[l], bf(w_out[l]), g_cross[l], bf(w_cq[l]),
                  bf(w_co[l]), g_ffn2[l], bf(w_ffn2_gu[l]), bf(w_ffn2_down[l]), g_final,
                  final_norm=(l == depth - 1))
        x = y
    return y
```

```python
import functools
import math

import numpy as np
import jax
import jax.numpy as jnp
from jax import lax
from jax.experimental import pallas as pl
from jax.experimental.pallas import tpu as pltpu

F32 = jnp.float32
BF16 = jnp.bfloat16

HEAD_DIM = 64
LANES = 128
N_HEADS_A = 12
N_HEADS_B = 4
DILATED_PATTERNS = ((128, 1), (512, 4), (2048, 16))
DIL_BLOCK = 128
MOBA_BLOCK = 256
MOBA_TOPK = 3
CROSS_HEADS = 4
REL_BUCKETS = 32
REL_MAX_DIST = 2048
NORM_EPS = 1e-6
NEG_INF = -jnp.inf
VMEM_LIMIT = 56 * 1024 * 1024

FFN_CHUNK = 256
TOKEN_TILE = 512


def _rms(x, g):
    xf = x.astype(F32)
    return xf * lax.rsqrt(jnp.mean(xf * xf, axis=-1, keepdims=True) + NORM_EPS) * g


def _dot(a, b):
    return jnp.dot(a, b, preferred_element_type=F32)


def _dot_nt(a, b):
    return lax.dot_general(a, b, (((1,), (1,)), ((), ())), preferred_element_type=F32)


def _resident(shape):
    nd = len(shape)
    return pl.BlockSpec(shape, lambda *_: (0,) * nd, pipeline_mode=pl.Buffered(1))


def _swiglu_half(h, wgu_ref, wd_ref, d_ff):
    acc = None
    for c in range(d_ff // FFN_CHUNK):
        lo = c * FFN_CHUNK
        g = _dot(h, wgu_ref[:, lo:lo + FFN_CHUNK])
        u = _dot(h, wgu_ref[:, d_ff + lo:d_ff + lo + FFN_CHUNK])
        a = (g * (1.0 / (1.0 + jnp.exp(-g))) * u).astype(BF16)
        part = _dot(a, wd_ref[lo:lo + FFN_CHUNK, :])
        acc = part if acc is None else acc + part
    return acc


def _memkv_kernel(mem_ref, g_ref, w_ref, kv_ref):
    h = _rms(mem_ref[0], g_ref[...]).astype(BF16)
    kv_ref[0] = _dot(h, w_ref[...]).astype(BF16)


def _memkv(mem, g_mem, w_ckv):
    B, M, D = mem.shape
    N = w_ckv.shape[1]
    return pl.pallas_call(
        _memkv_kernel,
        grid=(B,),
        in_specs=[pl.BlockSpec((1, M, D), lambda b: (b, 0, 0)),
                  _resident((1, D)), _resident((D, N))],
        out_specs=pl.BlockSpec((1, M, N), lambda b: (b, 0, 0)),
        out_shape=jax.ShapeDtypeStruct((B, M, N), BF16),
        compiler_params=pltpu.CompilerParams(dimension_semantics=("arbitrary",),
                                             vmem_limit_bytes=VMEM_LIMIT),
        name="memkv",
    )(mem, g_mem.reshape(1, D), w_ckv)


def _ffn1_kernel(x_ref, g_ref, wgu_ref, wd_ref, o_ref, *, d_ff):
    x = x_ref[...]
    h = _rms(x, g_ref[...]).astype(BF16)
    o_ref[...] = x + 0.5 * _swiglu_half(h, wgu_ref, wd_ref, d_ff)


def _ffn1(x2d, g, wgu, wd):
    T, D = x2d.shape
    d_ff = wd.shape[0]
    tm = TOKEN_TILE
    return pl.pallas_call(
        functools.partial(_ffn1_kernel, d_ff=d_ff),
        grid=(T // tm,),
        in_specs=[pl.BlockSpec((tm, D), lambda i: (i, 0)),
                  _resident((1, D)), _resident((D, 2 * d_ff)), _resident((d_ff, D))],
        out_specs=pl.BlockSpec((tm, D), lambda i: (i, 0)),
        out_shape=jax.ShapeDtypeStruct((T, D), F32),
        compiler_params=pltpu.CompilerParams(dimension_semantics=("arbitrary",),
                                             vmem_limit_bytes=VMEM_LIMIT),
        name="ffn1",
    )(x2d, g.reshape(1, D), wgu, wd)


def _proj_kernel(x_ref, g_ref, w_ref, a1_ref, a4_ref, a16_ref, b_ref, scr, scr4, *, wa):
    tm = x_ref.shape[1]
    h = _rms(x_ref[0], g_ref[...]).astype(BF16)
    pb = _dot(h, w_ref[:, wa:])
    for c in range(pb.shape[1] // LANES):
        b_ref[0, c] = pb[:, c * LANES:(c + 1) * LANES].astype(BF16)
    group = 2 * LANES
    for c in range(wa // LANES):
        if c % (group // LANES) == 0:
            pa = _dot(h, w_ref[:, c * LANES:c * LANES + group])
        sub = slice((c * LANES) % group, (c * LANES) % group + LANES)
        tile = pa[:, sub]
        a1_ref[0, c] = tile.astype(BF16)
        b = c % 2
        scr[b] = tile
        for r in range(4):
            plane = scr[b, pl.ds(r, tm // 4, stride=4), :]
            a4_ref[0, c, r] = plane.astype(BF16)
            scr4[b, r] = plane
        for r in range(16):
            a16_ref[0, c, r] = scr4[b, r % 4, pl.ds(r // 4, tm // 16, stride=4), :].astype(BF16)


def _proj(x1, g, w_in):
    B, S, D = x1.shape
    wa = 3 * N_HEADS_A * HEAD_DIM
    wb = 3 * N_HEADS_B * HEAD_DIM
    ta, tb = wa // LANES, wb // LANES
    tm = TOKEN_TILE
    kern = functools.partial(_proj_kernel, wa=wa)
    a1, a4, a16, b1 = pl.pallas_call(
        kern,
        grid=(B, S // tm),
        in_specs=[pl.BlockSpec((1, tm, D), lambda b, i: (b, i, 0)),
                  _resident((1, D)), _resident((D, wa + wb))],
        out_specs=[pl.BlockSpec((1, ta, tm, LANES), lambda b, i: (b, 0, i, 0)),
                   pl.BlockSpec((1, ta, 4, tm // 4, LANES), lambda b, i: (b, 0, 0, i, 0)),
                   pl.BlockSpec((1, ta, 16, tm // 16, LANES), lambda b, i: (b, 0, 0, i, 0)),
                   pl.BlockSpec((1, tb, tm, LANES), lambda b, i: (b, 0, i, 0))],
        out_shape=[jax.ShapeDtypeStruct((B, ta, S, LANES), BF16),
                   jax.ShapeDtypeStruct((B, ta, 4, S // 4, LANES), BF16),
                   jax.ShapeDtypeStruct((B, ta, 16, S // 16, LANES), BF16),
                   jax.ShapeDtypeStruct((B, tb, S, LANES), BF16)],
        scratch_shapes=[pltpu.VMEM((2, tm, LANES), F32), pltpu.VMEM((2, 4, tm // 4, LANES), F32)],
        compiler_params=pltpu.CompilerParams(dimension_semantics=("arbitrary", "arbitrary"),
                                             vmem_limit_bytes=VMEM_LIMIT),
        name="proj",
    )(x1, g.reshape(1, D), w_in)
    return a1, a4.reshape(a1.shape), a16.reshape(a1.shape), b1


def _rel_bucket(dist):
    max_exact = REL_BUCKETS // 2
    n = jnp.maximum(dist, 0)
    nf = jnp.maximum(n, 1).astype(F32)
    large = max_exact + (jnp.log(nf / max_exact) / math.log(REL_MAX_DIST / max_exact)
                         * (REL_BUCKETS - max_exact)).astype(jnp.int32)
    large = jnp.minimum(large, REL_BUCKETS - 1)
    return jnp.where(n < max_exact, n, large)


def _bias_lookup(dist, bias_cols, spec):
    onehot = (_rel_bucket(dist)[..., None] == jnp.arange(REL_BUCKETS)).astype(F32)
    return jnp.einsum(spec, onehot, bias_cols.astype(F32), precision=lax.Precision.HIGHEST)


def _dilated_tables(rel_bias):
    qb = DIL_BLOCK
    npat = len(DILATED_PATTERNS)
    dil = jnp.array([d for _, d in DILATED_PATTERNS], jnp.int32)[:, None, None, None]
    n_steps = jnp.array([w // d for w, d in DILATED_PATTERNS], jnp.int32)[:, None, None, None]
    variant = jnp.arange(3)[None, :, None, None]
    i = jnp.arange(qb)[None, None, :, None]
    j = jnp.arange(2 * qb)[None, None, None, :]
    delta = jnp.where(variant == 2, i - j, qb + i - j)
    ok = (delta >= 0) & (delta <= n_steps) & ((variant != 1) | (j >= qb))
    bias_cols = rel_bias[:, :N_HEADS_A].reshape(REL_BUCKETS, N_HEADS_A // 2, 2)
    t = _bias_lookup(delta * dil, bias_cols, "pvijb,bnk->npvkij")
    t = jnp.where(ok[None, :, :, None, :, :], t, NEG_INF)
    return t.reshape(N_HEADS_A // 2, npat, 3, 2 * qb, 2 * qb)


def _moba_num_tables():
    return -(-(REL_MAX_DIST + MOBA_BLOCK - 1) // MOBA_BLOCK) + 1


def _moba_tables(rel_bias):
    bs = MOBA_BLOCK
    off = jnp.arange(_moba_num_tables())[:, None, None]
    j = jnp.arange(bs)[None, :, None]
    i = jnp.arange(bs)[None, None, :]
    n = off * bs + i - j
    bias_cols = rel_bias[:, N_HEADS_A:].reshape(REL_BUCKETS, N_HEADS_B // 2, 2)
    t = _bias_lookup(n, bias_cols, "vijb,bnk->nkvij")
    return jnp.where((n >= 0)[None, None], t, NEG_INF)


def _dil_kernel(q1, k1, v1, q4, k4, v4, q16, k16, v16, tab_ref, o_ref, oscr, sc_scr, *, seq):
    qb = DIL_BLOCK
    nblk = seq // qb
    qrefs, krefs, vrefs = (q1, q4, q16), (k1, k4, k16), (v1, v4, v16)
    dils = tuple(d for _, d in DILATED_PATTERNS)
    ones_tile = jnp.ones((2 * qb, LANES), BF16)

    row2 = lax.broadcasted_iota(jnp.int32, (2 * qb, LANES), 0)
    lane2 = lax.broadcasted_iota(jnp.int32, (2 * qb, LANES), 1)
    own = (row2 < qb) == (lane2 < HEAD_DIM)
    lane1 = lax.broadcasted_iota(jnp.int32, (qb, LANES), 1)

    def band_start(blk):
        return pl.multiple_of(jnp.maximum(blk - 1, 0) * qb, qb)

    def score_block(blk, slot):
        q0 = pl.multiple_of(blk * qb, qb)
        for p in range(3):
            q = qrefs[p][0, pl.ds(q0, qb), :]
            q2 = jnp.concatenate([q, q], axis=0)
            qs = jnp.where(own, q2, jnp.zeros_like(q2))
            sc_scr[slot, p] = _dot_nt(qs, krefs[p][0, pl.ds(band_start(blk), 2 * qb), :])

    score_block(0, 0)

    def softmax_block(blk, slot):
        q0 = pl.multiple_of(blk * qb, qb)
        k0 = band_start(blk)
        for p in range(3):
            per = nblk // dils[p]
            variant = jnp.where(blk == 0, 2, jnp.where(blk % per == 0, 1, 0))
            s = sc_scr[slot, p] + tab_ref[0, p, variant]
            m = jnp.max(s, axis=-1, keepdims=True)
            pr = jnp.exp(s - m).astype(BF16)
            pv = _dot(pr, jnp.concatenate([vrefs[p][0, pl.ds(k0, 2 * qb), :], ones_tile], axis=1))
            h0 = lane1 < HEAD_DIM
            num = jnp.where(h0, pv[:qb, :LANES], pv[qb:, :LANES])
            den = jnp.where(h0, pv[:qb, LANES:], pv[qb:, LANES:])
            oscr[p, 0, pl.ds(q0, qb), :] = num / den
            oscr[p, 1, pl.ds(q0, qb), :] = jnp.where(h0, m[:qb], m[qb:]) + jnp.log(den)

    def block_pair(i, carry):
        b0 = 2 * i
        score_block(b0 + 1, 1)
        softmax_block(b0, 0)
        score_block(jnp.minimum(b0 + 2, nblk - 1), 0)
        softmax_block(b0 + 1, 1)
        return carry

    lax.fori_loop(0, nblk // 2, block_pair, 0)

    n16 = seq // 16
    n4 = seq // 4
    mc_rows = 128

    def merge(r, carry):
        for mc in range(n16 // mc_rows):
            m_lo = mc * mc_rows
            rows = (pl.ds(r + 16 * m_lo, mc_rows, stride=16),
                    pl.ds((r % 4) * n4 + r // 4 + 4 * m_lo, mc_rows, stride=4),
                    pl.ds(pl.multiple_of(r * n16 + m_lo, mc_rows), mc_rows))
            lse = [oscr[p, 1, rows[p], :] for p in range(3)]
            top = jnp.maximum(jnp.maximum(lse[0], lse[1]), lse[2])
            w = [jnp.exp(x - top) for x in lse]
            inv = 1.0 / (w[0] + w[1] + w[2])
            out = None
            for p in range(3):
                term = (w[p] * inv) * oscr[p, 0, rows[p], :]
                out = term if out is None else out + term
            o_ref[0, rows[0], :] = out
        return carry

    lax.fori_loop(0, 16, merge, 0)


def _dilated(a1, a4, a16, tabs):
    B, _, S, _ = a1.shape
    npair = N_HEADS_A // 2
    nq = npair

    def col(off):
        return pl.BlockSpec((1, None, S, LANES), lambda b, p, off=off: (b, off + p, 0, 0))

    in_specs = []
    for _ in range(3):
        in_specs += [col(0), col(nq), col(2 * nq)]
    in_specs.append(pl.BlockSpec((1,) + tabs.shape[1:], lambda b, p: (p, 0, 0, 0, 0)))
    return pl.pallas_call(
        functools.partial(_dil_kernel, seq=S),
        grid=(B, npair),
        in_specs=in_specs,
        out_specs=pl.BlockSpec((1, None, S, LANES), lambda b, p: (b, p, 0, 0)),
        out_shape=jax.ShapeDtypeStruct((B, npair, S, LANES), F32),
        scratch_shapes=[pltpu.VMEM((3, 2, S, LANES), F32),
                        pltpu.VMEM((2, 3, 2 * DIL_BLOCK, 2 * DIL_BLOCK), F32)],
        compiler_params=pltpu.CompilerParams(dimension_semantics=("arbitrary", "arbitrary"),
                                             vmem_limit_bytes=VMEM_LIMIT),
        name="dilated",
    )(a1, a1, a1, a4, a4, a4, a16, a16, a16, tabs)


def _moba_kernel(q_ref, k_ref, v_ref, tab_ref, o_ref, vt_ext, s_scr, p_scr, g_scr, *, seq):
    bs = MOBA_BLOCK
    nb = seq // bs
    ntab = tab_ref.shape[2]

    vt = v_ref[0].astype(F32).T
    row_f = lax.broadcasted_iota(jnp.int32, vt.shape, 0)
    vt_ext[0] = jnp.where(row_f < HEAD_DIM, vt, 1.0).astype(BF16)
    vt_ext[1] = jnp.where(row_f >= HEAD_DIM, vt, 1.0).astype(BF16)

    kmean = jnp.sum(k_ref[0].astype(F32).reshape(nb, bs, LANES), axis=1) * (1.0 / bs)
    km_hi = kmean.astype(BF16)
    km_lo = (kmean - km_hi.astype(F32)).astype(BF16)

    lane_q = lax.broadcasted_iota(jnp.int32, (bs, LANES), 1)
    head_mask = (lane_q < HEAD_DIM, lane_q >= HEAD_DIM)
    blk_id = lax.broadcasted_iota(jnp.int32, (nb, bs), 0)
    row_o = lax.broadcasted_iota(jnp.int32, (LANES, bs), 0)

    def score_block(i):
        q = q_ref[0, i * bs:(i + 1) * bs, :]
        n_keys = (i + 1) * bs
        for h in range(2):
            qh = jnp.where(head_mask[h], q, jnp.zeros_like(q))
            g_scr[i % 2, h] = _dot_nt(km_hi, qh) + _dot_nt(km_lo, qh)
            s_scr[i % 2, h, :n_keys, :] = _dot_nt(k_ref[0, :n_keys, :], qh)

    def attend_block(i):
        slot = i % 2
        n_keys = (i + 1) * bs
        outs = []
        for h in range(2):
            past = blk_id < i
            g = jnp.where(past, g_scr[slot, h], NEG_INF)
            allow = blk_id == i
            for _ in range(MOBA_TOPK):
                mx = jnp.max(g, axis=0, keepdims=True)
                first = jnp.min(jnp.where(g == mx, blk_id, nb), axis=0, keepdims=True)
                pick = blk_id == first
                allow = allow | (pick & past)
                g = jnp.where(pick, NEG_INF, g)
            blk_bias = jnp.where(allow, 0.0, NEG_INF)

            mx8 = None
            for jt in range(i + 1):
                rows = slice(jt * bs, (jt + 1) * bs)
                t = s_scr[slot, h, rows, :] + tab_ref[0, h, min(i - jt, ntab - 1)] + blk_bias[jt:jt + 1, :]
                s_scr[slot, h, rows, :] = t
                t8 = jnp.max(t.reshape(bs // 8, 8, bs), axis=0)
                mx8 = t8 if mx8 is None else jnp.maximum(mx8, t8)
            m = jnp.max(mx8, axis=0, keepdims=True)
            for jt in range(i + 1):
                rows = slice(jt * bs, (jt + 1) * bs)
                p_scr[h, rows, :] = jnp.exp(s_scr[slot, h, rows, :] - m).astype(BF16)
            outs.append(_dot(vt_ext[h, :, :n_keys], p_scr[h, :n_keys, :]))
        den0 = outs[0][HEAD_DIM:HEAD_DIM + 1, :]
        den1 = outs[1][0:1, :]
        o_t = jnp.where(row_o < HEAD_DIM, outs[0] / den0, outs[1] / den1)
        o_ref[0, i * bs:(i + 1) * bs, :] = o_t.T

    score_block(0)
    for i in range(nb):
        if i + 1 < nb:
            score_block(i + 1)
        attend_block(i)


def _moba(b1, tabs):
    B, _, S, _ = b1.shape
    npair = N_HEADS_B // 2

    def col(off):
        return pl.BlockSpec((1, None, S, LANES), lambda b, p, off=off: (b, off + p, 0, 0))

    return pl.pallas_call(
        functools.partial(_moba_kernel, seq=S),
        grid=(B, npair),
        in_specs=[col(0), col(npair), col(2 * npair),
                  pl.BlockSpec((1,) + tabs.shape[1:], lambda b, p: (p, 0, 0, 0, 0))],
        out_specs=pl.BlockSpec((1, None, S, LANES), lambda b, p: (b, p, 0, 0)),
        out_shape=jax.ShapeDtypeStruct((B, npair, S, LANES), F32),
        scratch_shapes=[pltpu.VMEM((2, LANES, S), BF16), pltpu.VMEM((2, 2, S, MOBA_BLOCK), F32),
                        pltpu.VMEM((2, S, MOBA_BLOCK), BF16),
                        pltpu.VMEM((2, 2, S // MOBA_BLOCK, MOBA_BLOCK), F32)],
        compiler_params=pltpu.CompilerParams(dimension_semantics=("arbitrary", "arbitrary"),
                                             vmem_limit_bytes=VMEM_LIMIT),
        name="moba",
    )(b1, b1, b1, tabs)


def _post_kernel(x_ref, oa_ref, ob_ref, kv_ref, ga_ref, gb_ref, wout_ref, gc_ref, wcq_ref, wco_ref,
                 gf2_ref, wgu_ref, wd_ref, gfin_ref, y_ref, *, d_ff, final_norm):
    d = x_ref.shape[2]
    oa = jnp.concatenate([oa_ref[0, c] for c in range(oa_ref.shape[1])], axis=1)
    ob = jnp.concatenate([ob_ref[0, c] for c in range(ob_ref.shape[1])], axis=1)
    wa = oa.shape[1]
    ya = _rms(oa, ga_ref[...]).astype(BF16)
    yb = _rms(ob, gb_ref[...]).astype(BF16)
    x = x_ref[0] + _dot(ya, wout_ref[:wa, :]) + _dot(yb, wout_ref[wa:, :])

    h = _rms(x, gc_ref[...]).astype(BF16)
    q = _dot(h, wcq_ref[...])
    dh = d // CROSS_HEADS
    scale = dh ** -0.5
    scores = [_dot_nt((q[:, c * dh:(c + 1) * dh] * scale).astype(BF16), kv_ref[0, :, c * dh:(c + 1) * dh])
              for c in range(CROSS_HEADS)]
    heads = []
    for c, s in enumerate(scores):
        e = jnp.exp(s - jnp.max(s, axis=-1, keepdims=True))
        p = e / jnp.sum(e, axis=-1, keepdims=True)
        heads.append(_dot(p.astype(BF16), kv_ref[0, :, d + c * dh:d + (c + 1) * dh]).astype(BF16))
    x = x + _dot(jnp.concatenate(heads, axis=1), wco_ref[...])

    h = _rms(x, gf2_ref[...]).astype(BF16)
    x = x + 0.5 * _swiglu_half(h, wgu_ref, wd_ref, d_ff)
    y_ref[0] = _rms(x, gfin_ref[...]) if final_norm else x


def _post(x1, oa, ob, kv, g_out_a, g_out_b, w_out, g_cross, w_cq, w_co, g_ffn2, wgu, wd, g_final, final_norm):
    B, S, D = x1.shape
    ta, tb = oa.shape[1], ob.shape[1]
    wa, wb = ta * LANES, tb * LANES
    M = kv.shape[1]
    d_ff = wd.shape[0]
    tm = TOKEN_TILE
    tok = lambda w: pl.BlockSpec((1, tm, w), lambda b, i: (b, i, 0))
    slab = lambda t: pl.BlockSpec((1, t, tm, LANES), lambda b, i: (b, 0, i, 0))
    return pl.pallas_call(
        functools.partial(_post_kernel, d_ff=d_ff, final_norm=final_norm),
        grid=(B, S // tm),
        in_specs=[tok(D), slab(ta), slab(tb),
                  pl.BlockSpec((1, M, 2 * D), lambda b, i: (b, 0, 0)),
                  _resident((1, wa)), _resident((1, wb)), _resident((wa + wb, D)),
                  _resident((1, D)), _resident((D, D)), _resident((D, D)),
                  _resident((1, D)), _resident((D, 2 * d_ff)), _resident((d_ff, D)),
                  _resident((1, D))],
        out_specs=tok(D),
        out_shape=jax.ShapeDtypeStruct((B, S, D), F32),
        compiler_params=pltpu.CompilerParams(dimension_semantics=("arbitrary", "arbitrary"),
                                             vmem_limit_bytes=VMEM_LIMIT),
        name="post",
    )(x1, oa, ob, kv, g_out_a.reshape(1, wa), g_out_b.reshape(1, wb), w_out, g_cross.reshape(1, D),
      w_cq, w_co, g_ffn2.reshape(1, D), wgu, wd, g_final.reshape(1, D))


def kernel(x, mem, g_ffn1, w_ffn1_gu, w_ffn1_down, g_mix, w_in, rel_bias, g_out_a, g_out_b, w_out,
           g_cross, g_mem, w_cq, w_ckv, w_co, g_ffn2, w_ffn2_gu, w_ffn2_down, g_final):
    B, S, D = x.shape
    depth = g_ffn1.shape[0]
    dil_tabs = _dilated_tables(rel_bias)
    moba_tabs = _moba_tables(rel_bias)
    bf = lambda w: w.astype(BF16)
    wqa, wka = N_HEADS_A * HEAD_DIM, 3 * N_HEADS_A * HEAD_DIM
    col = jnp.arange(w_in.shape[-1])
    is_q = (col < wqa) | ((col >= wka) & (col < wka + N_HEADS_B * HEAD_DIM))
    q_scale = jnp.where(is_q, HEAD_DIM ** -0.5, 1.0).astype(F32)
    y = None
    for l in range(depth):
        kv = _memkv(mem, g_mem[l], bf(w_ckv[l]))
        x1 = _ffn1(x.reshape(B * S, D), g_ffn1[l], bf(w_ffn1_gu[l]), bf(w_ffn1_down[l])).reshape(B, S, D)
        a1, a4, a16, b1 = _proj(x1, g_mix[l], bf(w_in[l] * q_scale))
        oa = _dilated(a1, a4, a16, dil_tabs)
        ob = _moba(b1, moba_tabs)
        y = _post(x1, oa, ob, kv, g_out_a[l], g_out_b[l], bf(w_out[l]), g_cross[l], bf(w_cq[l]),
                  bf(w_co[l]), g_ffn2[l], bf(w_ffn2_gu[l]), bf(w_ffn2_down[l]), g_final,
                  final_norm=(l == depth - 1))
        x = y
    return y
```

```python
import functools
import math

import numpy as np
import jax
import jax.numpy as jnp
from jax import lax
from jax.experimental import pallas as pl
from jax.experimental.pallas import tpu as pltpu

F32 = jnp.float32
BF16 = jnp.bfloat16

HEAD_DIM = 64
LANES = 128
BF16_SUBLANES = 16
N_HEADS_A = 12
N_HEADS_B = 4
DILATED_PATTERNS = ((128, 1), (512, 4), (2048, 16))
DIL_BLOCK = 128
MOBA_BLOCK = 256
MOBA_TOPK = 3
CROSS_HEADS = 4
REL_BUCKETS = 32
REL_MAX_DIST = 2048
NORM_EPS = 1e-6
NEG_INF = -jnp.inf
VMEM_LIMIT = 56 * 1024 * 1024

FFN_CHUNK = 256
TOKEN_TILE = 512


def _rms(x, g):
    xf = x.astype(F32)
    return xf * lax.rsqrt(jnp.mean(xf * xf, axis=-1, keepdims=True) + NORM_EPS) * g


def _dot(a, b):
    return jnp.dot(a, b, preferred_element_type=F32)


def _dot_nt(a, b):
    return lax.dot_general(a, b, (((1,), (1,)), ((), ())), preferred_element_type=F32)


def _resident(shape):
    nd = len(shape)
    return pl.BlockSpec(shape, lambda *_: (0,) * nd, pipeline_mode=pl.Buffered(1))


def _swiglu_half(h, wgu_ref, wd_ref, d_ff):
    acc = None
    for c in range(d_ff // FFN_CHUNK):
        lo = c * FFN_CHUNK
        g = _dot(h, wgu_ref[:, lo:lo + FFN_CHUNK])
        u = _dot(h, wgu_ref[:, d_ff + lo:d_ff + lo + FFN_CHUNK])
        a = (g * (1.0 / (1.0 + jnp.exp(-g))) * u).astype(BF16)
        part = _dot(a, wd_ref[lo:lo + FFN_CHUNK, :])
        acc = part if acc is None else acc + part
    return acc


def _memkv_kernel(mem_ref, g_ref, w_ref, kv_ref):
    h = _rms(mem_ref[0], g_ref[...]).astype(BF16)
    kv_ref[0] = _dot(h, w_ref[...]).astype(BF16)


def _memkv(mem, g_mem, w_ckv):
    B, M, D = mem.shape
    N = w_ckv.shape[1]
    return pl.pallas_call(
        _memkv_kernel,
        grid=(B,),
        in_specs=[pl.BlockSpec((1, M, D), lambda b: (b, 0, 0)),
                  _resident((1, D)), _resident((D, N))],
        out_specs=pl.BlockSpec((1, M, N), lambda b: (b, 0, 0)),
        out_shape=jax.ShapeDtypeStruct((B, M, N), BF16),
        compiler_params=pltpu.CompilerParams(dimension_semantics=("arbitrary",),
                                             vmem_limit_bytes=VMEM_LIMIT),
        name="memkv",
    )(mem, g_mem.reshape(1, D), w_ckv)


def _ffn1_kernel(x_ref, g_ref, wgu_ref, wd_ref, o_ref, *, d_ff):
    x = x_ref[...]
    h = _rms(x, g_ref[...]).astype(BF16)
    o_ref[...] = x + 0.5 * _swiglu_half(h, wgu_ref, wd_ref, d_ff)


def _ffn1(x2d, g, wgu, wd):
    T, D = x2d.shape
    d_ff = wd.shape[0]
    tm = TOKEN_TILE
    return pl.pallas_call(
        functools.partial(_ffn1_kernel, d_ff=d_ff),
        grid=(T // tm,),
        in_specs=[pl.BlockSpec((tm, D), lambda i: (i, 0)),
                  _resident((1, D)), _resident((D, 2 * d_ff)), _resident((d_ff, D))],
        out_specs=pl.BlockSpec((tm, D), lambda i: (i, 0)),
        out_shape=jax.ShapeDtypeStruct((T, D), F32),
        compiler_params=pltpu.CompilerParams(dimension_semantics=("arbitrary",),
                                             vmem_limit_bytes=VMEM_LIMIT),
        name="ffn1",
    )(x2d, g.reshape(1, D), wgu, wd)


def _proj_kernel(x_ref, g_ref, w_ref, a1_ref, a4_ref, a16_ref, b_ref, scr, scr4, *, wa):
    tm = x_ref.shape[1]
    h = _rms(x_ref[0], g_ref[...]).astype(BF16)
    pb = _dot(h, w_ref[:, wa:])
    for c in range(pb.shape[1] // LANES):
        b_ref[0, c] = pb[:, c * LANES:(c + 1) * LANES].astype(BF16)
    group = 2 * LANES
    for c in range(wa // LANES):
        if c % (group // LANES) == 0:
            pa = _dot(h, w_ref[:, c * LANES:c * LANES + group])
        sub = slice((c * LANES) % group, (c * LANES) % group + LANES)
        tile = pa[:, sub]
        a1_ref[0, c] = tile.astype(BF16)
        b = c % 2
        scr[b] = tile
        for r in range(4):
            plane = scr[b, pl.ds(r, tm // 4, stride=4), :]
            a4_ref[0, c, r] = plane.astype(BF16)
            scr4[b, r] = plane
        for r in range(16):
            a16_ref[0, c, r] = scr4[b, r % 4, pl.ds(r // 4, tm // 16, stride=4), :].astype(BF16)


def _proj(x1, g, w_in):
    B, S, D = x1.shape
    wa = 3 * N_HEADS_A * HEAD_DIM
    wb = 3 * N_HEADS_B * HEAD_DIM
    ta, tb = wa // LANES, wb // LANES
    tm = TOKEN_TILE
    kern = functools.partial(_proj_kernel, wa=wa)
    a1, a4, a16, b1 = pl.pallas_call(
        kern,
        grid=(B, S // tm),
        in_specs=[pl.BlockSpec((1, tm, D), lambda b, i: (b, i, 0)),
                  _resident((1, D)), _resident((D, wa + wb))],
        out_specs=[pl.BlockSpec((1, ta, tm, LANES), lambda b, i: (b, 0, i, 0)),
                   pl.BlockSpec((1, ta, 4, tm // 4, LANES), lambda b, i: (b, 0, 0, i, 0)),
                   pl.BlockSpec((1, ta, 16, tm // 16, LANES), lambda b, i: (b, 0, 0, i, 0)),
                   pl.BlockSpec((1, tb, tm, LANES), lambda b, i: (b, 0, i, 0))],
        out_shape=[jax.ShapeDtypeStruct((B, ta, S, LANES), BF16),
                   jax.ShapeDtypeStruct((B, ta, 4, S // 4, LANES), BF16),
                   jax.ShapeDtypeStruct((B, ta, 16, S // 16, LANES), BF16),
                   jax.ShapeDtypeStruct((B, tb, S, LANES), BF16)],
        scratch_shapes=[pltpu.VMEM((2, tm, LANES), F32), pltpu.VMEM((2, 4, tm // 4, LANES), F32)],
        compiler_params=pltpu.CompilerParams(dimension_semantics=("arbitrary", "arbitrary"),
                                             vmem_limit_bytes=VMEM_LIMIT),
        name="proj",
    )(x1, g.reshape(1, D), w_in)
    return a1, a4.reshape(a1.shape), a16.reshape(a1.shape), b1


def _rel_bucket(dist):
    max_exact = REL_BUCKETS // 2
    n = jnp.maximum(dist, 0)
    nf = jnp.maximum(n, 1).astype(F32)
    large = max_exact + (jnp.log(nf / max_exact) / math.log(REL_MAX_DIST / max_exact)
                         * (REL_BUCKETS - max_exact)).astype(jnp.int32)
    large = jnp.minimum(large, REL_BUCKETS - 1)
    return jnp.where(n < max_exact, n, large)


def _bias_lookup(dist, bias_cols, spec):
    onehot = (_rel_bucket(dist)[..., None] == jnp.arange(REL_BUCKETS)).astype(F32)
    return jnp.einsum(spec, onehot, bias_cols.astype(F32), precision=lax.Precision.HIGHEST)


def _dilated_tables(rel_bias):
    qb = DIL_BLOCK
    npat = len(DILATED_PATTERNS)
    dil = jnp.array([d for _, d in DILATED_PATTERNS], jnp.int32)[:, None, None, None]
    n_steps = jnp.array([w // d for w, d in DILATED_PATTERNS], jnp.int32)[:, None, None, None]
    variant = jnp.arange(3)[None, :, None, None]
    i = jnp.arange(qb)[None, None, :, None]
    j = jnp.arange(2 * qb)[None, None, None, :]
    delta = jnp.where(variant == 2, i - j, qb + i - j)
    ok = (delta >= 0) & (delta <= n_steps) & ((variant != 1) | (j >= qb))
    bias_cols = rel_bias[:, :N_HEADS_A].reshape(REL_BUCKETS, N_HEADS_A // 2, 2)
    t = _bias_lookup(delta * dil, bias_cols, "pvijb,bnk->npvkij")
    t = jnp.where(ok[None, :, :, None, :, :], t, NEG_INF)
    return t.reshape(N_HEADS_A // 2, npat, 3, 2 * qb, 2 * qb)


def _moba_num_tables():
    return -(-(REL_MAX_DIST + MOBA_BLOCK - 1) // MOBA_BLOCK) + 1


def _moba_tables(rel_bias):
    bs = MOBA_BLOCK
    off = jnp.arange(_moba_num_tables())[:, None, None]
    j = jnp.arange(bs)[None, :, None]
    i = jnp.arange(bs)[None, None, :]
    n = off * bs + i - j
    bias_cols = rel_bias[:, N_HEADS_A:].reshape(REL_BUCKETS, N_HEADS_B // 2, 2)
    t = _bias_lookup(n, bias_cols, "vijb,bnk->nkvij")
    return jnp.where((n >= 0)[None, None], t, NEG_INF)


def _dil_kernel(q1, k1, v1, q4, k4, v4, q16, k16, v16, tab_ref, o_ref, oscr, sc_scr, *, seq):
    qb = DIL_BLOCK
    nblk = seq // qb
    qrefs, krefs, vrefs = (q1, q4, q16), (k1, k4, k16), (v1, v4, v16)
    dils = tuple(d for _, d in DILATED_PATTERNS)
    ones_tile = jnp.ones((2 * qb, LANES), BF16)

    row2 = lax.broadcasted_iota(jnp.int32, (2 * qb, LANES), 0)
    lane2 = lax.broadcasted_iota(jnp.int32, (2 * qb, LANES), 1)
    own = (row2 < qb) == (lane2 < HEAD_DIM)
    lane1 = lax.broadcasted_iota(jnp.int32, (qb, LANES), 1)

    def band_start(blk):
        return pl.multiple_of(jnp.maximum(blk - 1, 0) * qb, qb)

    def score_block(blk, slot):
        q0 = pl.multiple_of(blk * qb, qb)
        for p in range(3):
            q = qrefs[p][0, pl.ds(q0, qb), :]
            q2 = jnp.concatenate([q, q], axis=0)
            qs = jnp.where(own, q2, jnp.zeros_like(q2))
            sc_scr[slot, p] = _dot_nt(qs, krefs[p][0, pl.ds(band_start(blk), 2 * qb), :])

    score_block(0, 0)

    def softmax_block(blk, slot):
        q0 = pl.multiple_of(blk * qb, qb)
        k0 = band_start(blk)
        for p in range(3):
            per = nblk // dils[p]
            variant = jnp.where(blk == 0, 2, jnp.where(blk % per == 0, 1, 0))
            s = sc_scr[slot, p] + tab_ref[0, p, variant]
            m = jnp.max(s, axis=-1, keepdims=True)
            pr = jnp.exp(s - m).astype(BF16)
            pv = _dot(pr, jnp.concatenate([vrefs[p][0, pl.ds(k0, 2 * qb), :], ones_tile], axis=1))
            h0 = lane1 < HEAD_DIM
            num = jnp.where(h0, pv[:qb, :LANES], pv[qb:, :LANES])
            den = jnp.where(h0, pv[:qb, LANES:], pv[qb:, LANES:])
            oscr[p, 0, pl.ds(q0, qb), :] = num / den
            oscr[p, 1, pl.ds(q0, qb), :] = jnp.where(h0, m[:qb], m[qb:]) + jnp.log(den)

    def block_pair(i, carry):
        b0 = 2 * i
        score_block(b0 + 1, 1)
        softmax_block(b0, 0)
        score_block(jnp.minimum(b0 + 2, nblk - 1), 0)
        softmax_block(b0 + 1, 1)
        return carry

    lax.fori_loop(0, nblk // 2, block_pair, 0)

    n16 = seq // 16
    n4 = seq // 4
    mc_rows = 128

    def merge(r, carry):
        for mc in range(n16 // mc_rows):
            m_lo = mc * mc_rows
            rows = (pl.ds(r + 16 * m_lo, mc_rows, stride=16),
                    pl.ds((r % 4) * n4 + r // 4 + 4 * m_lo, mc_rows, stride=4),
                    pl.ds(pl.multiple_of(r * n16 + m_lo, mc_rows), mc_rows))
            lse = [oscr[p, 1, rows[p], :] for p in range(3)]
            top = jnp.maximum(jnp.maximum(lse[0], lse[1]), lse[2])
            w = [jnp.exp(x - top) for x in lse]
            inv = 1.0 / (w[0] + w[1] + w[2])
            out = None
            for p in range(3):
                term = (w[p] * inv) * oscr[p, 0, rows[p], :]
                out = term if out is None else out + term
            o_ref[0, rows[0], :] = out
        return carry

    lax.fori_loop(0, 16, merge, 0)


def _dilated(a1, a4, a16, tabs):
    B, _, S, _ = a1.shape
    npair = N_HEADS_A // 2
    nq = npair

    def col(off):
        return pl.BlockSpec((1, None, S, LANES), lambda b, p, off=off: (b, off + p, 0, 0))

    in_specs = []
    for _ in range(3):
        in_specs += [col(0), col(nq), col(2 * nq)]
    in_specs.append(pl.BlockSpec((1,) + tabs.shape[1:], lambda b, p: (p, 0, 0, 0, 0)))
    return pl.pallas_call(
        functools.partial(_dil_kernel, seq=S),
        grid=(B, npair),
        in_specs=in_specs,
        out_specs=pl.BlockSpec((1, None, S, LANES), lambda b, p: (b, p, 0, 0)),
        out_shape=jax.ShapeDtypeStruct((B, npair, S, LANES), F32),
        scratch_shapes=[pltpu.VMEM((3, 2, S, LANES), F32),
                        pltpu.VMEM((2, 3, 2 * DIL_BLOCK, 2 * DIL_BLOCK), F32)],
        compiler_params=pltpu.CompilerParams(dimension_semantics=("arbitrary", "arbitrary"),
                                             vmem_limit_bytes=VMEM_LIMIT),
        name="dilated",
    )(a1, a1, a1, a4, a4, a4, a16, a16, a16, tabs)


def _moba_kernel(q_ref, k_ref, v_ref, tab_ref, o_ref, vt_ext, s_scr, p_scr, g_scr, *, seq):
    bs = MOBA_BLOCK
    nb = seq // bs
    ntab = tab_ref.shape[2]

    vt = v_ref[0].astype(F32).T
    ones_rows = jnp.ones((vt_ext.shape[1] - HEAD_DIM, seq), BF16)
    for h in range(2):
        vt_ext[h, :HEAD_DIM, :] = vt[h * HEAD_DIM:(h + 1) * HEAD_DIM, :].astype(BF16)
        vt_ext[h, HEAD_DIM:, :] = ones_rows

    kmean = jnp.sum(k_ref[0].astype(F32).reshape(nb, bs, LANES), axis=1) * (1.0 / bs)
    km_hi = kmean.astype(BF16)
    km_lo = (kmean - km_hi.astype(F32)).astype(BF16)

    lane_q = lax.broadcasted_iota(jnp.int32, (bs, LANES), 1)
    head_mask = (lane_q < HEAD_DIM, lane_q >= HEAD_DIM)
    blk_id = lax.broadcasted_iota(jnp.int32, (nb, bs), 0)

    def score_block(i):
        q = q_ref[0, i * bs:(i + 1) * bs, :]
        n_keys = (i + 1) * bs
        for h in range(2):
            qh = jnp.where(head_mask[h], q, jnp.zeros_like(q))
            g_scr[i % 2, h] = _dot_nt(km_hi, qh) + _dot_nt(km_lo, qh)
            s_scr[i % 2, h, :n_keys, :] = _dot_nt(k_ref[0, :n_keys, :], qh)

    def attend_block(i):
        slot = i % 2
        n_keys = (i + 1) * bs
        outs = []
        for h in range(2):
            past = blk_id < i
            g = jnp.where(past, g_scr[slot, h], NEG_INF)
            allow = blk_id == i
            for _ in range(MOBA_TOPK):
                mx = jnp.max(g, axis=0, keepdims=True)
                first = jnp.min(jnp.where(g == mx, blk_id, nb), axis=0, keepdims=True)
                pick = blk_id == first
                allow = allow | (pick & past)
                g = jnp.where(pick, NEG_INF, g)
            blk_bias = jnp.where(allow, 0.0, NEG_INF)

            mx8 = None
            for jt in range(i + 1):
                rows = slice(jt * bs, (jt + 1) * bs)
                t = s_scr[slot, h, rows, :] + tab_ref[0, h, min(i - jt, ntab - 1)] + blk_bias[jt:jt + 1, :]
                s_scr[slot, h, rows, :] = t
                t8 = jnp.max(t.reshape(bs // 8, 8, bs), axis=0)
                mx8 = t8 if mx8 is None else jnp.maximum(mx8, t8)
            m = jnp.max(mx8, axis=0, keepdims=True)
            for jt in range(i + 1):
                rows = slice(jt * bs, (jt + 1) * bs)
                p_scr[h, rows, :] = jnp.exp(s_scr[slot, h, rows, :] - m).astype(BF16)
            ov = _dot(vt_ext[h, :, :n_keys], p_scr[h, :n_keys, :])
            outs.append(ov[:HEAD_DIM, :] / ov[HEAD_DIM:HEAD_DIM + 1, :])
        o_t = jnp.concatenate(outs, axis=0)
        o_ref[0, i * bs:(i + 1) * bs, :] = o_t.T

    score_block(0)
    for i in range(nb):
        if i + 1 < nb:
            score_block(i + 1)
        attend_block(i)


def _moba(b1, tabs):
    B, _, S, _ = b1.shape
    npair = N_HEADS_B // 2

    def col(off):
        return pl.BlockSpec((1, None, S, LANES), lambda b, p, off=off: (b, off + p, 0, 0))

    return pl.pallas_call(
        functools.partial(_moba_kernel, seq=S),
        grid=(B, npair),
        in_specs=[col(0), col(npair), col(2 * npair),
                  pl.BlockSpec((1,) + tabs.shape[1:], lambda b, p: (p, 0, 0, 0, 0))],
        out_specs=pl.BlockSpec((1, None, S, LANES), lambda b, p: (b, p, 0, 0)),
        out_shape=jax.ShapeDtypeStruct((B, npair, S, LANES), F32),
        scratch_shapes=[pltpu.VMEM((2, HEAD_DIM + BF16_SUBLANES, S), BF16),
                        pltpu.VMEM((2, 2, S, MOBA_BLOCK), F32),
                        pltpu.VMEM((2, S, MOBA_BLOCK), BF16),
                        pltpu.VMEM((2, 2, S // MOBA_BLOCK, MOBA_BLOCK), F32)],
        compiler_params=pltpu.CompilerParams(dimension_semantics=("arbitrary", "arbitrary"),
                                             vmem_limit_bytes=VMEM_LIMIT),
        name="moba",
    )(b1, b1, b1, tabs)


def _post_kernel(x_ref, oa_ref, ob_ref, kv_ref, ga_ref, gb_ref, wout_ref, gc_ref, wcq_ref, wco_ref,
                 gf2_ref, wgu_ref, wd_ref, gfin_ref, y_ref, *, d_ff, final_norm):
    d = x_ref.shape[2]
    oa = jnp.concatenate([oa_ref[0, c] for c in range(oa_ref.shape[1])], axis=1)
    ob = jnp.concatenate([ob_ref[0, c] for c in range(ob_ref.shape[1])], axis=1)
    ya = _rms(oa, ga_ref[...]).astype(BF16)
    yb = _rms(ob, gb_ref[...]).astype(BF16)
    x = x_ref[0] + _dot(jnp.concatenate([ya, yb], axis=1), wout_ref[...])

    h = _rms(x, gc_ref[...]).astype(BF16)
    q = _dot(h, wcq_ref[...])
    dh = d // CROSS_HEADS
    scale = dh ** -0.5
    scores = [_dot_nt((q[:, c * dh:(c + 1) * dh] * scale).astype(BF16), kv_ref[0, :, c * dh:(c + 1) * dh])
              for c in range(CROSS_HEADS)]
    heads = []
    for c, s in enumerate(scores):
        e = jnp.exp(s - jnp.max(s, axis=-1, keepdims=True))
        p = e / jnp.sum(e, axis=-1, keepdims=True)
        heads.append(_dot(p.astype(BF16), kv_ref[0, :, d + c * dh:d + (c + 1) * dh]).astype(BF16))
    x = x + _dot(jnp.concatenate(heads, axis=1), wco_ref[...])

    h = _rms(x, gf2_ref[...]).astype(BF16)
    x = x + 0.5 * _swiglu_half(h, wgu_ref, wd_ref, d_ff)
    y_ref[0] = _rms(x, gfin_ref[...]) if final_norm else x


def _post(x1, oa, ob, kv, g_out_a, g_out_b, w_out, g_cross, w_cq, w_co, g_ffn2, wgu, wd, g_final, final_norm):
    B, S, D = x1.shape
    ta, tb = oa.shape[1], ob.shape[1]
    wa, wb = ta * LANES, tb * LANES
    M = kv.shape[1]
    d_ff = wd.shape[0]
    tm = TOKEN_TILE
    tok = lambda w: pl.BlockSpec((1, tm, w), lambda b, i: (b, i, 0))
    slab = lambda t: pl.BlockSpec((1, t, tm, LANES), lambda b, i: (b, 0, i, 0))
    return pl.pallas_call(
        functools.partial(_post_kernel, d_ff=d_ff, final_norm=final_norm),
        grid=(B, S // tm),
        in_specs=[tok(D), slab(ta), slab(tb),
                  pl.BlockSpec((1, M, 2 * D), lambda b, i: (b, 0, 0)),
                  _resident((1, wa)), _resident((1, wb)), _resident((wa + wb, D)),
                  _resident((1, D)), _resident((D, D)), _resident((D, D)),
                  _resident((1, D)), _resident((D, 2 * d_ff)), _resident((d_ff, D)),
                  _resident((1, D))],
        out_specs=tok(D),
        out_shape=jax.ShapeDtypeStruct((B, S, D), F32),
        compiler_params=pltpu.CompilerParams(dimension_semantics=("arbitrary", "arbitrary"),
                                             vmem_limit_bytes=VMEM_LIMIT),
        name="post",
    )(x1, oa, ob, kv, g_out_a.reshape(1, wa), g_out_b.reshape(1, wb), w_out, g_cross.reshape(1, D),
      w_cq, w_co, g_ffn2.reshape(1, D), wgu, wd, g_final.reshape(1, D))


def kernel(x, mem, g_ffn1, w_ffn1_gu, w_ffn1_down, g_mix, w_in, rel_bias, g_out_a, g_out_b, w_out,
           g_cross, g_mem, w_cq, w_ckv, w_co, g_ffn2, w_ffn2_gu, w_ffn2_down, g_final):
    B, S, D = x.shape
    depth = g_ffn1.shape[0]
    dil_tabs = _dilated_tables(rel_bias)
    moba_tabs = _moba_tables(rel_bias)
    bf = lambda w: w.astype(BF16)
    wqa, wka = N_HEADS_A * HEAD_DIM, 3 * N_HEADS_A * HEAD_DIM
    col = jnp.arange(w_in.shape[-1])
    is_q = (col < wqa) | ((col >= wka) & (col < wka + N_HEADS_B * HEAD_DIM))
    q_scale = jnp.where(is_q, HEAD_DIM ** -0.5, 1.0).astype(F32)
    y = None
    for l in range(depth):
        kv = _memkv(mem, g_mem[l], bf(w_ckv[l]))
        x1 = _ffn1(x.reshape(B * S, D), g_ffn1[l], bf(w_ffn1_gu[l]), bf(w_ffn1_down[l])).reshape(B, S, D)
        a1, a4, a16, b1 = _proj(x1, g_mix[l], bf(w_in[l] * q_scale))
        oa = _dilated(a1, a4, a16, dil_tabs)
        ob = _moba(b1, moba_tabs)
        y = _post(x1, oa, ob, kv, g_out_a[l], g_out_b[l], bf(w_out[l]), g_cross[l], bf(w_cq[l]),
                  bf(w_co[l]), g_ffn2[l], bf(w_ffn2_gu[l]), bf(w_ffn2_down[l]), g_final,
                  final_norm=(l == depth - 1))
        x = y
    return y
```

```python
import functools
import math

import numpy as np
import jax
import jax.numpy as jnp
from jax import lax
from jax.experimental import pallas as pl
from jax.experimental.pallas import tpu as pltpu

F32 = jnp.float32
BF16 = jnp.bfloat16

HEAD_DIM = 64
LANES = 128
BF16_SUBLANES = 16
N_HEADS_A = 12
N_HEADS_B = 4
DILATED_PATTERNS = ((128, 1), (512, 4), (2048, 16))
DIL_BLOCK = 128
MOBA_BLOCK = 256
MOBA_TOPK = 3
CROSS_HEADS = 4
REL_BUCKETS = 32
REL_MAX_DIST = 2048
NORM_EPS = 1e-6
NEG_INF = -jnp.inf
VMEM_LIMIT = 56 * 1024 * 1024

FFN_CHUNK = 256
TOKEN_TILE = 512


def _rms(x, g):
    xf = x.astype(F32)
    return xf * lax.rsqrt(jnp.mean(xf * xf, axis=-1, keepdims=True) + NORM_EPS) * g


def _dot(a, b):
    return jnp.dot(a, b, preferred_element_type=F32)


def _dot_nt(a, b):
    return lax.dot_general(a, b, (((1,), (1,)), ((), ())), preferred_element_type=F32)


def _resident(shape):
    nd = len(shape)
    return pl.BlockSpec(shape, lambda *_: (0,) * nd, pipeline_mode=pl.Buffered(1))


def _swiglu_half(h, wgu_ref, wd_ref, d_ff):
    acc = None
    for c in range(d_ff // FFN_CHUNK):
        lo = c * FFN_CHUNK
        g = _dot(h, wgu_ref[:, lo:lo + FFN_CHUNK])
        u = _dot(h, wgu_ref[:, d_ff + lo:d_ff + lo + FFN_CHUNK])
        a = (g * (1.0 / (1.0 + jnp.exp(-g))) * u).astype(BF16)
        part = _dot(a, wd_ref[lo:lo + FFN_CHUNK, :])
        acc = part if acc is None else acc + part
    return acc


def _memkv_kernel(mem_ref, g_ref, w_ref, kv_ref):
    h = _rms(mem_ref[0], g_ref[...]).astype(BF16)
    kv_ref[0] = _dot(h, w_ref[...]).astype(BF16)


def _memkv(mem, g_mem, w_ckv):
    B, M, D = mem.shape
    N = w_ckv.shape[1]
    return pl.pallas_call(
        _memkv_kernel,
        grid=(B,),
        in_specs=[pl.BlockSpec((1, M, D), lambda b: (b, 0, 0)),
                  _resident((1, D)), _resident((D, N))],
        out_specs=pl.BlockSpec((1, M, N), lambda b: (b, 0, 0)),
        out_shape=jax.ShapeDtypeStruct((B, M, N), BF16),
        compiler_params=pltpu.CompilerParams(dimension_semantics=("arbitrary",),
                                             vmem_limit_bytes=VMEM_LIMIT),
        name="memkv",
    )(mem, g_mem.reshape(1, D), w_ckv)


def _ffn1_kernel(x_ref, g_ref, wgu_ref, wd_ref, o_ref, *, d_ff):
    x = x_ref[...]
    h = _rms(x, g_ref[...]).astype(BF16)
    o_ref[...] = x + 0.5 * _swiglu_half(h, wgu_ref, wd_ref, d_ff)


def _ffn1(x2d, g, wgu, wd):
    T, D = x2d.shape
    d_ff = wd.shape[0]
    tm = TOKEN_TILE
    return pl.pallas_call(
        functools.partial(_ffn1_kernel, d_ff=d_ff),
        grid=(T // tm,),
        in_specs=[pl.BlockSpec((tm, D), lambda i: (i, 0)),
                  _resident((1, D)), _resident((D, 2 * d_ff)), _resident((d_ff, D))],
        out_specs=pl.BlockSpec((tm, D), lambda i: (i, 0)),
        out_shape=jax.ShapeDtypeStruct((T, D), F32),
        compiler_params=pltpu.CompilerParams(dimension_semantics=("arbitrary",),
                                             vmem_limit_bytes=VMEM_LIMIT),
        name="ffn1",
    )(x2d, g.reshape(1, D), wgu, wd)


def _proj_kernel(x_ref, g_ref, w_ref, a1_ref, a4_ref, a16_ref, b_ref, scr, scr4, *, wa):
    tm = x_ref.shape[1]
    h = _rms(x_ref[0], g_ref[...]).astype(BF16)
    pb = _dot(h, w_ref[:, wa:])
    for c in range(pb.shape[1] // LANES):
        b_ref[0, c] = pb[:, c * LANES:(c + 1) * LANES].astype(BF16)
    group = 2 * LANES
    for c in range(wa // LANES):
        if c % (group // LANES) == 0:
            pa = _dot(h, w_ref[:, c * LANES:c * LANES + group])
        sub = slice((c * LANES) % group, (c * LANES) % group + LANES)
        tile = pa[:, sub]
        a1_ref[0, c] = tile.astype(BF16)
        b = c % 2
        scr[b] = tile
        for r in range(4):
            plane = scr[b, pl.ds(r, tm // 4, stride=4), :]
            a4_ref[0, c, r] = plane.astype(BF16)
            scr4[b, r] = plane
        for r in range(16):
            a16_ref[0, c, r] = scr4[b, r % 4, pl.ds(r // 4, tm // 16, stride=4), :].astype(BF16)


def _proj(x1, g, w_in):
    B, S, D = x1.shape
    wa = 3 * N_HEADS_A * HEAD_DIM
    wb = 3 * N_HEADS_B * HEAD_DIM
    ta, tb = wa // LANES, wb // LANES
    tm = TOKEN_TILE
    kern = functools.partial(_proj_kernel, wa=wa)
    a1, a4, a16, b1 = pl.pallas_call(
        kern,
        grid=(B, S // tm),
        in_specs=[pl.BlockSpec((1, tm, D), lambda b, i: (b, i, 0)),
                  _resident((1, D)), _resident((D, wa + wb))],
        out_specs=[pl.BlockSpec((1, ta, tm, LANES), lambda b, i: (b, 0, i, 0)),
                   pl.BlockSpec((1, ta, 4, tm // 4, LANES), lambda b, i: (b, 0, 0, i, 0)),
                   pl.BlockSpec((1, ta, 16, tm // 16, LANES), lambda b, i: (b, 0, 0, i, 0)),
                   pl.BlockSpec((1, tb, tm, LANES), lambda b, i: (b, 0, i, 0))],
        out_shape=[jax.ShapeDtypeStruct((B, ta, S, LANES), BF16),
                   jax.ShapeDtypeStruct((B, ta, 4, S // 4, LANES), BF16),
                   jax.ShapeDtypeStruct((B, ta, 16, S // 16, LANES), BF16),
                   jax.ShapeDtypeStruct((B, tb, S, LANES), BF16)],
        scratch_shapes=[pltpu.VMEM((2, tm, LANES), F32), pltpu.VMEM((2, 4, tm // 4, LANES), F32)],
        compiler_params=pltpu.CompilerParams(dimension_semantics=("arbitrary", "arbitrary"),
                                             vmem_limit_bytes=VMEM_LIMIT),
        name="proj",
    )(x1, g.reshape(1, D), w_in)
    return a1, a4.reshape(a1.shape), a16.reshape(a1.shape), b1


def _rel_bucket(dist):
    max_exact = REL_BUCKETS // 2
    n = jnp.maximum(dist, 0)
    nf = jnp.maximum(n, 1).astype(F32)
    large = max_exact + (jnp.log(nf / max_exact) / math.log(REL_MAX_DIST / max_exact)
                         * (REL_BUCKETS - max_exact)).astype(jnp.int32)
    large = jnp.minimum(large, REL_BUCKETS - 1)
    return jnp.where(n < max_exact, n, large)


def _bias_lookup(bucket, bias_of_bucket):
    t = bias_of_bucket(REL_BUCKETS - 1)
    for b in range(REL_BUCKETS - 2, -1, -1):
        t = jnp.where(bucket == b, bias_of_bucket(b), t)
    return t


def _dilated_tables(rel_bias):
    qb = DIL_BLOCK
    npat = len(DILATED_PATTERNS)
    dil = jnp.array([d for _, d in DILATED_PATTERNS], jnp.int32)[:, None, None, None]
    n_steps = jnp.array([w // d for w, d in DILATED_PATTERNS], jnp.int32)[:, None, None, None]
    variant = jnp.arange(3)[None, :, None, None]
    i = jnp.arange(qb)[None, None, :, None]
    j = jnp.arange(2 * qb)[None, None, None, :]
    delta = jnp.where(variant == 2, i - j, qb + i - j)
    ok = (delta >= 0) & (delta <= n_steps) & ((variant != 1) | (j >= qb))
    bias_cols = rel_bias[:, :N_HEADS_A].astype(F32).reshape(REL_BUCKETS, N_HEADS_A // 2, 1, 1, 2, 1, 1)
    bucket = _rel_bucket(delta * dil)[None, :, :, None, :, :]
    t = _bias_lookup(bucket, lambda b: bias_cols[b])
    t = jnp.where(ok[None, :, :, None, :, :], t, NEG_INF)
    return t.reshape(N_HEADS_A // 2, npat, 3, 2 * qb, 2 * qb)


def _moba_num_tables():
    return -(-(REL_MAX_DIST + MOBA_BLOCK - 1) // MOBA_BLOCK) + 1


def _moba_tables(rel_bias):
    bs = MOBA_BLOCK
    off = jnp.arange(_moba_num_tables())[:, None, None]
    j = jnp.arange(bs)[None, :, None]
    i = jnp.arange(bs)[None, None, :]
    n = off * bs + i - j
    bias_cols = rel_bias[:, N_HEADS_A:].astype(F32).reshape(REL_BUCKETS, N_HEADS_B // 2, 2, 1, 1, 1)
    t = _bias_lookup(_rel_bucket(n)[None, None], lambda b: bias_cols[b])
    return jnp.where((n >= 0)[None, None], t, NEG_INF)


def _dil_kernel(q1, k1, v1, q4, k4, v4, q16, k16, v16, tab_ref, o_ref, oscr, sc_scr, *, seq):
    qb = DIL_BLOCK
    nblk = seq // qb
    qrefs, krefs, vrefs = (q1, q4, q16), (k1, k4, k16), (v1, v4, v16)
    dils = tuple(d for _, d in DILATED_PATTERNS)
    ones_tile = jnp.ones((2 * qb, LANES), BF16)

    row2 = lax.broadcasted_iota(jnp.int32, (2 * qb, LANES), 0)
    lane2 = lax.broadcasted_iota(jnp.int32, (2 * qb, LANES), 1)
    own = (row2 < qb) == (lane2 < HEAD_DIM)
    lane1 = lax.broadcasted_iota(jnp.int32, (qb, LANES), 1)

    def band_start(blk):
        return pl.multiple_of(jnp.maximum(blk - 1, 0) * qb, qb)

    def score_block(blk, slot):
        q0 = pl.multiple_of(blk * qb, qb)
        for p in range(3):
            q = qrefs[p][0, pl.ds(q0, qb), :]
            q2 = jnp.concatenate([q, q], axis=0)
            qs = jnp.where(own, q2, jnp.zeros_like(q2))
            sc_scr[slot, p] = _dot_nt(qs, krefs[p][0, pl.ds(band_start(blk), 2 * qb), :])

    score_block(0, 0)

    def softmax_block(blk, slot):
        q0 = pl.multiple_of(blk * qb, qb)
        k0 = band_start(blk)
        for p in range(3):
            per = nblk // dils[p]
            variant = jnp.where(blk == 0, 2, jnp.where(blk % per == 0, 1, 0))
            s = sc_scr[slot, p] + tab_ref[0, p, variant]
            m = jnp.max(s, axis=-1, keepdims=True)
            pr = jnp.exp(s - m).astype(BF16)
            pv = _dot(pr, jnp.concatenate([vrefs[p][0, pl.ds(k0, 2 * qb), :], ones_tile], axis=1))
            h0 = lane1 < HEAD_DIM
            num = jnp.where(h0, pv[:qb, :LANES], pv[qb:, :LANES])
            den = jnp.where(h0, pv[:qb, LANES:], pv[qb:, LANES:])
            oscr[p, 0, pl.ds(q0, qb), :] = num / den
            oscr[p, 1, pl.ds(q0, qb), :] = jnp.where(h0, m[:qb], m[qb:]) + jnp.log(den)

    def block_pair(i, carry):
        b0 = 2 * i
        score_block(b0 + 1, 1)
        softmax_block(b0, 0)
        score_block(jnp.minimum(b0 + 2, nblk - 1), 0)
        softmax_block(b0 + 1, 1)
        return carry

    lax.fori_loop(0, nblk // 2, block_pair, 0)

    n16 = seq // 16
    n4 = seq // 4
    mc_rows = 128

    def merge(r, carry):
        for mc in range(n16 // mc_rows):
            m_lo = mc * mc_rows
            rows = (pl.ds(r + 16 * m_lo, mc_rows, stride=16),
                    pl.ds((r % 4) * n4 + r // 4 + 4 * m_lo, mc_rows, stride=4),
                    pl.ds(pl.multiple_of(r * n16 + m_lo, mc_rows), mc_rows))
            lse = [oscr[p, 1, rows[p], :] for p in range(3)]
            top = jnp.maximum(jnp.maximum(lse[0], lse[1]), lse[2])
            w = [jnp.exp(x - top) for x in lse]
            inv = 1.0 / (w[0] + w[1] + w[2])
            out = None
            for p in range(3):
                term = (w[p] * inv) * oscr[p, 0, rows[p], :]
                out = term if out is None else out + term
            o_ref[0, rows[0], :] = out
        return carry

    lax.fori_loop(0, 16, merge, 0)


def _dilated(a1, a4, a16, tabs):
    B, _, S, _ = a1.shape
    npair = N_HEADS_A // 2
    nq = npair

    def col(off):
        return pl.BlockSpec((1, None, S, LANES), lambda b, p, off=off: (b, off + p, 0, 0))

    in_specs = []
    for _ in range(3):
        in_specs += [col(0), col(nq), col(2 * nq)]
    in_specs.append(pl.BlockSpec((1,) + tabs.shape[1:], lambda b, p: (p, 0, 0, 0, 0)))
    return pl.pallas_call(
        functools.partial(_dil_kernel, seq=S),
        grid=(B, npair),
        in_specs=in_specs,
        out_specs=pl.BlockSpec((1, None, S, LANES), lambda b, p: (b, p, 0, 0)),
        out_shape=jax.ShapeDtypeStruct((B, npair, S, LANES), F32),
        scratch_shapes=[pltpu.VMEM((3, 2, S, LANES), F32),
                        pltpu.VMEM((2, 3, 2 * DIL_BLOCK, 2 * DIL_BLOCK), F32)],
        compiler_params=pltpu.CompilerParams(dimension_semantics=("arbitrary", "arbitrary"),
                                             vmem_limit_bytes=VMEM_LIMIT),
        name="dilated",
    )(a1, a1, a1, a4, a4, a4, a16, a16, a16, tabs)


def _moba_kernel(q_ref, k_ref, v_ref, tab_ref, o_ref, vt_ext, s_scr, p_scr, g_scr, *, seq):
    bs = MOBA_BLOCK
    nb = seq // bs
    ntab = tab_ref.shape[2]

    vt = v_ref[0].astype(F32).T
    ones_rows = jnp.ones((vt_ext.shape[1] - HEAD_DIM, seq), BF16)
    for h in range(2):
        vt_ext[h, :HEAD_DIM, :] = vt[h * HEAD_DIM:(h + 1) * HEAD_DIM, :].astype(BF16)
        vt_ext[h, HEAD_DIM:, :] = ones_rows

    kmean = jnp.sum(k_ref[0].astype(F32).reshape(nb, bs, LANES), axis=1) * (1.0 / bs)
    km_hi = kmean.astype(BF16)
    km_lo = (kmean - km_hi.astype(F32)).astype(BF16)

    lane_q = lax.broadcasted_iota(jnp.int32, (bs, LANES), 1)
    head_mask = (lane_q < HEAD_DIM, lane_q >= HEAD_DIM)
    blk_id = lax.broadcasted_iota(jnp.int32, (nb, bs), 0)

    def score_block(i):
        q = q_ref[0, i * bs:(i + 1) * bs, :]
        n_keys = (i + 1) * bs
        for h in range(2):
            qh = jnp.where(head_mask[h], q, jnp.zeros_like(q))
            g_scr[i % 2, h] = _dot_nt(km_hi, qh) + _dot_nt(km_lo, qh)
            s_scr[i % 2, h, :n_keys, :] = _dot_nt(k_ref[0, :n_keys, :], qh)

    def attend_block(i):
        slot = i % 2
        n_keys = (i + 1) * bs
        outs = []
        for h in range(2):
            past = blk_id < i
            g = jnp.where(past, g_scr[slot, h], NEG_INF)
            allow = blk_id == i
            for _ in range(MOBA_TOPK):
                mx = jnp.max(g, axis=0, keepdims=True)
                first = jnp.min(jnp.where(g == mx, blk_id, nb), axis=0, keepdims=True)
                pick = blk_id == first
                allow = allow | (pick & past)
                g = jnp.where(pick, NEG_INF, g)
            blk_bias = jnp.where(allow, 0.0, NEG_INF)

            mx8 = None
            for jt in range(i + 1):
                rows = slice(jt * bs, (jt + 1) * bs)
                t = s_scr[slot, h, rows, :] + tab_ref[0, h, min(i - jt, ntab - 1)] + blk_bias[jt:jt + 1, :]
                s_scr[slot, h, rows, :] = t
                t8 = jnp.max(t.reshape(bs // 8, 8, bs), axis=0)
                mx8 = t8 if mx8 is None else jnp.maximum(mx8, t8)
            m = jnp.max(mx8, axis=0, keepdims=True)
            for jt in range(i + 1):
                rows = slice(jt * bs, (jt + 1) * bs)
                p_scr[h, rows, :] = jnp.exp(s_scr[slot, h, rows, :] - m).astype(BF16)
            ov = _dot(vt_ext[h, :, :n_keys], p_scr[h, :n_keys, :])
            outs.append(ov[:HEAD_DIM, :] / ov[HEAD_DIM:HEAD_DIM + 1, :])
        o_t = jnp.concatenate(outs, axis=0)
        o_ref[0, i * bs:(i + 1) * bs, :] = o_t.T

    score_block(0)
    for i in range(nb):
        if i + 1 < nb:
            score_block(i + 1)
        attend_block(i)


def _moba(b1, tabs):
    B, _, S, _ = b1.shape
    npair = N_HEADS_B // 2

    def col(off):
        return pl.BlockSpec((1, None, S, LANES), lambda b, p, off=off: (b, off + p, 0, 0))

    return pl.pallas_call(
        functools.partial(_moba_kernel, seq=S),
        grid=(B, npair),
        in_specs=[col(0), col(npair), col(2 * npair),
                  pl.BlockSpec((1,) + tabs.shape[1:], lambda b, p: (p, 0, 0, 0, 0))],
        out_specs=pl.BlockSpec((1, None, S, LANES), lambda b, p: (b, p, 0, 0)),
        out_shape=jax.ShapeDtypeStruct((B, npair, S, LANES), F32),
        scratch_shapes=[pltpu.VMEM((2, HEAD_DIM + BF16_SUBLANES, S), BF16),
                        pltpu.VMEM((2, 2, S, MOBA_BLOCK), F32),
                        pltpu.VMEM((2, S, MOBA_BLOCK), BF16),
                        pltpu.VMEM((2, 2, S // MOBA_BLOCK, MOBA_BLOCK), F32)],
        compiler_params=pltpu.CompilerParams(dimension_semantics=("arbitrary", "arbitrary"),
                                             vmem_limit_bytes=VMEM_LIMIT),
        name="moba",
    )(b1, b1, b1, tabs)


def _post_kernel(x_ref, oa_ref, ob_ref, kv_ref, ga_ref, gb_ref, wout_ref, gc_ref, wcq_ref, wco_ref,
                 gf2_ref, wgu_ref, wd_ref, gfin_ref, y_ref, *, d_ff, final_norm):
    d = x_ref.shape[2]
    oa = jnp.concatenate([oa_ref[0, c] for c in range(oa_ref.shape[1])], axis=1)
    ob = jnp.concatenate([ob_ref[0, c] for c in range(ob_ref.shape[1])], axis=1)
    ya = _rms(oa, ga_ref[...]).astype(BF16)
    yb = _rms(ob, gb_ref[...]).astype(BF16)
    x = x_ref[0] + _dot(jnp.concatenate([ya, yb], axis=1), wout_ref[...])

    h = _rms(x, gc_ref[...]).astype(BF16)
    q = _dot(h, wcq_ref[...])
    dh = d // CROSS_HEADS
    scale = dh ** -0.5
    scores = [_dot_nt((q[:, c * dh:(c + 1) * dh] * scale).astype(BF16), kv_ref[0, :, c * dh:(c + 1) * dh])
              for c in range(CROSS_HEADS)]
    heads = []
    for c, s in enumerate(scores):
        e = jnp.exp(s - jnp.max(s, axis=-1, keepdims=True))
        p = e / jnp.sum(e, axis=-1, keepdims=True)
        heads.append(_dot(p.astype(BF16), kv_ref[0, :, d + c * dh:d + (c + 1) * dh]).astype(BF16))
    x = x + _dot(jnp.concatenate(heads, axis=1), wco_ref[...])

    h = _rms(x, gf2_ref[...]).astype(BF16)
    x = x + 0.5 * _swiglu_half(h, wgu_ref, wd_ref, d_ff)
    y_ref[0] = _rms(x, gfin_ref[...]) if final_norm else x


def _post(x1, oa, ob, kv, g_out_a, g_out_b, w_out, g_cross, w_cq, w_co, g_ffn2, wgu, wd, g_final, final_norm):
    B, S, D = x1.shape
    ta, tb = oa.shape[1], ob.shape[1]
    wa, wb = ta * LANES, tb * LANES
    M = kv.shape[1]
    d_ff = wd.shape[0]
    tm = TOKEN_TILE
    tok = lambda w: pl.BlockSpec((1, tm, w), lambda b, i: (b, i, 0))
    slab = lambda t: pl.BlockSpec((1, t, tm, LANES), lambda b, i: (b, 0, i, 0))
    return pl.pallas_call(
        functools.partial(_post_kernel, d_ff=d_ff, final_norm=final_norm),
        grid=(B, S // tm),
        in_specs=[tok(D), slab(ta), slab(tb),
                  pl.BlockSpec((1, M, 2 * D), lambda b, i: (b, 0, 0)),
                  _resident((1, wa)), _resident((1, wb)), _resident((wa + wb, D)),
                  _resident((1, D)), _resident((D, D)), _resident((D, D)),
                  _resident((1, D)), _resident((D, 2 * d_ff)), _resident((d_ff, D)),
                  _resident((1, D))],
        out_specs=tok(D),
        out_shape=jax.ShapeDtypeStruct((B, S, D), F32),
        compiler_params=pltpu.CompilerParams(dimension_semantics=("arbitrary", "arbitrary"),
                                             vmem_limit_bytes=VMEM_LIMIT),
        name="post",
    )(x1, oa, ob, kv, g_out_a.reshape(1, wa), g_out_b.reshape(1, wb), w_out, g_cross.reshape(1, D),
      w_cq, w_co, g_ffn2.reshape(1, D), wgu, wd, g_final.reshape(1, D))


def kernel(x, mem, g_ffn1, w_ffn1_gu, w_ffn1_down, g_mix, w_in, rel_bias, g_out_a, g_out_b, w_out,
           g_cross, g_mem, w_cq, w_ckv, w_co, g_ffn2, w_ffn2_gu, w_ffn2_down, g_final):
    B, S, D = x.shape
    depth = g_ffn1.shape[0]
    dil_tabs = _dilated_tables(rel_bias)
    moba_tabs = _moba_tables(rel_bias)
    bf = lambda w: w.astype(BF16)
    wqa, wka = N_HEADS_A * HEAD_DIM, 3 * N_HEADS_A * HEAD_DIM
    col = jnp.arange(w_in.shape[-1])
    is_q = (col < wqa) | ((col >= wka) & (col < wka + N_HEADS_B * HEAD_DIM))
    q_scale = jnp.where(is_q, HEAD_DIM ** -0.5, 1.0).astype(F32)
    y = None
    for l in range(depth):
        kv = _memkv(mem, g_mem[l], bf(w_ckv[l]))
        x1 = _ffn1(x.reshape(B * S, D), g_ffn1[l], bf(w_ffn1_gu[l]), bf(w_ffn1_down[l])).reshape(B, S, D)
        a1, a4, a16, b1 = _proj(x1, g_mix[l], bf(w_in[l] * q_scale))
        oa = _dilated(a1, a4, a16, dil_tabs)
        ob = _moba(b1, moba_tabs)
        y = _post(x1, oa, ob, kv, g_out_a[l], g_out_b[l], bf(w_out[l]), g_cross[l], bf(w_cq[l]),
                  bf(w_co[l]), g_ffn2[l], bf(w_ffn2_gu[l]), bf(w_ffn2_down[l]), g_final,
                  final_norm=(l == depth - 1))
        x = y
    return y
```

```python
import functools
import math

import numpy as np
import jax
import jax.numpy as jnp
from jax import lax
from jax.experimental import pallas as pl
from jax.experimental.pallas import tpu as pltpu

F32 = jnp.float32
BF16 = jnp.bfloat16

HEAD_DIM = 64
LANES = 128
BF16_SUBLANES = 16
N_HEADS_A = 12
N_HEADS_B = 4
DILATED_PATTERNS = ((128, 1), (512, 4), (2048, 16))
DIL_BLOCK = 128
DIL_GROUP = 4
MOBA_BLOCK = 256
MOBA_TOPK = 3
CROSS_HEADS = 4
REL_BUCKETS = 32
REL_MAX_DIST = 2048
NORM_EPS = 1e-6
NEG_INF = -jnp.inf
VMEM_LIMIT = 56 * 1024 * 1024

FFN_CHUNK = 256
TOKEN_TILE = 512


def _rms(x, g):
    xf = x.astype(F32)
    return xf * lax.rsqrt(jnp.mean(xf * xf, axis=-1, keepdims=True) + NORM_EPS) * g


def _dot(a, b):
    return jnp.dot(a, b, preferred_element_type=F32)


def _dot_nt(a, b):
    return lax.dot_general(a, b, (((1,), (1,)), ((), ())), preferred_element_type=F32)


def _resident(shape):
    nd = len(shape)
    return pl.BlockSpec(shape, lambda *_: (0,) * nd, pipeline_mode=pl.Buffered(1))


def _swiglu_half(h, wgu_ref, wd_ref, d_ff):
    acc = None
    for c in range(d_ff // FFN_CHUNK):
        lo = c * FFN_CHUNK
        g = _dot(h, wgu_ref[:, lo:lo + FFN_CHUNK])
        u = _dot(h, wgu_ref[:, d_ff + lo:d_ff + lo + FFN_CHUNK])
        a = (g * (1.0 / (1.0 + jnp.exp(-g))) * u).astype(BF16)
        part = _dot(a, wd_ref[lo:lo + FFN_CHUNK, :])
        acc = part if acc is None else acc + part
    return acc


def _memkv_kernel(mem_ref, g_ref, w_ref, kv_ref):
    h = _rms(mem_ref[0], g_ref[...]).astype(BF16)
    kv_ref[0] = _dot(h, w_ref[...]).astype(BF16)


def _memkv(mem, g_mem, w_ckv):
    B, M, D = mem.shape
    N = w_ckv.shape[1]
    return pl.pallas_call(
        _memkv_kernel,
        grid=(B,),
        in_specs=[pl.BlockSpec((1, M, D), lambda b: (b, 0, 0)),
                  _resident((1, D)), _resident((D, N))],
        out_specs=pl.BlockSpec((1, M, N), lambda b: (b, 0, 0)),
        out_shape=jax.ShapeDtypeStruct((B, M, N), BF16),
        compiler_params=pltpu.CompilerParams(dimension_semantics=("arbitrary",),
                                             vmem_limit_bytes=VMEM_LIMIT),
        name="memkv",
    )(mem, g_mem.reshape(1, D), w_ckv)


def _ffn1_kernel(x_ref, g_ref, wgu_ref, wd_ref, o_ref, *, d_ff):
    x = x_ref[...]
    h = _rms(x, g_ref[...]).astype(BF16)
    o_ref[...] = x + 0.5 * _swiglu_half(h, wgu_ref, wd_ref, d_ff)


def _ffn1(x2d, g, wgu, wd):
    T, D = x2d.shape
    d_ff = wd.shape[0]
    tm = TOKEN_TILE
    return pl.pallas_call(
        functools.partial(_ffn1_kernel, d_ff=d_ff),
        grid=(T // tm,),
        in_specs=[pl.BlockSpec((tm, D), lambda i: (i, 0)),
                  _resident((1, D)), _resident((D, 2 * d_ff)), _resident((d_ff, D))],
        out_specs=pl.BlockSpec((tm, D), lambda i: (i, 0)),
        out_shape=jax.ShapeDtypeStruct((T, D), F32),
        compiler_params=pltpu.CompilerParams(dimension_semantics=("arbitrary",),
                                             vmem_limit_bytes=VMEM_LIMIT),
        name="ffn1",
    )(x2d, g.reshape(1, D), wgu, wd)


def _proj_kernel(x_ref, g_ref, w_ref, a1_ref, a4_ref, a16_ref, b_ref, scr, scr4, *, wa):
    tm = x_ref.shape[1]
    h = _rms(x_ref[0], g_ref[...]).astype(BF16)
    pb = _dot(h, w_ref[:, wa:])
    for c in range(pb.shape[1] // LANES):
        b_ref[0, c] = pb[:, c * LANES:(c + 1) * LANES].astype(BF16)
    group = 2 * LANES
    for c in range(wa // LANES):
        if c % (group // LANES) == 0:
            pa = _dot(h, w_ref[:, c * LANES:c * LANES + group])
        sub = slice((c * LANES) % group, (c * LANES) % group + LANES)
        tile = pa[:, sub]
        a1_ref[0, c] = tile.astype(BF16)
        b = c % 2
        scr[b] = tile
        for r in range(4):
            plane = scr[b, pl.ds(r, tm // 4, stride=4), :]
            a4_ref[0, c, r] = plane.astype(BF16)
            scr4[b, r] = plane
        for r in range(16):
            a16_ref[0, c, r] = scr4[b, r % 4, pl.ds(r // 4, tm // 16, stride=4), :].astype(BF16)


def _proj(x1, g, w_in):
    B, S, D = x1.shape
    wa = 3 * N_HEADS_A * HEAD_DIM
    wb = 3 * N_HEADS_B * HEAD_DIM
    ta, tb = wa // LANES, wb // LANES
    tm = TOKEN_TILE
    kern = functools.partial(_proj_kernel, wa=wa)
    a1, a4, a16, b1 = pl.pallas_call(
        kern,
        grid=(B, S // tm),
        in_specs=[pl.BlockSpec((1, tm, D), lambda b, i: (b, i, 0)),
                  _resident((1, D)), _resident((D, wa + wb))],
        out_specs=[pl.BlockSpec((1, ta, tm, LANES), lambda b, i: (b, 0, i, 0)),
                   pl.BlockSpec((1, ta, 4, tm // 4, LANES), lambda b, i: (b, 0, 0, i, 0)),
                   pl.BlockSpec((1, ta, 16, tm // 16, LANES), lambda b, i: (b, 0, 0, i, 0)),
                   pl.BlockSpec((1, tb, tm, LANES), lambda b, i: (b, 0, i, 0))],
        out_shape=[jax.ShapeDtypeStruct((B, ta, S, LANES), BF16),
                   jax.ShapeDtypeStruct((B, ta, 4, S // 4, LANES), BF16),
                   jax.ShapeDtypeStruct((B, ta, 16, S // 16, LANES), BF16),
                   jax.ShapeDtypeStruct((B, tb, S, LANES), BF16)],
        scratch_shapes=[pltpu.VMEM((2, tm, LANES), F32), pltpu.VMEM((2, 4, tm // 4, LANES), F32)],
        compiler_params=pltpu.CompilerParams(dimension_semantics=("arbitrary", "arbitrary"),
                                             vmem_limit_bytes=VMEM_LIMIT),
        name="proj",
    )(x1, g.reshape(1, D), w_in)
    return a1, a4.reshape(a1.shape), a16.reshape(a1.shape), b1


def _rel_bucket(dist):
    max_exact = REL_BUCKETS // 2
    n = jnp.maximum(dist, 0)
    nf = jnp.maximum(n, 1).astype(F32)
    large = max_exact + (jnp.log(nf / max_exact) / math.log(REL_MAX_DIST / max_exact)
                         * (REL_BUCKETS - max_exact)).astype(jnp.int32)
    large = jnp.minimum(large, REL_BUCKETS - 1)
    return jnp.where(n < max_exact, n, large)


def _bias_lookup(dist, bias_cols, spec):
    onehot = (_rel_bucket(dist)[..., None] == jnp.arange(REL_BUCKETS)).astype(F32)
    return jnp.einsum(spec, onehot, bias_cols.astype(F32), precision=lax.Precision.HIGHEST)


def _dilated_tables(rel_bias):
    qb = DIL_BLOCK
    npat = len(DILATED_PATTERNS)
    dil = jnp.array([d for _, d in DILATED_PATTERNS], jnp.int32)[:, None, None, None]
    n_steps = jnp.array([w // d for w, d in DILATED_PATTERNS], jnp.int32)[:, None, None, None]
    variant = jnp.arange(3)[None, :, None, None]
    i = jnp.arange(qb)[None, None, :, None]
    j = jnp.arange(2 * qb)[None, None, None, :]
    delta = jnp.where(variant == 2, i - j, qb + i - j)
    ok = (delta >= 0) & (delta <= n_steps) & ((variant != 1) | (j >= qb))
    bias_cols = rel_bias[:, :N_HEADS_A].reshape(REL_BUCKETS, N_HEADS_A // 2, 2)
    t = _bias_lookup(delta * dil, bias_cols, "pvijb,bnk->npvkij")
    t = jnp.where(ok[None, :, :, None, :, :], t, NEG_INF)
    return t.reshape(N_HEADS_A // 2, npat, 3, 2 * qb, 2 * qb)


def _moba_num_tables():
    return -(-(REL_MAX_DIST + MOBA_BLOCK - 1) // MOBA_BLOCK) + 1


def _moba_tables(rel_bias):
    bs = MOBA_BLOCK
    off = jnp.arange(_moba_num_tables())[:, None, None]
    j = jnp.arange(bs)[None, :, None]
    i = jnp.arange(bs)[None, None, :]
    n = off * bs + i - j
    bias_cols = rel_bias[:, N_HEADS_A:].reshape(REL_BUCKETS, N_HEADS_B // 2, 2)
    t = _bias_lookup(n, bias_cols, "vijb,bnk->nkvij")
    return jnp.where((n >= 0)[None, None], t, NEG_INF)


def _dil_kernel(q1, k1, v1, q4, k4, v4, q16, k16, v16, tab_ref, o_ref, oscr, sc_scr, *, seq):
    qb = DIL_BLOCK
    nblk = seq // qb
    qrefs, krefs, vrefs = (q1, q4, q16), (k1, k4, k16), (v1, v4, v16)
    dils = tuple(d for _, d in DILATED_PATTERNS)
    ones_tile = jnp.ones((2 * qb, LANES), BF16)

    row2 = lax.broadcasted_iota(jnp.int32, (2 * qb, LANES), 0)
    lane2 = lax.broadcasted_iota(jnp.int32, (2 * qb, LANES), 1)
    own = (row2 < qb) == (lane2 < HEAD_DIM)
    lane1 = lax.broadcasted_iota(jnp.int32, (qb, LANES), 1)

    def band_start(blk):
        return pl.multiple_of(jnp.maximum(blk - 1, 0) * qb, qb)

    def score_block(blk, slot):
        q0 = pl.multiple_of(blk * qb, qb)
        for p in range(3):
            q = qrefs[p][0, pl.ds(q0, qb), :]
            q2 = jnp.concatenate([q, q], axis=0)
            qs = jnp.where(own, q2, jnp.zeros_like(q2))
            sc_scr[slot, p] = _dot_nt(qs, krefs[p][0, pl.ds(band_start(blk), 2 * qb), :])

    score_block(0, 0)

    def softmax_block(blk, slot):
        q0 = pl.multiple_of(blk * qb, qb)
        k0 = band_start(blk)
        for p in range(3):
            per = nblk // dils[p]
            variant = jnp.where(blk == 0, 2, jnp.where(blk % per == 0, 1, 0))
            s = sc_scr[slot, p] + tab_ref[0, p, variant]
            m = jnp.max(s, axis=-1, keepdims=True)
            pr = jnp.exp(s - m).astype(BF16)
            pv = _dot(pr, jnp.concatenate([vrefs[p][0, pl.ds(k0, 2 * qb), :], ones_tile], axis=1))
            h0 = lane1 < HEAD_DIM
            num = jnp.where(h0, pv[:qb, :LANES], pv[qb:, :LANES])
            den = jnp.where(h0, pv[:qb, LANES:], pv[qb:, LANES:])
            oscr[p, 0, pl.ds(q0, qb), :] = num / den
            oscr[p, 1, pl.ds(q0, qb), :] = jnp.where(h0, m[:qb], m[qb:]) + jnp.log(den)

    def block_group(i, carry):
        b0 = DIL_GROUP * i
        for g in range(DIL_GROUP):
            score_block(jnp.minimum(b0 + g + 1, nblk - 1), (g + 1) % 2)
            softmax_block(b0 + g, g % 2)
        return carry

    lax.fori_loop(0, nblk // DIL_GROUP, block_group, 0)

    n16 = seq // 16
    n4 = seq // 4
    mc_rows = 128

    def merge(r, carry):
        for mc in range(n16 // mc_rows):
            m_lo = mc * mc_rows
            rows = (pl.ds(r + 16 * m_lo, mc_rows, stride=16),
                    pl.ds((r % 4) * n4 + r // 4 + 4 * m_lo, mc_rows, stride=4),
                    pl.ds(pl.multiple_of(r * n16 + m_lo, mc_rows), mc_rows))
            lse = [oscr[p, 1, rows[p], :] for p in range(3)]
            top = jnp.maximum(jnp.maximum(lse[0], lse[1]), lse[2])
            w = [jnp.exp(x - top) for x in lse]
            inv = 1.0 / (w[0] + w[1] + w[2])
            out = None
            for p in range(3):
                term = (w[p] * inv) * oscr[p, 0, rows[p], :]
                out = term if out is None else out + term
            o_ref[0, rows[0], :] = out
        return carry

    lax.fori_loop(0, 16, merge, 0, unroll=2)


def _dilated(a1, a4, a16, tabs):
    B, _, S, _ = a1.shape
    npair = N_HEADS_A // 2
    nq = npair

    def col(off):
        return pl.BlockSpec((1, None, S, LANES), lambda b, p, off=off: (b, off + p, 0, 0))

    in_specs = []
    for _ in range(3):
        in_specs += [col(0), col(nq), col(2 * nq)]
    in_specs.append(pl.BlockSpec((1,) + tabs.shape[1:], lambda b, p: (p, 0, 0, 0, 0)))
    return pl.pallas_call(
        functools.partial(_dil_kernel, seq=S),
        grid=(B, npair),
        in_specs=in_specs,
        out_specs=pl.BlockSpec((1, None, S, LANES), lambda b, p: (b, p, 0, 0)),
        out_shape=jax.ShapeDtypeStruct((B, npair, S, LANES), F32),
        scratch_shapes=[pltpu.VMEM((3, 2, S, LANES), F32),
                        pltpu.VMEM((2, 3, 2 * DIL_BLOCK, 2 * DIL_BLOCK), F32)],
        compiler_params=pltpu.CompilerParams(dimension_semantics=("arbitrary", "arbitrary"),
                                             vmem_limit_bytes=VMEM_LIMIT),
        name="dilated",
    )(a1, a1, a1, a4, a4, a4, a16, a16, a16, tabs)


def _moba_kernel(q_ref, k_ref, v_ref, tab_ref, o_ref, vt_ext, s_scr, p_scr, g_scr, *, seq):
    bs = MOBA_BLOCK
    nb = seq // bs
    ntab = tab_ref.shape[2]

    vt = v_ref[0].astype(F32).T
    ones_rows = jnp.ones((vt_ext.shape[1] - HEAD_DIM, seq), BF16)
    for h in range(2):
        vt_ext[h, :HEAD_DIM, :] = vt[h * HEAD_DIM:(h + 1) * HEAD_DIM, :].astype(BF16)
        vt_ext[h, HEAD_DIM:, :] = ones_rows

    kmean = jnp.sum(k_ref[0].astype(F32).reshape(nb, bs, LANES), axis=1) * (1.0 / bs)
    km_hi = kmean.astype(BF16)
    km_lo = (kmean - km_hi.astype(F32)).astype(BF16)

    lane_q = lax.broadcasted_iota(jnp.int32, (bs, LANES), 1)
    head_mask = (lane_q < HEAD_DIM, lane_q >= HEAD_DIM)
    blk_id = lax.broadcasted_iota(jnp.int32, (nb, bs), 0)

    def score_block(i):
        q = q_ref[0, i * bs:(i + 1) * bs, :]
        n_keys = (i + 1) * bs
        for h in range(2):
            qh = jnp.where(head_mask[h], q, jnp.zeros_like(q))
            g_scr[i % 2, h] = _dot_nt(km_hi, qh) + _dot_nt(km_lo, qh)
            s_scr[i % 2, h, :n_keys, :] = _dot_nt(k_ref[0, :n_keys, :], qh)

    def attend_block(i):
        slot = i % 2
        n_keys = (i + 1) * bs
        outs = []
        for h in range(2):
            past = blk_id < i
            g = jnp.where(past, g_scr[slot, h], NEG_INF)
            allow = blk_id == i
            for _ in range(MOBA_TOPK):
                mx = jnp.max(g, axis=0, keepdims=True)
                first = jnp.min(jnp.where(g == mx, blk_id, nb), axis=0, keepdims=True)
                pick = blk_id == first
                allow = allow | (pick & past)
                g = jnp.where(pick, NEG_INF, g)
            blk_bias = jnp.where(allow, 0.0, NEG_INF)

            mx8 = None
            for jt in range(i + 1):
                rows = slice(jt * bs, (jt + 1) * bs)
                t = s_scr[slot, h, rows, :] + tab_ref[0, h, min(i - jt, ntab - 1)] + blk_bias[jt:jt + 1, :]
                s_scr[slot, h, rows, :] = t
                t8 = jnp.max(t.reshape(bs // 8, 8, bs), axis=0)
                mx8 = t8 if mx8 is None else jnp.maximum(mx8, t8)
            m = jnp.max(mx8, axis=0, keepdims=True)
            for jt in range(i + 1):
                rows = slice(jt * bs, (jt + 1) * bs)
                p_scr[h, rows, :] = jnp.exp(s_scr[slot, h, rows, :] - m).astype(BF16)
            ov = _dot(vt_ext[h, :, :n_keys], p_scr[h, :n_keys, :])
            outs.append(ov[:HEAD_DIM, :] / ov[HEAD_DIM:HEAD_DIM + 1, :])
        o_t = jnp.concatenate(outs, axis=0)
        o_ref[0, i * bs:(i + 1) * bs, :] = o_t.T

    score_block(0)
    for i in range(nb):
        if i + 1 < nb:
            score_block(i + 1)
        attend_block(i)


def _moba(b1, tabs):
    B, _, S, _ = b1.shape
    npair = N_HEADS_B // 2

    def col(off):
        return pl.BlockSpec((1, None, S, LANES), lambda b, p, off=off: (b, off + p, 0, 0))

    return pl.pallas_call(
        functools.partial(_moba_kernel, seq=S),
        grid=(B, npair),
        in_specs=[col(0), col(npair), col(2 * npair),
                  pl.BlockSpec((1,) + tabs.shape[1:], lambda b, p: (p, 0, 0, 0, 0))],
        out_specs=pl.BlockSpec((1, None, S, LANES), lambda b, p: (b, p, 0, 0)),
        out_shape=jax.ShapeDtypeStruct((B, npair, S, LANES), F32),
        scratch_shapes=[pltpu.VMEM((2, HEAD_DIM + BF16_SUBLANES, S), BF16),
                        pltpu.VMEM((2, 2, S, MOBA_BLOCK), F32),
                        pltpu.VMEM((2, S, MOBA_BLOCK), BF16),
                        pltpu.VMEM((2, 2, S // MOBA_BLOCK, MOBA_BLOCK), F32)],
        compiler_params=pltpu.CompilerParams(dimension_semantics=("arbitrary", "arbitrary"),
                                             vmem_limit_bytes=VMEM_LIMIT),
        name="moba",
    )(b1, b1, b1, tabs)


def _post_kernel(x_ref, oa_ref, ob_ref, kv_ref, ga_ref, gb_ref, wout_ref, gc_ref, wcq_ref, wco_ref,
                 gf2_ref, wgu_ref, wd_ref, gfin_ref, y_ref, *, d_ff, final_norm):
    d = x_ref.shape[2]
    oa = jnp.concatenate([oa_ref[0, c] for c in range(oa_ref.shape[1])], axis=1)
    ob = jnp.concatenate([ob_ref[0, c] for c in range(ob_ref.shape[1])], axis=1)
    ya = _rms(oa, ga_ref[...]).astype(BF16)
    yb = _rms(ob, gb_ref[...]).astype(BF16)
    x = x_ref[0] + _dot(jnp.concatenate([ya, yb], axis=1), wout_ref[...])

    h = _rms(x, gc_ref[...]).astype(BF16)
    q = _dot(h, wcq_ref[...])
    dh = d // CROSS_HEADS
    scale = dh ** -0.5
    scores = [_dot_nt((q[:, c * dh:(c + 1) * dh] * scale).astype(BF16), kv_ref[0, :, c * dh:(c + 1) * dh])
              for c in range(CROSS_HEADS)]
    heads = []
    for c, s in enumerate(scores):
        e = jnp.exp(s - jnp.max(s, axis=-1, keepdims=True))
        p = e / jnp.sum(e, axis=-1, keepdims=True)
        heads.append(_dot(p.astype(BF16), kv_ref[0, :, d + c * dh:d + (c + 1) * dh]).astype(BF16))
    x = x + _dot(jnp.concatenate(heads, axis=1), wco_ref[...])

    h = _rms(x, gf2_ref[...]).astype(BF16)
    x = x + 0.5 * _swiglu_half(h, wgu_ref, wd_ref, d_ff)
    y_ref[0] = _rms(x, gfin_ref[...]) if final_norm else x


def _post(x1, oa, ob, kv, g_out_a, g_out_b, w_out, g_cross, w_cq, w_co, g_ffn2, wgu, wd, g_final, final_norm):
    B, S, D = x1.shape
    ta, tb = oa.shape[1], ob.shape[1]
    wa, wb = ta * LANES, tb * LANES
    M = kv.shape[1]
    d_ff = wd.shape[0]
    tm = TOKEN_TILE
    tok = lambda w: pl.BlockSpec((1, tm, w), lambda b, i: (b, i, 0))
    slab = lambda t: pl.BlockSpec((1, t, tm, LANES), lambda b, i: (b, 0, i, 0))
    return pl.pallas_call(
        functools.partial(_post_kernel, d_ff=d_ff, final_norm=final_norm),
        grid=(B, S // tm),
        in_specs=[tok(D), slab(ta), slab(tb),
                  pl.BlockSpec((1, M, 2 * D), lambda b, i: (b, 0, 0)),
                  _resident((1, wa)), _resident((1, wb)), _resident((wa + wb, D)),
                  _resident((1, D)), _resident((D, D)), _resident((D, D)),
                  _resident((1, D)), _resident((D, 2 * d_ff)), _resident((d_ff, D)),
                  _resident((1, D))],
        out_specs=tok(D),
        out_shape=jax.ShapeDtypeStruct((B, S, D), F32),
        compiler_params=pltpu.CompilerParams(dimension_semantics=("arbitrary", "arbitrary"),
                                             vmem_limit_bytes=VMEM_LIMIT),
        name="post",
    )(x1, oa, ob, kv, g_out_a.reshape(1, wa), g_out_b.reshape(1, wb), w_out, g_cross.reshape(1, D),
      w_cq, w_co, g_ffn2.reshape(1, D), wgu, wd, g_final.reshape(1, D))


def kernel(x, mem, g_ffn1, w_ffn1_gu, w_ffn1_down, g_mix, w_in, rel_bias, g_out_a, g_out_b, w_out,
           g_cross, g_mem, w_cq, w_ckv, w_co, g_ffn2, w_ffn2_gu, w_ffn2_down, g_final):
    B, S, D = x.shape
    depth = g_ffn1.shape[0]
    dil_tabs = _dilated_tables(rel_bias)
    moba_tabs = _moba_tables(rel_bias)
    bf = lambda w: w.astype(BF16)
    wqa, wka = N_HEADS_A * HEAD_DIM, 3 * N_HEADS_A * HEAD_DIM
    col = jnp.arange(w_in.shape[-1])
    is_q = (col < wqa) | ((col >= wka) & (col < wka + N_HEADS_B * HEAD_DIM))
    q_scale = jnp.where(is_q, HEAD_DIM ** -0.5, 1.0).astype(F32)
    y = None
    for l in range(depth):
        kv = _memkv(mem, g_mem[l], bf(w_ckv[l]))
        x1 = _ffn1(x.reshape(B * S, D), g_ffn1[l], bf(w_ffn1_gu[l]), bf(w_ffn1_down[l])).reshape(B, S, D)
        a1, a4, a16, b1 = _proj(x1, g_mix[l], bf(w_in[l] * q_scale))
        oa = _dilated(a1, a4, a16, dil_tabs)
        ob = _moba(b1, moba_tabs)
        y = _post(x1, oa, ob, kv, g_out_a[l], g_out_b[l], bf(w_out[l]), g_cross[l], bf(w_cq[l]),
                  bf(w_co[l]), g_ffn2[l], bf(w_ffn2_gu[l]), bf(w_ffn2_down[l]), g_final,
                  final_norm=(l == depth - 1))
        x = y
    return y
```

```python
import functools
import math

import numpy as np
import jax
import jax.numpy as jnp
from jax import lax
from jax.experimental import pallas as pl
from jax.experimental.pallas import tpu as pltpu

F32 = jnp.float32
BF16 = jnp.bfloat16

HEAD_DIM = 64
LANES = 128
BF16_SUBLANES = 16
N_HEADS_A = 12
N_HEADS_B = 4
DILATED_PATTERNS = ((128, 1), (512, 4), (2048, 16))
DIL_BLOCK = 128
DIL_GROUP = 8
MOBA_BLOCK = 256
MOBA_TOPK = 3
CROSS_HEADS = 4
REL_BUCKETS = 32
REL_MAX_DIST = 2048
NORM_EPS = 1e-6
NEG_INF = -jnp.inf
VMEM_LIMIT = 56 * 1024 * 1024

FFN_CHUNK = 256
TOKEN_TILE = 512
FFN1_TOKEN_TILE = 1024


def _rms(x, g):
    xf = x.astype(F32)
    return xf * lax.rsqrt(jnp.mean(xf * xf, axis=-1, keepdims=True) + NORM_EPS) * g


def _dot(a, b):
    return jnp.dot(a, b, preferred_element_type=F32)


def _dot_nt(a, b):
    return lax.dot_general(a, b, (((1,), (1,)), ((), ())), preferred_element_type=F32)


def _resident(shape):
    nd = len(shape)
    return pl.BlockSpec(shape, lambda *_: (0,) * nd, pipeline_mode=pl.Buffered(1))


def _swiglu_half(h, wgu_ref, wd_ref, d_ff):
    acc = None
    for c in range(d_ff // FFN_CHUNK):
        lo = c * FFN_CHUNK
        g = _dot(h, wgu_ref[:, lo:lo + FFN_CHUNK])
        u = _dot(h, wgu_ref[:, d_ff + lo:d_ff + lo + FFN_CHUNK])
        a = (g * (1.0 / (1.0 + jnp.exp(-g))) * u).astype(BF16)
        part = _dot(a, wd_ref[lo:lo + FFN_CHUNK, :])
        acc = part if acc is None else acc + part
    return acc


def _memkv_kernel(mem_ref, g_ref, w_ref, kv_ref):
    h = _rms(mem_ref[0], g_ref[...]).astype(BF16)
    kv_ref[0] = _dot(h, w_ref[...]).astype(BF16)


def _memkv(mem, g_mem, w_ckv):
    B, M, D = mem.shape
    N = w_ckv.shape[1]
    return pl.pallas_call(
        _memkv_kernel,
        grid=(B,),
        in_specs=[pl.BlockSpec((1, M, D), lambda b: (b, 0, 0)),
                  _resident((1, D)), _resident((D, N))],
        out_specs=pl.BlockSpec((1, M, N), lambda b: (b, 0, 0)),
        out_shape=jax.ShapeDtypeStruct((B, M, N), BF16),
        compiler_params=pltpu.CompilerParams(dimension_semantics=("arbitrary",),
                                             vmem_limit_bytes=VMEM_LIMIT),
        name="memkv",
    )(mem, g_mem.reshape(1, D), w_ckv)


def _ffn1_kernel(x_ref, g_ref, wgu_ref, wd_ref, o_ref, *, d_ff):
    x = x_ref[...]
    h = _rms(x, g_ref[...]).astype(BF16)
    o_ref[...] = x + 0.5 * _swiglu_half(h, wgu_ref, wd_ref, d_ff)


def _ffn1(x2d, g, wgu, wd):
    T, D = x2d.shape
    d_ff = wd.shape[0]
    tm = FFN1_TOKEN_TILE
    return pl.pallas_call(
        functools.partial(_ffn1_kernel, d_ff=d_ff),
        grid=(T // tm,),
        in_specs=[pl.BlockSpec((tm, D), lambda i: (i, 0)),
                  _resident((1, D)), _resident((D, 2 * d_ff)), _resident((d_ff, D))],
        out_specs=pl.BlockSpec((tm, D), lambda i: (i, 0)),
        out_shape=jax.ShapeDtypeStruct((T, D), F32),
        compiler_params=pltpu.CompilerParams(dimension_semantics=("arbitrary",),
                                             vmem_limit_bytes=VMEM_LIMIT),
        name="ffn1",
    )(x2d, g.reshape(1, D), wgu, wd)


def _proj_kernel(x_ref, g_ref, w_ref, a1_ref, a4_ref, a16_ref, b_ref, scr, scr4, *, wa):
    tm = x_ref.shape[1]
    h = _rms(x_ref[0], g_ref[...]).astype(BF16)
    pb = _dot(h, w_ref[:, wa:])
    for c in range(pb.shape[1] // LANES):
        b_ref[0, c] = pb[:, c * LANES:(c + 1) * LANES].astype(BF16)
    group = 2 * LANES
    for c in range(wa // LANES):
        if c % (group // LANES) == 0:
            pa = _dot(h, w_ref[:, c * LANES:c * LANES + group])
        sub = slice((c * LANES) % group, (c * LANES) % group + LANES)
        tile = pa[:, sub]
        a1_ref[0, c] = tile.astype(BF16)
        b = c % 2
        scr[b] = tile
        for r in range(4):
            plane = scr[b, pl.ds(r, tm // 4, stride=4), :]
            a4_ref[0, c, r] = plane.astype(BF16)
            scr4[b, r] = plane
        for r in range(16):
            a16_ref[0, c, r] = scr4[b, r % 4, pl.ds(r // 4, tm // 16, stride=4), :].astype(BF16)


def _proj(x1, g, w_in):
    B, S, D = x1.shape
    wa = 3 * N_HEADS_A * HEAD_DIM
    wb = 3 * N_HEADS_B * HEAD_DIM
    ta, tb = wa // LANES, wb // LANES
    tm = TOKEN_TILE
    kern = functools.partial(_proj_kernel, wa=wa)
    a1, a4, a16, b1 = pl.pallas_call(
        kern,
        grid=(B, S // tm),
        in_specs=[pl.BlockSpec((1, tm, D), lambda b, i: (b, i, 0)),
                  _resident((1, D)), _resident((D, wa + wb))],
        out_specs=[pl.BlockSpec((1, ta, tm, LANES), lambda b, i: (b, 0, i, 0)),
                   pl.BlockSpec((1, ta, 4, tm // 4, LANES), lambda b, i: (b, 0, 0, i, 0)),
                   pl.BlockSpec((1, ta, 16, tm // 16, LANES), lambda b, i: (b, 0, 0, i, 0)),
                   pl.BlockSpec((1, tb, tm, LANES), lambda b, i: (b, 0, i, 0))],
        out_shape=[jax.ShapeDtypeStruct((B, ta, S, LANES), BF16),
                   jax.ShapeDtypeStruct((B, ta, 4, S // 4, LANES), BF16),
                   jax.ShapeDtypeStruct((B, ta, 16, S // 16, LANES), BF16),
                   jax.ShapeDtypeStruct((B, tb, S, LANES), BF16)],
        scratch_shapes=[pltpu.VMEM((2, tm, LANES), F32), pltpu.VMEM((2, 4, tm // 4, LANES), F32)],
        compiler_params=pltpu.CompilerParams(dimension_semantics=("arbitrary", "arbitrary"),
                                             vmem_limit_bytes=VMEM_LIMIT),
        name="proj",
    )(x1, g.reshape(1, D), w_in)
    return a1, a4.reshape(a1.shape), a16.reshape(a1.shape), b1


def _rel_bucket(dist):
    max_exact = REL_BUCKETS // 2
    n = jnp.maximum(dist, 0)
    nf = jnp.maximum(n, 1).astype(F32)
    large = max_exact + (jnp.log(nf / max_exact) / math.log(REL_MAX_DIST / max_exact)
                         * (REL_BUCKETS - max_exact)).astype(jnp.int32)
    large = jnp.minimum(large, REL_BUCKETS - 1)
    return jnp.where(n < max_exact, n, large)


def _bias_lookup(dist, bias_cols, spec):
    onehot = (_rel_bucket(dist)[..., None] == jnp.arange(REL_BUCKETS)).astype(F32)
    return jnp.einsum(spec, onehot, bias_cols.astype(F32), precision=lax.Precision.HIGHEST)


def _dilated_tables(rel_bias):
    qb = DIL_BLOCK
    npat = len(DILATED_PATTERNS)
    dil = jnp.array([d for _, d in DILATED_PATTERNS], jnp.int32)[:, None, None, None]
    n_steps = jnp.array([w // d for w, d in DILATED_PATTERNS], jnp.int32)[:, None, None, None]
    variant = jnp.arange(3)[None, :, None, None]
    i = jnp.arange(qb)[None, None, :, None]
    j = jnp.arange(2 * qb)[None, None, None, :]
    delta = jnp.where(variant == 2, i - j, qb + i - j)
    ok = (delta >= 0) & (delta <= n_steps) & ((variant != 1) | (j >= qb))
    bias_cols = rel_bias[:, :N_HEADS_A].reshape(REL_BUCKETS, N_HEADS_A // 2, 2)
    t = _bias_lookup(delta * dil, bias_cols, "pvijb,bnk->npvkij")
    t = jnp.where(ok[None, :, :, None, :, :], t, NEG_INF)
    return t.reshape(N_HEADS_A // 2, npat, 3, 2 * qb, 2 * qb)


def _moba_num_tables():
    return -(-(REL_MAX_DIST + MOBA_BLOCK - 1) // MOBA_BLOCK) + 1


def _moba_tables(rel_bias):
    bs = MOBA_BLOCK
    off = jnp.arange(_moba_num_tables())[:, None, None]
    j = jnp.arange(bs)[None, :, None]
    i = jnp.arange(bs)[None, None, :]
    n = off * bs + i - j
    bias_cols = rel_bias[:, N_HEADS_A:].reshape(REL_BUCKETS, N_HEADS_B // 2, 2)
    t = _bias_lookup(n, bias_cols, "vijb,bnk->nkvij")
    return jnp.where((n >= 0)[None, None], t, NEG_INF)


def _dil_kernel(q1, k1, v1, q4, k4, v4, q16, k16, v16, tab_ref, o_ref, oscr, sc_scr, *, seq):
    qb = DIL_BLOCK
    nblk = seq // qb
    qrefs, krefs, vrefs = (q1, q4, q16), (k1, k4, k16), (v1, v4, v16)
    dils = tuple(d for _, d in DILATED_PATTERNS)
    ones_tile = jnp.ones((2 * qb, LANES), BF16)

    row2 = lax.broadcasted_iota(jnp.int32, (2 * qb, LANES), 0)
    lane2 = lax.broadcasted_iota(jnp.int32, (2 * qb, LANES), 1)
    own = (row2 < qb) == (lane2 < HEAD_DIM)
    lane1 = lax.broadcasted_iota(jnp.int32, (qb, LANES), 1)

    def band_start(blk):
        return pl.multiple_of(jnp.maximum(blk - 1, 0) * qb, qb)

    def score_block(blk, slot):
        q0 = pl.multiple_of(blk * qb, qb)
        for p in range(3):
            q = qrefs[p][0, pl.ds(q0, qb), :]
            q2 = jnp.concatenate([q, q], axis=0)
            qs = jnp.where(own, q2, jnp.zeros_like(q2))
            sc_scr[slot, p] = _dot_nt(qs, krefs[p][0, pl.ds(band_start(blk), 2 * qb), :])

    score_block(0, 0)

    def softmax_block(blk, slot):
        q0 = pl.multiple_of(blk * qb, qb)
        k0 = band_start(blk)
        for p in range(3):
            per = nblk // dils[p]
            variant = jnp.where(blk == 0, 2, jnp.where(blk % per == 0, 1, 0))
            s = sc_scr[slot, p] + tab_ref[0, p, variant]
            m = jnp.max(s, axis=-1, keepdims=True)
            pr = jnp.exp(s - m).astype(BF16)
            pv = _dot(pr, jnp.concatenate([vrefs[p][0, pl.ds(k0, 2 * qb), :], ones_tile], axis=1))
            h0 = lane1 < HEAD_DIM
            num = jnp.where(h0, pv[:qb, :LANES], pv[qb:, :LANES])
            den = jnp.where(h0, pv[:qb, LANES:], pv[qb:, LANES:])
            oscr[p, 0, pl.ds(q0, qb), :] = num / den
            oscr[p, 1, pl.ds(q0, qb), :] = jnp.where(h0, m[:qb], m[qb:]) + jnp.log(den)

    def block_group(i, carry):
        b0 = DIL_GROUP * i
        for g in range(DIL_GROUP):
            score_block(jnp.minimum(b0 + g + 1, nblk - 1), (g + 1) % 2)
            softmax_block(b0 + g, g % 2)
        return carry

    lax.fori_loop(0, nblk // DIL_GROUP, block_group, 0)

    n16 = seq // 16
    n4 = seq // 4
    mc_rows = 128

    def merge(r, carry):
        for mc in range(n16 // mc_rows):
            m_lo = mc * mc_rows
            rows = (pl.ds(r + 16 * m_lo, mc_rows, stride=16),
                    pl.ds((r % 4) * n4 + r // 4 + 4 * m_lo, mc_rows, stride=4),
                    pl.ds(pl.multiple_of(r * n16 + m_lo, mc_rows), mc_rows))
            lse = [oscr[p, 1, rows[p], :] for p in range(3)]
            top = jnp.maximum(jnp.maximum(lse[0], lse[1]), lse[2])
            w = [jnp.exp(x - top) for x in lse]
            inv = 1.0 / (w[0] + w[1] + w[2])
            out = None
            for p in range(3):
                term = (w[p] * inv) * oscr[p, 0, rows[p], :]
                out = term if out is None else out + term
            o_ref[0, rows[0], :] = out
        return carry

    lax.fori_loop(0, 16, merge, 0, unroll=4)


def _dilated(a1, a4, a16, tabs):
    B, _, S, _ = a1.shape
    npair = N_HEADS_A // 2
    nq = npair

    def col(off):
        return pl.BlockSpec((1, None, S, LANES), lambda b, p, off=off: (b, off + p, 0, 0))

    in_specs = []
    for _ in range(3):
        in_specs += [col(0), col(nq), col(2 * nq)]
    in_specs.append(pl.BlockSpec((1,) + tabs.shape[1:], lambda b, p: (p, 0, 0, 0, 0)))
    return pl.pallas_call(
        functools.partial(_dil_kernel, seq=S),
        grid=(B, npair),
        in_specs=in_specs,
        out_specs=pl.BlockSpec((1, None, S, LANES), lambda b, p: (b, p, 0, 0)),
        out_shape=jax.ShapeDtypeStruct((B, npair, S, LANES), F32),
        scratch_shapes=[pltpu.VMEM((3, 2, S, LANES), F32),
                        pltpu.VMEM((2, 3, 2 * DIL_BLOCK, 2 * DIL_BLOCK), F32)],
        compiler_params=pltpu.CompilerParams(dimension_semantics=("arbitrary", "arbitrary"),
                                             vmem_limit_bytes=VMEM_LIMIT),
        name="dilated",
    )(a1, a1, a1, a4, a4, a4, a16, a16, a16, tabs)


def _moba_kernel(q_ref, k_ref, v_ref, tab_ref, o_ref, vt_ext, s_scr, p_scr, g_scr, *, seq):
    bs = MOBA_BLOCK
    nb = seq // bs
    ntab = tab_ref.shape[2]

    vt = v_ref[0].astype(F32).T
    ones_rows = jnp.ones((vt_ext.shape[1] - HEAD_DIM, seq), BF16)
    for h in range(2):
        vt_ext[h, :HEAD_DIM, :] = vt[h * HEAD_DIM:(h + 1) * HEAD_DIM, :].astype(BF16)
        vt_ext[h, HEAD_DIM:, :] = ones_rows

    kmean = jnp.sum(k_ref[0].astype(F32).reshape(nb, bs, LANES), axis=1) * (1.0 / bs)
    km_hi = kmean.astype(BF16)
    km_lo = (kmean - km_hi.astype(F32)).astype(BF16)

    lane_q = lax.broadcasted_iota(jnp.int32, (bs, LANES), 1)
    head_mask = (lane_q < HEAD_DIM, lane_q >= HEAD_DIM)
    blk_id = lax.broadcasted_iota(jnp.int32, (nb, bs), 0)

    def score_block(i):
        q = q_ref[0, i * bs:(i + 1) * bs, :]
        n_keys = (i + 1) * bs
        for h in range(2):
            qh = jnp.where(head_mask[h], q, jnp.zeros_like(q))
            g_scr[i % 2, h] = _dot_nt(km_hi, qh) + _dot_nt(km_lo, qh)
            s_scr[i % 2, h, :n_keys, :] = _dot_nt(k_ref[0, :n_keys, :], qh)

    def attend_block(i):
        slot = i % 2
        n_keys = (i + 1) * bs
        outs = []
        for h in range(2):
            past = blk_id < i
            g = jnp.where(past, g_scr[slot, h], NEG_INF)
            allow = blk_id == i
            for _ in range(MOBA_TOPK):
                mx = jnp.max(g, axis=0, keepdims=True)
                first = jnp.min(jnp.where(g == mx, blk_id, nb), axis=0, keepdims=True)
                pick = blk_id == first
                allow = allow | (pick & past)
                g = jnp.where(pick, NEG_INF, g)
            blk_bias = jnp.where(allow, 0.0, NEG_INF)

            mx8 = None
            for jt in range(i + 1):
                rows = slice(jt * bs, (jt + 1) * bs)
                t = s_scr[slot, h, rows, :] + tab_ref[0, h, min(i - jt, ntab - 1)] + blk_bias[jt:jt + 1, :]
                s_scr[slot, h, rows, :] = t
                t8 = jnp.max(t.reshape(bs // 8, 8, bs), axis=0)
                mx8 = t8 if mx8 is None else jnp.maximum(mx8, t8)
            m = jnp.max(mx8, axis=0, keepdims=True)
            for jt in range(i + 1):
                rows = slice(jt * bs, (jt + 1) * bs)
                p_scr[h, rows, :] = jnp.exp(s_scr[slot, h, rows, :] - m).astype(BF16)
            ov = _dot(vt_ext[h, :, :n_keys], p_scr[h, :n_keys, :])
            outs.append(ov[:HEAD_DIM, :] / ov[HEAD_DIM:HEAD_DIM + 1, :])
        o_t = jnp.concatenate(outs, axis=0)
        o_ref[0, i * bs:(i + 1) * bs, :] = o_t.T

    score_block(0)
    for i in range(nb):
        if i + 1 < nb:
            score_block(i + 1)
        attend_block(i)


def _moba(b1, tabs):
    B, _, S, _ = b1.shape
    npair = N_HEADS_B // 2

    def col(off):
        return pl.BlockSpec((1, None, S, LANES), lambda b, p, off=off: (b, off + p, 0, 0))

    return pl.pallas_call(
        functools.partial(_moba_kernel, seq=S),
        grid=(B, npair),
        in_specs=[col(0), col(npair), col(2 * npair),
                  pl.BlockSpec((1,) + tabs.shape[1:], lambda b, p: (p, 0, 0, 0, 0))],
        out_specs=pl.BlockSpec((1, None, S, LANES), lambda b, p: (b, p, 0, 0)),
        out_shape=jax.ShapeDtypeStruct((B, npair, S, LANES), F32),
        scratch_shapes=[pltpu.VMEM((2, HEAD_DIM + BF16_SUBLANES, S), BF16),
                        pltpu.VMEM((2, 2, S, MOBA_BLOCK), F32),
                        pltpu.VMEM((2, S, MOBA_BLOCK), BF16),
                        pltpu.VMEM((2, 2, S // MOBA_BLOCK, MOBA_BLOCK), F32)],
        compiler_params=pltpu.CompilerParams(dimension_semantics=("arbitrary", "arbitrary"),
                                             vmem_limit_bytes=VMEM_LIMIT),
        name="moba",
    )(b1, b1, b1, tabs)


def _post_kernel(x_ref, oa_ref, ob_ref, kv_ref, ga_ref, gb_ref, wout_ref, gc_ref, wcq_ref, wco_ref,
                 gf2_ref, wgu_ref, wd_ref, gfin_ref, y_ref, *, d_ff, final_norm):
    d = x_ref.shape[2]
    oa = jnp.concatenate([oa_ref[0, c] for c in range(oa_ref.shape[1])], axis=1)
    ob = jnp.concatenate([ob_ref[0, c] for c in range(ob_ref.shape[1])], axis=1)
    ya = _rms(oa, ga_ref[...]).astype(BF16)
    yb = _rms(ob, gb_ref[...]).astype(BF16)
    x = x_ref[0] + _dot(jnp.concatenate([ya, yb], axis=1), wout_ref[...])

    h = _rms(x, gc_ref[...]).astype(BF16)
    q = _dot(h, wcq_ref[...])
    dh = d // CROSS_HEADS
    scale = dh ** -0.5
    scores = [_dot_nt((q[:, c * dh:(c + 1) * dh] * scale).astype(BF16), kv_ref[0, :, c * dh:(c + 1) * dh])
              for c in range(CROSS_HEADS)]
    heads = []
    for c, s in enumerate(scores):
        e = jnp.exp(s - jnp.max(s, axis=-1, keepdims=True))
        p = e / jnp.sum(e, axis=-1, keepdims=True)
        heads.append(_dot(p.astype(BF16), kv_ref[0, :, d + c * dh:d + (c + 1) * dh]).astype(BF16))
    x = x + _dot(jnp.concatenate(heads, axis=1), wco_ref[...])

    h = _rms(x, gf2_ref[...]).astype(BF16)
    x = x + 0.5 * _swiglu_half(h, wgu_ref, wd_ref, d_ff)
    y_ref[0] = _rms(x, gfin_ref[...]) if final_norm else x


def _post(x1, oa, ob, kv, g_out_a, g_out_b, w_out, g_cross, w_cq, w_co, g_ffn2, wgu, wd, g_final, final_norm):
    B, S, D = x1.shape
    ta, tb = oa.shape[1], ob.shape[1]
    wa, wb = ta * LANES, tb * LANES
    M = kv.shape[1]
    d_ff = wd.shape[0]
    tm = TOKEN_TILE
    tok = lambda w: pl.BlockSpec((1, tm, w), lambda b, i: (b, i, 0))
    slab = lambda t: pl.BlockSpec((1, t, tm, LANES), lambda b, i: (b, 0, i, 0))
    return pl.pallas_call(
        functools.partial(_post_kernel, d_ff=d_ff, final_norm=final_norm),
        grid=(B, S // tm),
        in_specs=[tok(D), slab(ta), slab(tb),
                  pl.BlockSpec((1, M, 2 * D), lambda b, i: (b, 0, 0)),
                  _resident((1, wa)), _resident((1, wb)), _resident((wa + wb, D)),
                  _resident((1, D)), _resident((D, D)), _resident((D, D)),
                  _resident((1, D)), _resident((D, 2 * d_ff)), _resident((d_ff, D)),
                  _resident((1, D))],
        out_specs=tok(D),
        out_shape=jax.ShapeDtypeStruct((B, S, D), F32),
        compiler_params=pltpu.CompilerParams(dimension_semantics=("arbitrary", "arbitrary"),
                                             vmem_limit_bytes=VMEM_LIMIT),
        name="post",
    )(x1, oa, ob, kv, g_out_a.reshape(1, wa), g_out_b.reshape(1, wb), w_out, g_cross.reshape(1, D),
      w_cq, w_co, g_ffn2.reshape(1, D), wgu, wd, g_final.reshape(1, D))


def kernel(x, mem, g_ffn1, w_ffn1_gu, w_ffn1_down, g_mix, w_in, rel_bias, g_out_a, g_out_b, w_out,
           g_cross, g_mem, w_cq, w_ckv, w_co, g_ffn2, w_ffn2_gu, w_ffn2_down, g_final):
    B, S, D = x.shape
    depth = g_ffn1.shape[0]
    dil_tabs = _dilated_tables(rel_bias)
    moba_tabs = _moba_tables(rel_bias)
    bf = lambda w: w.astype(BF16)
    wqa, wka = N_HEADS_A * HEAD_DIM, 3 * N_HEADS_A * HEAD_DIM
    col = jnp.arange(w_in.shape[-1])
    is_q = (col < wqa) | ((col >= wka) & (col < wka + N_HEADS_B * HEAD_DIM))
    q_scale = jnp.where(is_q, HEAD_DIM ** -0.5, 1.0).astype(F32)
    y = None
    for l in range(depth):
        kv = _memkv(mem, g_mem[l], bf(w_ckv[l]))
        x1 = _ffn1(x.reshape(B * S, D), g_ffn1[l], bf(w_ffn1_gu[l]), bf(w_ffn1_down[l])).reshape(B, S, D)
        a1, a4, a16, b1 = _proj(x1, g_mix[l], bf(w_in[l] * q_scale))
        oa = _dilated(a1, a4, a16, dil_tabs)
        ob = _moba(b1, moba_tabs)
        y = _post(x1, oa, ob, kv, g_out_a[l], g_out_b[l], bf(w_out[l]), g_cross[l], bf(w_cq[l]),
                  bf(w_co[l]), g_ffn2[l], bf(w_ffn2_gu[l]), bf(w_ffn2_down[l]), g_final,
                  final_norm=(l == depth - 1))
        x = y
    return y
```

```python
import functools
import math

import numpy as np
import jax
import jax.numpy as jnp
from jax import lax
from jax.experimental import pallas as pl
from jax.experimental.pallas import tpu as pltpu

F32 = jnp.float32
BF16 = jnp.bfloat16

HEAD_DIM = 64
LANES = 128
BF16_SUBLANES = 16
N_HEADS_A = 12
N_HEADS_B = 4
DILATED_PATTERNS = ((128, 1), (512, 4), (2048, 16))
DIL_BLOCK = 128
DIL_GROUP = 16
MOBA_BLOCK = 256
MOBA_TOPK = 3
CROSS_HEADS = 4
REL_BUCKETS = 32
REL_MAX_DIST = 2048
NORM_EPS = 1e-6
NEG_INF = -jnp.inf
VMEM_LIMIT = 56 * 1024 * 1024

FFN_CHUNK = 256
TOKEN_TILE = 512
FFN1_TOKEN_TILE = 1024
PROJ_TOKEN_TILE = 1024


def _rms(x, g):
    xf = x.astype(F32)
    return xf * lax.rsqrt(jnp.mean(xf * xf, axis=-1, keepdims=True) + NORM_EPS) * g


def _dot(a, b):
    return jnp.dot(a, b, preferred_element_type=F32)


def _dot_nt(a, b):
    return lax.dot_general(a, b, (((1,), (1,)), ((), ())), preferred_element_type=F32)


def _resident(shape):
    nd = len(shape)
    return pl.BlockSpec(shape, lambda *_: (0,) * nd, pipeline_mode=pl.Buffered(1))


def _swiglu_half(h, wgu_ref, wd_ref, d_ff):
    acc = None
    for c in range(d_ff // FFN_CHUNK):
        lo = c * FFN_CHUNK
        g = _dot(h, wgu_ref[:, lo:lo + FFN_CHUNK])
        u = _dot(h, wgu_ref[:, d_ff + lo:d_ff + lo + FFN_CHUNK])
        a = (g * (1.0 / (1.0 + jnp.exp(-g))) * u).astype(BF16)
        part = _dot(a, wd_ref[lo:lo + FFN_CHUNK, :])
        acc = part if acc is None else acc + part
    return acc


def _memkv_kernel(mem_ref, g_ref, w_ref, kv_ref):
    h = _rms(mem_ref[0], g_ref[...]).astype(BF16)
    kv_ref[0] = _dot(h, w_ref[...]).astype(BF16)


def _memkv(mem, g_mem, w_ckv):
    B, M, D = mem.shape
    N = w_ckv.shape[1]
    return pl.pallas_call(
        _memkv_kernel,
        grid=(B,),
        in_specs=[pl.BlockSpec((1, M, D), lambda b: (b, 0, 0)),
                  _resident((1, D)), _resident((D, N))],
        out_specs=pl.BlockSpec((1, M, N), lambda b: (b, 0, 0)),
        out_shape=jax.ShapeDtypeStruct((B, M, N), BF16),
        compiler_params=pltpu.CompilerParams(dimension_semantics=("arbitrary",),
                                             vmem_limit_bytes=VMEM_LIMIT),
        name="memkv",
    )(mem, g_mem.reshape(1, D), w_ckv)


def _ffn1_kernel(x_ref, g_ref, wgu_ref, wd_ref, o_ref, *, d_ff):
    x = x_ref[...]
    h = _rms(x, g_ref[...]).astype(BF16)
    o_ref[...] = x + 0.5 * _swiglu_half(h, wgu_ref, wd_ref, d_ff)


def _ffn1(x2d, g, wgu, wd):
    T, D = x2d.shape
    d_ff = wd.shape[0]
    tm = FFN1_TOKEN_TILE
    return pl.pallas_call(
        functools.partial(_ffn1_kernel, d_ff=d_ff),
        grid=(T // tm,),
        in_specs=[pl.BlockSpec((tm, D), lambda i: (i, 0)),
                  _resident((1, D)), _resident((D, 2 * d_ff)), _resident((d_ff, D))],
        out_specs=pl.BlockSpec((tm, D), lambda i: (i, 0)),
        out_shape=jax.ShapeDtypeStruct((T, D), F32),
        compiler_params=pltpu.CompilerParams(dimension_semantics=("arbitrary",),
                                             vmem_limit_bytes=VMEM_LIMIT),
        name="ffn1",
    )(x2d, g.reshape(1, D), wgu, wd)


def _proj_kernel(x_ref, g_ref, w_ref, a1_ref, a4_ref, a16_ref, b_ref, scr, scr4, *, wa):
    tm = x_ref.shape[1]
    h = _rms(x_ref[0], g_ref[...]).astype(BF16)
    pb = _dot(h, w_ref[:, wa:])
    for c in range(pb.shape[1] // LANES):
        b_ref[0, c] = pb[:, c * LANES:(c + 1) * LANES].astype(BF16)
    group = 2 * LANES
    for c in range(wa // LANES):
        if c % (group // LANES) == 0:
            pa = _dot(h, w_ref[:, c * LANES:c * LANES + group])
        sub = slice((c * LANES) % group, (c * LANES) % group + LANES)
        tile = pa[:, sub]
        a1_ref[0, c] = tile.astype(BF16)
        b = c % 2
        scr[b] = tile
        for r in range(4):
            plane = scr[b, pl.ds(r, tm // 4, stride=4), :]
            a4_ref[0, c, r] = plane.astype(BF16)
            scr4[b, r] = plane
        for r in range(16):
            a16_ref[0, c, r] = scr4[b, r % 4, pl.ds(r // 4, tm // 16, stride=4), :].astype(BF16)


def _proj(x1, g, w_in):
    B, S, D = x1.shape
    wa = 3 * N_HEADS_A * HEAD_DIM
    wb = 3 * N_HEADS_B * HEAD_DIM
    ta, tb = wa // LANES, wb // LANES
    tm = PROJ_TOKEN_TILE
    kern = functools.partial(_proj_kernel, wa=wa)
    a1, a4, a16, b1 = pl.pallas_call(
        kern,
        grid=(B, S // tm),
        in_specs=[pl.BlockSpec((1, tm, D), lambda b, i: (b, i, 0)),
                  _resident((1, D)), _resident((D, wa + wb))],
        out_specs=[pl.BlockSpec((1, ta, tm, LANES), lambda b, i: (b, 0, i, 0)),
                   pl.BlockSpec((1, ta, 4, tm // 4, LANES), lambda b, i: (b, 0, 0, i, 0)),
                   pl.BlockSpec((1, ta, 16, tm // 16, LANES), lambda b, i: (b, 0, 0, i, 0)),
                   pl.BlockSpec((1, tb, tm, LANES), lambda b, i: (b, 0, i, 0))],
        out_shape=[jax.ShapeDtypeStruct((B, ta, S, LANES), BF16),
                   jax.ShapeDtypeStruct((B, ta, 4, S // 4, LANES), BF16),
                   jax.ShapeDtypeStruct((B, ta, 16, S // 16, LANES), BF16),
                   jax.ShapeDtypeStruct((B, tb, S, LANES), BF16)],
        scratch_shapes=[pltpu.VMEM((2, tm, LANES), F32), pltpu.VMEM((2, 4, tm // 4, LANES), F32)],
        compiler_params=pltpu.CompilerParams(dimension_semantics=("arbitrary", "arbitrary"),
                                             vmem_limit_bytes=VMEM_LIMIT),
        name="proj",
    )(x1, g.reshape(1, D), w_in)
    return a1, a4.reshape(a1.shape), a16.reshape(a1.shape), b1


def _rel_bucket(dist):
    max_exact = REL_BUCKETS // 2
    n = jnp.maximum(dist, 0)
    nf = jnp.maximum(n, 1).astype(F32)
    large = max_exact + (jnp.log(nf / max_exact) / math.log(REL_MAX_DIST / max_exact)
                         * (REL_BUCKETS - max_exact)).astype(jnp.int32)
    large = jnp.minimum(large, REL_BUCKETS - 1)
    return jnp.where(n < max_exact, n, large)


def _bias_lookup(dist, bias_cols, spec):
    onehot = (_rel_bucket(dist)[..., None] == jnp.arange(REL_BUCKETS)).astype(F32)
    return jnp.einsum(spec, onehot, bias_cols.astype(F32), precision=lax.Precision.HIGHEST)


def _dilated_tables(rel_bias):
    qb = DIL_BLOCK
    npat = len(DILATED_PATTERNS)
    dil = jnp.array([d for _, d in DILATED_PATTERNS], jnp.int32)[:, None, None, None]
    n_steps = jnp.array([w // d for w, d in DILATED_PATTERNS], jnp.int32)[:, None, None, None]
    variant = jnp.arange(3)[None, :, None, None]
    i = jnp.arange(qb)[None, None, :, None]
    j = jnp.arange(2 * qb)[None, None, None, :]
    delta = jnp.where(variant == 2, i - j, qb + i - j)
    ok = (delta >= 0) & (delta <= n_steps) & ((variant != 1) | (j >= qb))
    bias_cols = rel_bias[:, :N_HEADS_A].reshape(REL_BUCKETS, N_HEADS_A // 2, 2)
    t = _bias_lookup(delta * dil, bias_cols, "pvijb,bnk->npvkij")
    t = jnp.where(ok[None, :, :, None, :, :], t, NEG_INF)
    return t.reshape(N_HEADS_A // 2, npat, 3, 2 * qb, 2 * qb)


def _moba_num_tables():
    return -(-(REL_MAX_DIST + MOBA_BLOCK - 1) // MOBA_BLOCK) + 1


def _moba_tables(rel_bias):
    bs = MOBA_BLOCK
    off = jnp.arange(_moba_num_tables())[:, None, None]
    j = jnp.arange(bs)[None, :, None]
    i = jnp.arange(bs)[None, None, :]
    n = off * bs + i - j
    bias_cols = rel_bias[:, N_HEADS_A:].reshape(REL_BUCKETS, N_HEADS_B // 2, 2)
    t = _bias_lookup(n, bias_cols, "vijb,bnk->nkvij")
    return jnp.where((n >= 0)[None, None], t, NEG_INF)


def _dil_kernel(q1, k1, v1, q4, k4, v4, q16, k16, v16, tab_ref, o_ref, oscr, sc_scr, *, seq):
    qb = DIL_BLOCK
    nblk = seq // qb
    qrefs, krefs, vrefs = (q1, q4, q16), (k1, k4, k16), (v1, v4, v16)
    dils = tuple(d for _, d in DILATED_PATTERNS)
    ones_tile = jnp.ones((2 * qb, LANES), BF16)

    row2 = lax.broadcasted_iota(jnp.int32, (2 * qb, LANES), 0)
    lane2 = lax.broadcasted_iota(jnp.int32, (2 * qb, LANES), 1)
    own = (row2 < qb) == (lane2 < HEAD_DIM)
    lane1 = lax.broadcasted_iota(jnp.int32, (qb, LANES), 1)

    def band_start(blk):
        return pl.multiple_of(jnp.maximum(blk - 1, 0) * qb, qb)

    def score_block(blk, slot):
        q0 = pl.multiple_of(blk * qb, qb)
        for p in range(3):
            q = qrefs[p][0, pl.ds(q0, qb), :]
            q2 = jnp.concatenate([q, q], axis=0)
            qs = jnp.where(own, q2, jnp.zeros_like(q2))
            sc_scr[slot, p] = _dot_nt(qs, krefs[p][0, pl.ds(band_start(blk), 2 * qb), :])

    score_block(0, 0)

    def softmax_block(blk, slot):
        q0 = pl.multiple_of(blk * qb, qb)
        k0 = band_start(blk)
        for p in range(3):
            per = nblk // dils[p]
            variant = jnp.where(blk == 0, 2, jnp.where(blk % per == 0, 1, 0))
            s = sc_scr[slot, p] + tab_ref[0, p, variant]
            m = jnp.max(s, axis=-1, keepdims=True)
            pr = jnp.exp(s - m).astype(BF16)
            pv = _dot(pr, jnp.concatenate([vrefs[p][0, pl.ds(k0, 2 * qb), :], ones_tile], axis=1))
            h0 = lane1 < HEAD_DIM
            num = jnp.where(h0, pv[:qb, :LANES], pv[qb:, :LANES])
            den = jnp.where(h0, pv[:qb, LANES:], pv[qb:, LANES:])
            oscr[p, 0, pl.ds(q0, qb), :] = num / den
            oscr[p, 1, pl.ds(q0, qb), :] = jnp.where(h0, m[:qb], m[qb:]) + jnp.log(den)

    def block_group(i, carry):
        b0 = DIL_GROUP * i
        for g in range(DIL_GROUP):
            score_block(jnp.minimum(b0 + g + 1, nblk - 1), (g + 1) % 2)
            softmax_block(b0 + g, g % 2)
        return carry

    lax.fori_loop(0, nblk // DIL_GROUP, block_group, 0)

    n16 = seq // 16
    n4 = seq // 4
    mc_rows = 128

    def merge(r, carry):
        for mc in range(n16 // mc_rows):
            m_lo = mc * mc_rows
            rows = (pl.ds(r + 16 * m_lo, mc_rows, stride=16),
                    pl.ds((r % 4) * n4 + r // 4 + 4 * m_lo, mc_rows, stride=4),
                    pl.ds(pl.multiple_of(r * n16 + m_lo, mc_rows), mc_rows))
            lse = [oscr[p, 1, rows[p], :] for p in range(3)]
            top = jnp.maximum(jnp.maximum(lse[0], lse[1]), lse[2])
            w = [jnp.exp(x - top) for x in lse]
            inv = 1.0 / (w[0] + w[1] + w[2])
            out = None
            for p in range(3):
                term = (w[p] * inv) * oscr[p, 0, rows[p], :]
                out = term if out is None else out + term
            o_ref[0, rows[0], :] = out
        return carry

    lax.fori_loop(0, 16, merge, 0, unroll=8)


def _dilated(a1, a4, a16, tabs):
    B, _, S, _ = a1.shape
    npair = N_HEADS_A // 2
    nq = npair

    def col(off):
        return pl.BlockSpec((1, None, S, LANES), lambda b, p, off=off: (b, off + p, 0, 0))

    in_specs = []
    for _ in range(3):
        in_specs += [col(0), col(nq), col(2 * nq)]
    in_specs.append(pl.BlockSpec((1,) + tabs.shape[1:], lambda b, p: (p, 0, 0, 0, 0)))
    return pl.pallas_call(
        functools.partial(_dil_kernel, seq=S),
        grid=(B, npair),
        in_specs=in_specs,
        out_specs=pl.BlockSpec((1, None, S, LANES), lambda b, p: (b, p, 0, 0)),
        out_shape=jax.ShapeDtypeStruct((B, npair, S, LANES), F32),
        scratch_shapes=[pltpu.VMEM((3, 2, S, LANES), F32),
                        pltpu.VMEM((2, 3, 2 * DIL_BLOCK, 2 * DIL_BLOCK), F32)],
        compiler_params=pltpu.CompilerParams(dimension_semantics=("arbitrary", "arbitrary"),
                                             vmem_limit_bytes=VMEM_LIMIT),
        name="dilated",
    )(a1, a1, a1, a4, a4, a4, a16, a16, a16, tabs)


def _moba_kernel(q_ref, k_ref, v_ref, tab_ref, o_ref, vt_ext, s_scr, p_scr, g_scr, *, seq):
    bs = MOBA_BLOCK
    nb = seq // bs
    ntab = tab_ref.shape[2]

    vt = v_ref[0].astype(F32).T
    ones_rows = jnp.ones((vt_ext.shape[1] - HEAD_DIM, seq), BF16)
    for h in range(2):
        vt_ext[h, :HEAD_DIM, :] = vt[h * HEAD_DIM:(h + 1) * HEAD_DIM, :].astype(BF16)
        vt_ext[h, HEAD_DIM:, :] = ones_rows

    kmean = jnp.sum(k_ref[0].astype(F32).reshape(nb, bs, LANES), axis=1) * (1.0 / bs)
    km_hi = kmean.astype(BF16)
    km_lo = (kmean - km_hi.astype(F32)).astype(BF16)

    lane_q = lax.broadcasted_iota(jnp.int32, (bs, LANES), 1)
    head_mask = (lane_q < HEAD_DIM, lane_q >= HEAD_DIM)
    blk_id = lax.broadcasted_iota(jnp.int32, (nb, bs), 0)

    def score_block(i):
        q = q_ref[0, i * bs:(i + 1) * bs, :]
        n_keys = (i + 1) * bs
        for h in range(2):
            qh = jnp.where(head_mask[h], q, jnp.zeros_like(q))
            g_scr[i % 2, h] = _dot_nt(km_hi, qh) + _dot_nt(km_lo, qh)
            s_scr[i % 2, h, :n_keys, :] = _dot_nt(k_ref[0, :n_keys, :], qh)

    def attend_block(i):
        slot = i % 2
        n_keys = (i + 1) * bs
        outs = []
        for h in range(2):
            past = blk_id < i
            g = jnp.where(past, g_scr[slot, h], NEG_INF)
            allow = blk_id == i
            for _ in range(MOBA_TOPK):
                mx = jnp.max(g, axis=0, keepdims=True)
                first = jnp.min(jnp.where(g == mx, blk_id, nb), axis=0, keepdims=True)
                pick = blk_id == first
                allow = allow | (pick & past)
                g = jnp.where(pick, NEG_INF, g)
            blk_bias = jnp.where(allow, 0.0, NEG_INF)

            mx8 = None
            for jt in range(i + 1):
                rows = slice(jt * bs, (jt + 1) * bs)
                t = s_scr[slot, h, rows, :] + tab_ref[0, h, min(i - jt, ntab - 1)] + blk_bias[jt:jt + 1, :]
                s_scr[slot, h, rows, :] = t
                t8 = jnp.max(t.reshape(bs // 8, 8, bs), axis=0)
                mx8 = t8 if mx8 is None else jnp.maximum(mx8, t8)
            m = jnp.max(mx8, axis=0, keepdims=True)
            for jt in range(i + 1):
                rows = slice(jt * bs, (jt + 1) * bs)
                p_scr[h, rows, :] = jnp.exp(s_scr[slot, h, rows, :] - m).astype(BF16)
            ov = _dot(vt_ext[h, :, :n_keys], p_scr[h, :n_keys, :])
            outs.append(ov[:HEAD_DIM, :] / ov[HEAD_DIM:HEAD_DIM + 1, :])
        o_t = jnp.concatenate(outs, axis=0)
        o_ref[0, i * bs:(i + 1) * bs, :] = o_t.T

    score_block(0)
    for i in range(nb):
        if i + 1 < nb:
            score_block(i + 1)
        attend_block(i)


def _moba(b1, tabs):
    B, _, S, _ = b1.shape
    npair = N_HEADS_B // 2

    def col(off):
        return pl.BlockSpec((1, None, S, LANES), lambda b, p, off=off: (b, off + p, 0, 0))

    return pl.pallas_call(
        functools.partial(_moba_kernel, seq=S),
        grid=(B, npair),
        in_specs=[col(0), col(npair), col(2 * npair),
                  pl.BlockSpec((1,) + tabs.shape[1:], lambda b, p: (p, 0, 0, 0, 0))],
        out_specs=pl.BlockSpec((1, None, S, LANES), lambda b, p: (b, p, 0, 0)),
        out_shape=jax.ShapeDtypeStruct((B, npair, S, LANES), F32),
        scratch_shapes=[pltpu.VMEM((2, HEAD_DIM + BF16_SUBLANES, S), BF16),
                        pltpu.VMEM((2, 2, S, MOBA_BLOCK), F32),
                        pltpu.VMEM((2, S, MOBA_BLOCK), BF16),
                        pltpu.VMEM((2, 2, S // MOBA_BLOCK, MOBA_BLOCK), F32)],
        compiler_params=pltpu.CompilerParams(dimension_semantics=("arbitrary", "arbitrary"),
                                             vmem_limit_bytes=VMEM_LIMIT),
        name="moba",
    )(b1, b1, b1, tabs)


def _post_kernel(x_ref, oa_ref, ob_ref, kv_ref, ga_ref, gb_ref, wout_ref, gc_ref, wcq_ref, wco_ref,
                 gf2_ref, wgu_ref, wd_ref, gfin_ref, y_ref, *, d_ff, final_norm):
    d = x_ref.shape[2]
    oa = jnp.concatenate([oa_ref[0, c] for c in range(oa_ref.shape[1])], axis=1)
    ob = jnp.concatenate([ob_ref[0, c] for c in range(ob_ref.shape[1])], axis=1)
    ya = _rms(oa, ga_ref[...]).astype(BF16)
    yb = _rms(ob, gb_ref[...]).astype(BF16)
    x = x_ref[0] + _dot(jnp.concatenate([ya, yb], axis=1), wout_ref[...])

    h = _rms(x, gc_ref[...]).astype(BF16)
    q = _dot(h, wcq_ref[...])
    dh = d // CROSS_HEADS
    scale = dh ** -0.5
    scores = [_dot_nt((q[:, c * dh:(c + 1) * dh] * scale).astype(BF16), kv_ref[0, :, c * dh:(c + 1) * dh])
              for c in range(CROSS_HEADS)]
    heads = []
    for c, s in enumerate(scores):
        e = jnp.exp(s - jnp.max(s, axis=-1, keepdims=True))
        p = e / jnp.sum(e, axis=-1, keepdims=True)
        heads.append(_dot(p.astype(BF16), kv_ref[0, :, d + c * dh:d + (c + 1) * dh]).astype(BF16))
    x = x + _dot(jnp.concatenate(heads, axis=1), wco_ref[...])

    h = _rms(x, gf2_ref[...]).astype(BF16)
    x = x + 0.5 * _swiglu_half(h, wgu_ref, wd_ref, d_ff)
    y_ref[0] = _rms(x, gfin_ref[...]) if final_norm else x


def _post(x1, oa, ob, kv, g_out_a, g_out_b, w_out, g_cross, w_cq, w_co, g_ffn2, wgu, wd, g_final, final_norm):
    B, S, D = x1.shape
    ta, tb = oa.shape[1], ob.shape[1]
    wa, wb = ta * LANES, tb * LANES
    M = kv.shape[1]
    d_ff = wd.shape[0]
    tm = TOKEN_TILE
    tok = lambda w: pl.BlockSpec((1, tm, w), lambda b, i: (b, i, 0))
    slab = lambda t: pl.BlockSpec((1, t, tm, LANES), lambda b, i: (b, 0, i, 0))
    return pl.pallas_call(
        functools.partial(_post_kernel, d_ff=d_ff, final_norm=final_norm),
        grid=(B, S // tm),
        in_specs=[tok(D), slab(ta), slab(tb),
                  pl.BlockSpec((1, M, 2 * D), lambda b, i: (b, 0, 0)),
                  _resident((1, wa)), _resident((1, wb)), _resident((wa + wb, D)),
                  _resident((1, D)), _resident((D, D)), _resident((D, D)),
                  _resident((1, D)), _resident((D, 2 * d_ff)), _resident((d_ff, D)),
                  _resident((1, D))],
        out_specs=tok(D),
        out_shape=jax.ShapeDtypeStruct((B, S, D), F32),
        compiler_params=pltpu.CompilerParams(dimension_semantics=("arbitrary", "arbitrary"),
                                             vmem_limit_bytes=VMEM_LIMIT),
        name="post",
    )(x1, oa, ob, kv, g_out_a.reshape(1, wa), g_out_b.reshape(1, wb), w_out, g_cross.reshape(1, D),
      w_cq, w_co, g_ffn2.reshape(1, D), wgu, wd, g_final.reshape(1, D))


def kernel(x, mem, g_ffn1, w_ffn1_gu, w_ffn1_down, g_mix, w_in, rel_bias, g_out_a, g_out_b, w_out,
           g_cross, g_mem, w_cq, w_ckv, w_co, g_ffn2, w_ffn2_gu, w_ffn2_down, g_final):
    B, S, D = x.shape
    depth = g_ffn1.shape[0]
    dil_tabs = _dilated_tables(rel_bias)
    moba_tabs = _moba_tables(rel_bias)
    bf = lambda w: w.astype(BF16)
    wqa, wka = N_HEADS_A * HEAD_DIM, 3 * N_HEADS_A * HEAD_DIM
    col = jnp.arange(w_in.shape[-1])
    is_q = (col < wqa) | ((col >= wka) & (col < wka + N_HEADS_B * HEAD_DIM))
    q_scale = jnp.where(is_q, HEAD_DIM ** -0.5, 1.0).astype(F32)
    y = None
    for l in range(depth):
        kv = _memkv(mem, g_mem[l], bf(w_ckv[l]))
        x1 = _ffn1(x.reshape(B * S, D), g_ffn1[l], bf(w_ffn1_gu[l]), bf(w_ffn1_down[l])).reshape(B, S, D)
        a1, a4, a16, b1 = _proj(x1, g_mix[l], bf(w_in[l] * q_scale))
        oa = _dilated(a1, a4, a16, dil_tabs)
        ob = _moba(b1, moba_tabs)
        y = _post(x1, oa, ob, kv, g_out_a[l], g_out_b[l], bf(w_out[l]), g_cross[l], bf(w_cq[l]),
                  bf(w_co[l]), g_ffn2[l], bf(w_ffn2_gu[l]), bf(w_ffn2_down[l]), g_final,
                  final_norm=(l == depth - 1))
        x = y
    return y
```

```python
import functools
import math

import numpy as np
import jax
import jax.numpy as jnp
from jax import lax
from jax.experimental import pallas as pl
from jax.experimental.pallas import tpu as pltpu

F32 = jnp.float32
BF16 = jnp.bfloat16

HEAD_DIM = 64
LANES = 128
BF16_SUBLANES = 16
N_HEADS_A = 12
N_HEADS_B = 4
DILATED_PATTERNS = ((128, 1), (512, 4), (2048, 16))
DIL_BLOCK = 128
DIL_GROUP = 32
MOBA_BLOCK = 256
MOBA_TOPK = 3
CROSS_HEADS = 4
REL_BUCKETS = 32
REL_MAX_DIST = 2048
NORM_EPS = 1e-6
NEG_INF = -jnp.inf
VMEM_LIMIT = 56 * 1024 * 1024

FFN_CHUNK = 256
TOKEN_TILE = 512
FFN1_TOKEN_TILE = 1024


def _rms(x, g):
    xf = x.astype(F32)
    return xf * lax.rsqrt(jnp.mean(xf * xf, axis=-1, keepdims=True) + NORM_EPS) * g


def _dot(a, b):
    return jnp.dot(a, b, preferred_element_type=F32)


def _dot_nt(a, b):
    return lax.dot_general(a, b, (((1,), (1,)), ((), ())), preferred_element_type=F32)


def _resident(shape):
    nd = len(shape)
    return pl.BlockSpec(shape, lambda *_: (0,) * nd, pipeline_mode=pl.Buffered(1))


def _swiglu_half(h, wgu_ref, wd_ref, d_ff):
    acc = None
    for c in range(d_ff // FFN_CHUNK):
        lo = c * FFN_CHUNK
        g = _dot(h, wgu_ref[:, lo:lo + FFN_CHUNK])
        u = _dot(h, wgu_ref[:, d_ff + lo:d_ff + lo + FFN_CHUNK])
        a = (g * (1.0 / (1.0 + jnp.exp(-g))) * u).astype(BF16)
        part = _dot(a, wd_ref[lo:lo + FFN_CHUNK, :])
        acc = part if acc is None else acc + part
    return acc


def _memkv_kernel(mem_ref, g_ref, w_ref, kv_ref):
    h = _rms(mem_ref[0], g_ref[...]).astype(BF16)
    kv_ref[0] = _dot(h, w_ref[...]).astype(BF16)


def _memkv(mem, g_mem, w_ckv):
    B, M, D = mem.shape
    N = w_ckv.shape[1]
    return pl.pallas_call(
        _memkv_kernel,
        grid=(B,),
        in_specs=[pl.BlockSpec((1, M, D), lambda b: (b, 0, 0)),
                  _resident((1, D)), _resident((D, N))],
        out_specs=pl.BlockSpec((1, M, N), lambda b: (b, 0, 0)),
        out_shape=jax.ShapeDtypeStruct((B, M, N), BF16),
        compiler_params=pltpu.CompilerParams(dimension_semantics=("arbitrary",),
                                             vmem_limit_bytes=VMEM_LIMIT),
        name="memkv",
    )(mem, g_mem.reshape(1, D), w_ckv)


def _ffn1_kernel(x_ref, g_ref, wgu_ref, wd_ref, o_ref, *, d_ff):
    x = x_ref[...]
    h = _rms(x, g_ref[...]).astype(BF16)
    o_ref[...] = x + 0.5 * _swiglu_half(h, wgu_ref, wd_ref, d_ff)


def _ffn1(x2d, g, wgu, wd):
    T, D = x2d.shape
    d_ff = wd.shape[0]
    tm = FFN1_TOKEN_TILE
    return pl.pallas_call(
        functools.partial(_ffn1_kernel, d_ff=d_ff),
        grid=(T // tm,),
        in_specs=[pl.BlockSpec((tm, D), lambda i: (i, 0)),
                  _resident((1, D)), _resident((D, 2 * d_ff)), _resident((d_ff, D))],
        out_specs=pl.BlockSpec((tm, D), lambda i: (i, 0)),
        out_shape=jax.ShapeDtypeStruct((T, D), F32),
        compiler_params=pltpu.CompilerParams(dimension_semantics=("arbitrary",),
                                             vmem_limit_bytes=VMEM_LIMIT),
        name="ffn1",
    )(x2d, g.reshape(1, D), wgu, wd)


def _proj_kernel(x_ref, g_ref, w_ref, a1_ref, a4_ref, a16_ref, b_ref, scr, scr4, *, wa):
    tm = x_ref.shape[1]
    h = _rms(x_ref[0], g_ref[...]).astype(BF16)
    pb = _dot(h, w_ref[:, wa:])
    for c in range(pb.shape[1] // LANES):
        b_ref[0, c] = pb[:, c * LANES:(c + 1) * LANES].astype(BF16)
    group = 2 * LANES
    for c in range(wa // LANES):
        if c % (group // LANES) == 0:
            pa = _dot(h, w_ref[:, c * LANES:c * LANES + group])
        sub = slice((c * LANES) % group, (c * LANES) % group + LANES)
        tile = pa[:, sub]
        a1_ref[0, c] = tile.astype(BF16)
        b = c % 2
        scr[b] = tile
        for r in range(4):
            plane = scr[b, pl.ds(r, tm // 4, stride=4), :]
            a4_ref[0, c, r] = plane.astype(BF16)
            scr4[b, r] = plane
        for r in range(16):
            a16_ref[0, c, r] = scr4[b, r % 4, pl.ds(r // 4, tm // 16, stride=4), :].astype(BF16)


def _proj(x1, g, w_in):
    B, S, D = x1.shape
    wa = 3 * N_HEADS_A * HEAD_DIM
    wb = 3 * N_HEADS_B * HEAD_DIM
    ta, tb = wa // LANES, wb // LANES
    tm = TOKEN_TILE
    kern = functools.partial(_proj_kernel, wa=wa)
    a1, a4, a16, b1 = pl.pallas_call(
        kern,
        grid=(B, S // tm),
        in_specs=[pl.BlockSpec((1, tm, D), lambda b, i: (b, i, 0)),
                  _resident((1, D)), _resident((D, wa + wb))],
        out_specs=[pl.BlockSpec((1, ta, tm, LANES), lambda b, i: (b, 0, i, 0)),
                   pl.BlockSpec((1, ta, 4, tm // 4, LANES), lambda b, i: (b, 0, 0, i, 0)),
                   pl.BlockSpec((1, ta, 16, tm // 16, LANES), lambda b, i: (b, 0, 0, i, 0)),
                   pl.BlockSpec((1, tb, tm, LANES), lambda b, i: (b, 0, i, 0))],
        out_shape=[jax.ShapeDtypeStruct((B, ta, S, LANES), BF16),
                   jax.ShapeDtypeStruct((B, ta, 4, S // 4, LANES), BF16),
                   jax.ShapeDtypeStruct((B, ta, 16, S // 16, LANES), BF16),
                   jax.ShapeDtypeStruct((B, tb, S, LANES), BF16)],
        scratch_shapes=[pltpu.VMEM((2, tm, LANES), F32), pltpu.VMEM((2, 4, tm // 4, LANES), F32)],
        compiler_params=pltpu.CompilerParams(dimension_semantics=("arbitrary", "arbitrary"),
                                             vmem_limit_bytes=VMEM_LIMIT),
        name="proj",
    )(x1, g.reshape(1, D), w_in)
    return a1, a4.reshape(a1.shape), a16.reshape(a1.shape), b1


def _rel_bucket(dist):
    max_exact = REL_BUCKETS // 2
    n = jnp.maximum(dist, 0)
    nf = jnp.maximum(n, 1).astype(F32)
    large = max_exact + (jnp.log(nf / max_exact) / math.log(REL_MAX_DIST / max_exact)
                         * (REL_BUCKETS - max_exact)).astype(jnp.int32)
    large = jnp.minimum(large, REL_BUCKETS - 1)
    return jnp.where(n < max_exact, n, large)


def _bias_lookup(dist, bias_cols, spec):
    onehot = (_rel_bucket(dist)[..., None] == jnp.arange(REL_BUCKETS)).astype(F32)
    return jnp.einsum(spec, onehot, bias_cols.astype(F32), precision=lax.Precision.HIGHEST)


def _dilated_tables(rel_bias):
    qb = DIL_BLOCK
    npat = len(DILATED_PATTERNS)
    dil = jnp.array([d for _, d in DILATED_PATTERNS], jnp.int32)[:, None, None, None]
    n_steps = jnp.array([w // d for w, d in DILATED_PATTERNS], jnp.int32)[:, None, None, None]
    variant = jnp.arange(3)[None, :, None, None]
    i = jnp.arange(qb)[None, None, :, None]
    j = jnp.arange(2 * qb)[None, None, None, :]
    delta = jnp.where(variant == 2, i - j, qb + i - j)
    ok = (delta >= 0) & (delta <= n_steps) & ((variant != 1) | (j >= qb))
    bias_cols = rel_bias[:, :N_HEADS_A].reshape(REL_BUCKETS, N_HEADS_A // 2, 2)
    t = _bias_lookup(delta * dil, bias_cols, "pvijb,bnk->npvkij")
    t = jnp.where(ok[None, :, :, None, :, :], t, NEG_INF)
    return t.reshape(N_HEADS_A // 2, npat, 3, 2 * qb, 2 * qb)


def _moba_num_tables():
    return -(-(REL_MAX_DIST + MOBA_BLOCK - 1) // MOBA_BLOCK) + 1


def _moba_tables(rel_bias):
    bs = MOBA_BLOCK
    off = jnp.arange(_moba_num_tables())[:, None, None]
    j = jnp.arange(bs)[None, :, None]
    i = jnp.arange(bs)[None, None, :]
    n = off * bs + i - j
    bias_cols = rel_bias[:, N_HEADS_A:].reshape(REL_BUCKETS, N_HEADS_B // 2, 2)
    t = _bias_lookup(n, bias_cols, "vijb,bnk->nkvij")
    return jnp.where((n >= 0)[None, None], t, NEG_INF)


def _dil_kernel(q1, k1, v1, q4, k4, v4, q16, k16, v16, tab_ref, o_ref, oscr, sc_scr, *, seq):
    qb = DIL_BLOCK
    nblk = seq // qb
    qrefs, krefs, vrefs = (q1, q4, q16), (k1, k4, k16), (v1, v4, v16)
    dils = tuple(d for _, d in DILATED_PATTERNS)
    ones_tile = jnp.ones((2 * qb, LANES), BF16)

    row2 = lax.broadcasted_iota(jnp.int32, (2 * qb, LANES), 0)
    lane2 = lax.broadcasted_iota(jnp.int32, (2 * qb, LANES), 1)
    own = (row2 < qb) == (lane2 < HEAD_DIM)
    lane1 = lax.broadcasted_iota(jnp.int32, (qb, LANES), 1)

    def band_start(blk):
        return pl.multiple_of(jnp.maximum(blk - 1, 0) * qb, qb)

    def score_block(blk, slot):
        q0 = pl.multiple_of(blk * qb, qb)
        for p in range(3):
            q = qrefs[p][0, pl.ds(q0, qb), :]
            q2 = jnp.concatenate([q, q], axis=0)
            qs = jnp.where(own, q2, jnp.zeros_like(q2))
            sc_scr[slot, p] = _dot_nt(qs, krefs[p][0, pl.ds(band_start(blk), 2 * qb), :])

    score_block(0, 0)

    def softmax_block(blk, slot):
        q0 = pl.multiple_of(blk * qb, qb)
        k0 = band_start(blk)
        for p in range(3):
            per = nblk // dils[p]
            variant = jnp.where(blk == 0, 2, jnp.where(blk % per == 0, 1, 0))
            s = sc_scr[slot, p] + tab_ref[0, p, variant]
            m = jnp.max(s, axis=-1, keepdims=True)
            pr = jnp.exp(s - m).astype(BF16)
            pv = _dot(pr, jnp.concatenate([vrefs[p][0, pl.ds(k0, 2 * qb), :], ones_tile], axis=1))
            h0 = lane1 < HEAD_DIM
            num = jnp.where(h0, pv[:qb, :LANES], pv[qb:, :LANES])
            den = jnp.where(h0, pv[:qb, LANES:], pv[qb:, LANES:])
            oscr[p, 0, pl.ds(q0, qb), :] = num / den
            oscr[p, 1, pl.ds(q0, qb), :] = jnp.where(h0, m[:qb], m[qb:]) + jnp.log(den)

    def block_group(i, carry):
        b0 = DIL_GROUP * i
        for g in range(DIL_GROUP):
            score_block(jnp.minimum(b0 + g + 1, nblk - 1), (g + 1) % 2)
            softmax_block(b0 + g, g % 2)
        return carry

    lax.fori_loop(0, nblk // DIL_GROUP, block_group, 0)

    n16 = seq // 16
    n4 = seq // 4
    mc_rows = 128

    def merge(r, carry):
        for mc in range(n16 // mc_rows):
            m_lo = mc * mc_rows
            rows = (pl.ds(r + 16 * m_lo, mc_rows, stride=16),
                    pl.ds((r % 4) * n4 + r // 4 + 4 * m_lo, mc_rows, stride=4),
                    pl.ds(pl.multiple_of(r * n16 + m_lo, mc_rows), mc_rows))
            lse = [oscr[p, 1, rows[p], :] for p in range(3)]
            top = jnp.maximum(jnp.maximum(lse[0], lse[1]), lse[2])
            w = [jnp.exp(x - top) for x in lse]
            inv = 1.0 / (w[0] + w[1] + w[2])
            out = None
            for p in range(3):
                term = (w[p] * inv) * oscr[p, 0, rows[p], :]
                out = term if out is None else out + term
            o_ref[0, rows[0], :] = out
        return carry

    lax.fori_loop(0, 16, merge, 0, unroll=16)


def _dilated(a1, a4, a16, tabs):
    B, _, S, _ = a1.shape
    assert (S // DIL_BLOCK) % DIL_GROUP == 0 and S % (16 * DIL_BLOCK) == 0
    npair = N_HEADS_A // 2
    nq = npair

    def col(off):
        return pl.BlockSpec((1, None, S, LANES), lambda b, p, off=off: (b, off + p, 0, 0))

    in_specs = []
    for _ in range(3):
        in_specs += [col(0), col(nq), col(2 * nq)]
    in_specs.append(pl.BlockSpec((1,) + tabs.shape[1:], lambda b, p: (p, 0, 0, 0, 0)))
    return pl.pallas_call(
        functools.partial(_dil_kernel, seq=S),
        grid=(B, npair),
        in_specs=in_specs,
        out_specs=pl.BlockSpec((1, None, S, LANES), lambda b, p: (b, p, 0, 0)),
        out_shape=jax.ShapeDtypeStruct((B, npair, S, LANES), F32),
        scratch_shapes=[pltpu.VMEM((3, 2, S, LANES), F32),
                        pltpu.VMEM((2, 3, 2 * DIL_BLOCK, 2 * DIL_BLOCK), F32)],
        compiler_params=pltpu.CompilerParams(dimension_semantics=("arbitrary", "arbitrary"),
                                             vmem_limit_bytes=VMEM_LIMIT),
        name="dilated",
    )(a1, a1, a1, a4, a4, a4, a16, a16, a16, tabs)


def _moba_kernel(q_ref, k_ref, v_ref, tab_ref, o_ref, vt_ext, s_scr, p_scr, g_scr, *, seq):
    bs = MOBA_BLOCK
    nb = seq // bs
    ntab = tab_ref.shape[2]

    vt = v_ref[0].astype(F32).T
    ones_rows = jnp.ones((vt_ext.shape[1] - HEAD_DIM, seq), BF16)
    for h in range(2):
        vt_ext[h, :HEAD_DIM, :] = vt[h * HEAD_DIM:(h + 1) * HEAD_DIM, :].astype(BF16)
        vt_ext[h, HEAD_DIM:, :] = ones_rows

    kmean = jnp.sum(k_ref[0].astype(F32).reshape(nb, bs, LANES), axis=1) * (1.0 / bs)
    km_hi = kmean.astype(BF16)
    km_lo = (kmean - km_hi.astype(F32)).astype(BF16)

    lane_q = lax.broadcasted_iota(jnp.int32, (bs, LANES), 1)
    head_mask = (lane_q < HEAD_DIM, lane_q >= HEAD_DIM)
    blk_id = lax.broadcasted_iota(jnp.int32, (nb, bs), 0)

    def score_block(i):
        q = q_ref[0, i * bs:(i + 1) * bs, :]
        n_keys = (i + 1) * bs
        for h in range(2):
            qh = jnp.where(head_mask[h], q, jnp.zeros_like(q))
            g_scr[i % 2, h] = _dot_nt(km_hi, qh) + _dot_nt(km_lo, qh)
            s_scr[i % 2, h, :n_keys, :] = _dot_nt(k_ref[0, :n_keys, :], qh)

    def attend_block(i):
        slot = i % 2
        n_keys = (i + 1) * bs
        outs = []
        for h in range(2):
            past = blk_id < i
            g = jnp.where(past, g_scr[slot, h], NEG_INF)
            allow = blk_id == i
            for _ in range(MOBA_TOPK):
                mx = jnp.max(g, axis=0, keepdims=True)
                first = jnp.min(jnp.where(g == mx, blk_id, nb), axis=0, keepdims=True)
                pick = blk_id == first
                allow = allow | (pick & past)
                g = jnp.where(pick, NEG_INF, g)
            blk_bias = jnp.where(allow, 0.0, NEG_INF)

            mx8 = None
            for jt in range(i + 1):
                rows = slice(jt * bs, (jt + 1) * bs)
                t = s_scr[slot, h, rows, :] + tab_ref[0, h, min(i - jt, ntab - 1)] + blk_bias[jt:jt + 1, :]
                s_scr[slot, h, rows, :] = t
                t8 = jnp.max(t.reshape(bs // 8, 8, bs), axis=0)
                mx8 = t8 if mx8 is None else jnp.maximum(mx8, t8)
            m = jnp.max(mx8, axis=0, keepdims=True)
            for jt in range(i + 1):
                rows = slice(jt * bs, (jt + 1) * bs)
                p_scr[h, rows, :] = jnp.exp(s_scr[slot, h, rows, :] - m).astype(BF16)
            ov = _dot(vt_ext[h, :, :n_keys], p_scr[h, :n_keys, :])
            outs.append(ov[:HEAD_DIM, :] / ov[HEAD_DIM:HEAD_DIM + 1, :])
        o_t = jnp.concatenate(outs, axis=0)
        o_ref[0, i * bs:(i + 1) * bs, :] = o_t.T

    score_block(0)
    for i in range(nb):
        if i + 1 < nb:
            score_block(i + 1)
        attend_block(i)


def _moba(b1, tabs):
    B, _, S, _ = b1.shape
    npair = N_HEADS_B // 2

    def col(off):
        return pl.BlockSpec((1, None, S, LANES), lambda b, p, off=off: (b, off + p, 0, 0))

    return pl.pallas_call(
        functools.partial(_moba_kernel, seq=S),
        grid=(B, npair),
        in_specs=[col(0), col(npair), col(2 * npair),
                  pl.BlockSpec((1,) + tabs.shape[1:], lambda b, p: (p, 0, 0, 0, 0))],
        out_specs=pl.BlockSpec((1, None, S, LANES), lambda b, p: (b, p, 0, 0)),
        out_shape=jax.ShapeDtypeStruct((B, npair, S, LANES), F32),
        scratch_shapes=[pltpu.VMEM((2, HEAD_DIM + BF16_SUBLANES, S), BF16),
                        pltpu.VMEM((2, 2, S, MOBA_BLOCK), F32),
                        pltpu.VMEM((2, S, MOBA_BLOCK), BF16),
                        pltpu.VMEM((2, 2, S // MOBA_BLOCK, MOBA_BLOCK), F32)],
        compiler_params=pltpu.CompilerParams(dimension_semantics=("arbitrary", "arbitrary"),
                                             vmem_limit_bytes=VMEM_LIMIT),
        name="moba",
    )(b1, b1, b1, tabs)


def _post_kernel(x_ref, oa_ref, ob_ref, kv_ref, ga_ref, gb_ref, wout_ref, gc_ref, wcq_ref, wco_ref,
                 gf2_ref, wgu_ref, wd_ref, gfin_ref, y_ref, *, d_ff, final_norm):
    d = x_ref.shape[2]
    oa = jnp.concatenate([oa_ref[0, c] for c in range(oa_ref.shape[1])], axis=1)
    ob = jnp.concatenate([ob_ref[0, c] for c in range(ob_ref.shape[1])], axis=1)
    ya = _rms(oa, ga_ref[...]).astype(BF16)
    yb = _rms(ob, gb_ref[...]).astype(BF16)
    x = x_ref[0] + _dot(jnp.concatenate([ya, yb], axis=1), wout_ref[...])

    h = _rms(x, gc_ref[...]).astype(BF16)
    q = _dot(h, wcq_ref[...])
    dh = d // CROSS_HEADS
    scale = dh ** -0.5
    scores = [_dot_nt((q[:, c * dh:(c + 1) * dh] * scale).astype(BF16), kv_ref[0, :, c * dh:(c + 1) * dh])
              for c in range(CROSS_HEADS)]
    heads = []
    for c, s in enumerate(scores):
        e = jnp.exp(s - jnp.max(s, axis=-1, keepdims=True))
        p = e / jnp.sum(e, axis=-1, keepdims=True)
        heads.append(_dot(p.astype(BF16), kv_ref[0, :, d + c * dh:d + (c + 1) * dh]).astype(BF16))
    x = x + _dot(jnp.concatenate(heads, axis=1), wco_ref[...])

    h = _rms(x, gf2_ref[...]).astype(BF16)
    x = x + 0.5 * _swiglu_half(h, wgu_ref, wd_ref, d_ff)
    y_ref[0] = _rms(x, gfin_ref[...]) if final_norm else x


def _post(x1, oa, ob, kv, g_out_a, g_out_b, w_out, g_cross, w_cq, w_co, g_ffn2, wgu, wd, g_final, final_norm):
    B, S, D = x1.shape
    ta, tb = oa.shape[1], ob.shape[1]
    wa, wb = ta * LANES, tb * LANES
    M = kv.shape[1]
    d_ff = wd.shape[0]
    tm = TOKEN_TILE
    tok = lambda w: pl.BlockSpec((1, tm, w), lambda b, i: (b, i, 0))
    slab = lambda t: pl.BlockSpec((1, t, tm, LANES), lambda b, i: (b, 0, i, 0))
    return pl.pallas_call(
        functools.partial(_post_kernel, d_ff=d_ff, final_norm=final_norm),
        grid=(B, S // tm),
        in_specs=[tok(D), slab(ta), slab(tb),
                  pl.BlockSpec((1, M, 2 * D), lambda b, i: (b, 0, 0)),
                  _resident((1, wa)), _resident((1, wb)), _resident((wa + wb, D)),
                  _resident((1, D)), _resident((D, D)), _resident((D, D)),
                  _resident((1, D)), _resident((D, 2 * d_ff)), _resident((d_ff, D)),
                  _resident((1, D))],
        out_specs=tok(D),
        out_shape=jax.ShapeDtypeStruct((B, S, D), F32),
        compiler_params=pltpu.CompilerParams(dimension_semantics=("arbitrary", "arbitrary"),
                                             vmem_limit_bytes=VMEM_LIMIT),
        name="post",
    )(x1, oa, ob, kv, g_out_a.reshape(1, wa), g_out_b.reshape(1, wb), w_out, g_cross.reshape(1, D),
      w_cq, w_co, g_ffn2.reshape(1, D), wgu, wd, g_final.reshape(1, D))


def kernel(x, mem, g_ffn1, w_ffn1_gu, w_ffn1_down, g_mix, w_in, rel_bias, g_out_a, g_out_b, w_out,
           g_cross, g_mem, w_cq, w_ckv, w_co, g_ffn2, w_ffn2_gu, w_ffn2_down, g_final):
    B, S, D = x.shape
    depth = g_ffn1.shape[0]
    dil_tabs = _dilated_tables(rel_bias)
    moba_tabs = _moba_tables(rel_bias)
    bf = lambda w: w.astype(BF16)
    wqa, wka = N_HEADS_A * HEAD_DIM, 3 * N_HEADS_A * HEAD_DIM
    col = jnp.arange(w_in.shape[-1])
    is_q = (col < wqa) | ((col >= wka) & (col < wka + N_HEADS_B * HEAD_DIM))
    q_scale = jnp.where(is_q, HEAD_DIM ** -0.5, 1.0).astype(F32)
    y = None
    for l in range(depth):
        kv = _memkv(mem, g_mem[l], bf(w_ckv[l]))
        x1 = _ffn1(x.reshape(B * S, D), g_ffn1[l], bf(w_ffn1_gu[l]), bf(w_ffn1_down[l])).reshape(B, S, D)
        a1, a4, a16, b1 = _proj(x1, g_mix[l], bf(w_in[l] * q_scale))
        oa = _dilated(a1, a4, a16, dil_tabs)
        ob = _moba(b1, moba_tabs)
        y = _post(x1, oa, ob, kv, g_out_a[l], g_out_b[l], bf(w_out[l]), g_cross[l], bf(w_cq[l]),
                  bf(w_co[l]), g_ffn2[l], bf(w_ffn2_gu[l]), bf(w_ffn2_down[l]), g_final,
                  final_norm=(l == depth - 1))
        x = y
    return y
```

```python
import functools
import math

import jax
import jax.numpy as jnp
from jax import lax
from jax.experimental import pallas as pl
from jax.experimental.pallas import tpu as pltpu

F32 = jnp.float32
BF16 = jnp.bfloat16

HEAD_DIM = 64
LANES = 128
BF16_SUBLANES = 16
N_HEADS_A = 12
N_HEADS_B = 4
DILATED_PATTERNS = ((128, 1), (512, 4), (2048, 16))
DIL_BLOCK = 128
DIL_GROUP = 16
MOBA_BLOCK = 256
MOBA_TOPK = 3
CROSS_HEADS = 4
REL_BUCKETS = 32
REL_MAX_DIST = 2048
NORM_EPS = 1e-6
NEG_INF = -jnp.inf
VMEM_LIMIT = 56 * 1024 * 1024

FFN_CHUNK = 256
TOKEN_TILE = 512
FFN1_TOKEN_TILE = 1024


def _rms(x, g):
    xf = x.astype(F32)
    return xf * lax.rsqrt(jnp.mean(xf * xf, axis=-1, keepdims=True) + NORM_EPS) * g


def _dot(a, b):
    return jnp.dot(a, b, preferred_element_type=F32)


def _dot_nt(a, b):
    return lax.dot_general(a, b, (((1,), (1,)), ((), ())), preferred_element_type=F32)


def _resident(shape):
    nd = len(shape)
    return pl.BlockSpec(shape, lambda *_: (0,) * nd, pipeline_mode=pl.Buffered(1))


def _swiglu_half(h, wgu_ref, wd_ref, d_ff):
    acc = None
    for c in range(d_ff // FFN_CHUNK):
        lo = c * FFN_CHUNK
        g = _dot(h, wgu_ref[:, lo:lo + FFN_CHUNK])
        u = _dot(h, wgu_ref[:, d_ff + lo:d_ff + lo + FFN_CHUNK])
        a = (g * (1.0 / (1.0 + jnp.exp(-g))) * u).astype(BF16)
        part = _dot(a, wd_ref[lo:lo + FFN_CHUNK, :])
        acc = part if acc is None else acc + part
    return acc


def _memkv_kernel(mem_ref, g_ref, w_ref, kv_ref):
    h = _rms(mem_ref[0], g_ref[...]).astype(BF16)
    kv_ref[0] = _dot(h, w_ref[...]).astype(BF16)


def _memkv(mem, g_mem, w_ckv):
    B, M, D = mem.shape
    N = w_ckv.shape[1]
    return pl.pallas_call(
        _memkv_kernel,
        grid=(B,),
        in_specs=[pl.BlockSpec((1, M, D), lambda b: (b, 0, 0)),
                  _resident((1, D)), _resident((D, N))],
        out_specs=pl.BlockSpec((1, M, N), lambda b: (b, 0, 0)),
        out_shape=jax.ShapeDtypeStruct((B, M, N), BF16),
        compiler_params=pltpu.CompilerParams(dimension_semantics=("arbitrary",),
                                             vmem_limit_bytes=VMEM_LIMIT),
        name="memkv",
    )(mem, g_mem.reshape(1, D), w_ckv)


def _ffn1_kernel(x_ref, g_ref, wgu_ref, wd_ref, o_ref, *, d_ff):
    x = x_ref[...]
    h = _rms(x, g_ref[...]).astype(BF16)
    o_ref[...] = x + 0.5 * _swiglu_half(h, wgu_ref, wd_ref, d_ff)


def _ffn1(x2d, g, wgu, wd):
    T, D = x2d.shape
    d_ff = wd.shape[0]
    tm = FFN1_TOKEN_TILE
    return pl.pallas_call(
        functools.partial(_ffn1_kernel, d_ff=d_ff),
        grid=(T // tm,),
        in_specs=[pl.BlockSpec((tm, D), lambda i: (i, 0)),
                  _resident((1, D)), _resident((D, 2 * d_ff)), _resident((d_ff, D))],
        out_specs=pl.BlockSpec((tm, D), lambda i: (i, 0)),
        out_shape=jax.ShapeDtypeStruct((T, D), F32),
        compiler_params=pltpu.CompilerParams(dimension_semantics=("arbitrary",),
                                             vmem_limit_bytes=VMEM_LIMIT),
        name="ffn1",
    )(x2d, g.reshape(1, D), wgu, wd)


def _proj_kernel(x_ref, g_ref, w_ref, a1_ref, a4_ref, a16_ref, b_ref, scr, scr4, *, wa):
    tm = x_ref.shape[1]
    h = _rms(x_ref[0], g_ref[...]).astype(BF16)
    pb = _dot(h, w_ref[:, wa:])
    for c in range(pb.shape[1] // LANES):
        b_ref[0, c] = pb[:, c * LANES:(c + 1) * LANES].astype(BF16)
    group = 2 * LANES
    for c in range(wa // LANES):
        if c % (group // LANES) == 0:
            pa = _dot(h, w_ref[:, c * LANES:c * LANES + group])
        sub = slice((c * LANES) % group, (c * LANES) % group + LANES)
        tile = pa[:, sub]
        a1_ref[0, c] = tile.astype(BF16)
        b = c % 2
        scr[b] = tile
        for r in range(4):
            plane = scr[b, pl.ds(r, tm // 4, stride=4), :]
            a4_ref[0, c, r] = plane.astype(BF16)
            scr4[b, r] = plane
        for r in range(16):
            a16_ref[0, c, r] = scr4[b, r % 4, pl.ds(r // 4, tm // 16, stride=4), :].astype(BF16)


def _proj(x1, g, w_in):
    B, S, D = x1.shape
    wa = 3 * N_HEADS_A * HEAD_DIM
    wb = 3 * N_HEADS_B * HEAD_DIM
    ta, tb = wa // LANES, wb // LANES
    tm = TOKEN_TILE
    kern = functools.partial(_proj_kernel, wa=wa)
    a1, a4, a16, b1 = pl.pallas_call(
        kern,
        grid=(B, S // tm),
        in_specs=[pl.BlockSpec((1, tm, D), lambda b, i: (b, i, 0)),
                  _resident((1, D)), _resident((D, wa + wb))],
        out_specs=[pl.BlockSpec((1, ta, tm, LANES), lambda b, i: (b, 0, i, 0)),
                   pl.BlockSpec((1, ta, 4, tm // 4, LANES), lambda b, i: (b, 0, 0, i, 0)),
                   pl.BlockSpec((1, ta, 16, tm // 16, LANES), lambda b, i: (b, 0, 0, i, 0)),
                   pl.BlockSpec((1, tb, tm, LANES), lambda b, i: (b, 0, i, 0))],
        out_shape=[jax.ShapeDtypeStruct((B, ta, S, LANES), BF16),
                   jax.ShapeDtypeStruct((B, ta, 4, S // 4, LANES), BF16),
                   jax.ShapeDtypeStruct((B, ta, 16, S // 16, LANES), BF16),
                   jax.ShapeDtypeStruct((B, tb, S, LANES), BF16)],
        scratch_shapes=[pltpu.VMEM((2, tm, LANES), F32), pltpu.VMEM((2, 4, tm // 4, LANES), F32)],
        compiler_params=pltpu.CompilerParams(dimension_semantics=("arbitrary", "arbitrary"),
                                             vmem_limit_bytes=VMEM_LIMIT),
        name="proj",
    )(x1, g.reshape(1, D), w_in)
    return a1, a4.reshape(a1.shape), a16.reshape(a1.shape), b1


def _rel_bucket(dist):
    max_exact = REL_BUCKETS // 2
    n = jnp.maximum(dist, 0)
    nf = jnp.maximum(n, 1).astype(F32)
    large = max_exact + (jnp.log(nf / max_exact) / math.log(REL_MAX_DIST / max_exact)
                         * (REL_BUCKETS - max_exact)).astype(jnp.int32)
    large = jnp.minimum(large, REL_BUCKETS - 1)
    return jnp.where(n < max_exact, n, large)


def _bias_lookup(dist, bias_cols, spec):
    onehot = (_rel_bucket(dist)[..., None] == jnp.arange(REL_BUCKETS)).astype(F32)
    return jnp.einsum(spec, onehot, bias_cols.astype(F32), precision=lax.Precision.HIGHEST)


def _dilated_tables(rel_bias):
    qb = DIL_BLOCK
    npat = len(DILATED_PATTERNS)
    dil = jnp.array([d for _, d in DILATED_PATTERNS], jnp.int32)[:, None, None, None]
    n_steps = jnp.array([w // d for w, d in DILATED_PATTERNS], jnp.int32)[:, None, None, None]
    variant = jnp.arange(3)[None, :, None, None]
    i = jnp.arange(qb)[None, None, :, None]
    j = jnp.arange(2 * qb)[None, None, None, :]
    delta = jnp.where(variant == 2, i - j, qb + i - j)
    ok = (delta >= 0) & (delta <= n_steps) & ((variant != 1) | (j >= qb))
    bias_cols = rel_bias[:, :N_HEADS_A].reshape(REL_BUCKETS, N_HEADS_A // 2, 2)
    t = _bias_lookup(delta * dil, bias_cols, "pvijb,bnk->npvkij")
    t = jnp.where(ok[None, :, :, None, :, :], t, NEG_INF)
    return t.reshape(N_HEADS_A // 2, npat, 3, 2 * qb, 2 * qb)


def _moba_num_tables():
    return -(-(REL_MAX_DIST + MOBA_BLOCK - 1) // MOBA_BLOCK) + 1


def _moba_tables(rel_bias):
    bs = MOBA_BLOCK
    off = jnp.arange(_moba_num_tables())[:, None, None]
    j = jnp.arange(bs)[None, :, None]
    i = jnp.arange(bs)[None, None, :]
    n = off * bs + i - j
    bias_cols = rel_bias[:, N_HEADS_A:].reshape(REL_BUCKETS, N_HEADS_B // 2, 2)
    t = _bias_lookup(n, bias_cols, "vijb,bnk->nkvij")
    return jnp.where((n >= 0)[None, None], t, NEG_INF)


def _dil_kernel(q1, k1, v1, q4, k4, v4, q16, k16, v16, tab_ref, o_ref, oscr, sc_scr, *, seq):
    qb = DIL_BLOCK
    nblk = seq // qb
    qrefs, krefs, vrefs = (q1, q4, q16), (k1, k4, k16), (v1, v4, v16)
    dils = tuple(d for _, d in DILATED_PATTERNS)
    ones_tile = jnp.ones((2 * qb, LANES), BF16)

    row2 = lax.broadcasted_iota(jnp.int32, (2 * qb, LANES), 0)
    lane2 = lax.broadcasted_iota(jnp.int32, (2 * qb, LANES), 1)
    own = (row2 < qb) == (lane2 < HEAD_DIM)
    lane1 = lax.broadcasted_iota(jnp.int32, (qb, LANES), 1)

    def band_start(blk):
        return pl.multiple_of(jnp.maximum(blk - 1, 0) * qb, qb)

    def score_block(blk, slot):
        q0 = pl.multiple_of(blk * qb, qb)
        for p in range(3):
            q = qrefs[p][0, pl.ds(q0, qb), :]
            q2 = jnp.concatenate([q, q], axis=0)
            qs = jnp.where(own, q2, jnp.zeros_like(q2))
            sc_scr[slot, p] = _dot_nt(qs, krefs[p][0, pl.ds(band_start(blk), 2 * qb), :])

    score_block(0, 0)

    def softmax_block(blk, slot):
        q0 = pl.multiple_of(blk * qb, qb)
        k0 = band_start(blk)
        for p in range(3):
            per = nblk // dils[p]
            variant = jnp.where(blk == 0, 2, jnp.where(blk % per == 0, 1, 0))
            s = sc_scr[slot, p] + tab_ref[0, p, variant]
            m = jnp.max(s, axis=-1, keepdims=True)
            pr = jnp.exp(s - m).astype(BF16)
            pv = _dot(pr, jnp.concatenate([vrefs[p][0, pl.ds(k0, 2 * qb), :], ones_tile], axis=1))
            h0 = lane1 < HEAD_DIM
            num = jnp.where(h0, pv[:qb, :LANES], pv[qb:, :LANES])
            den = jnp.where(h0, pv[:qb, LANES:], pv[qb:, LANES:])
            oscr[p, 0, pl.ds(q0, qb), :] = num / den
            oscr[p, 1, pl.ds(q0, qb), :] = jnp.where(h0, m[:qb], m[qb:]) + jnp.log(den)

    def block_group(i, carry):
        b0 = DIL_GROUP * i
        for g in range(DIL_GROUP):
            score_block(jnp.minimum(b0 + g + 1, nblk - 1), (g + 1) % 2)
            softmax_block(b0 + g, g % 2)
        return carry

    lax.fori_loop(0, nblk // DIL_GROUP, block_group, 0)

    n16 = seq // 16
    n4 = seq // 4
    mc_rows = 128

    def merge(r, carry):
        for mc in range(n16 // mc_rows):
            m_lo = mc * mc_rows
            rows = (pl.ds(r + 16 * m_lo, mc_rows, stride=16),
                    pl.ds((r % 4) * n4 + r // 4 + 4 * m_lo, mc_rows, stride=4),
                    pl.ds(pl.multiple_of(r * n16 + m_lo, mc_rows), mc_rows))
            lse = [oscr[p, 1, rows[p], :] for p in range(3)]
            top = jnp.maximum(jnp.maximum(lse[0], lse[1]), lse[2])
            w = [jnp.exp(x - top) for x in lse]
            inv = 1.0 / (w[0] + w[1] + w[2])
            out = None
            for p in range(3):
                term = (w[p] * inv) * oscr[p, 0, rows[p], :]
                out = term if out is None else out + term
            o_ref[0, rows[0], :] = out
        return carry

    lax.fori_loop(0, 16, merge, 0, unroll=8)


def _dilated(a1, a4, a16, tabs):
    B, _, S, _ = a1.shape
    assert (S // DIL_BLOCK) % DIL_GROUP == 0 and S % (16 * DIL_BLOCK) == 0
    npair = N_HEADS_A // 2
    nq = npair

    def col(off):
        return pl.BlockSpec((1, None, S, LANES), lambda b, p, off=off: (b, off + p, 0, 0))

    in_specs = []
    for _ in range(3):
        in_specs += [col(0), col(nq), col(2 * nq)]
    in_specs.append(pl.BlockSpec((1,) + tabs.shape[1:], lambda b, p: (p, 0, 0, 0, 0)))
    return pl.pallas_call(
        functools.partial(_dil_kernel, seq=S),
        grid=(B, npair),
        in_specs=in_specs,
        out_specs=pl.BlockSpec((1, None, S, LANES), lambda b, p: (b, p, 0, 0)),
        out_shape=jax.ShapeDtypeStruct((B, npair, S, LANES), F32),
        scratch_shapes=[pltpu.VMEM((3, 2, S, LANES), F32),
                        pltpu.VMEM((2, 3, 2 * DIL_BLOCK, 2 * DIL_BLOCK), F32)],
        compiler_params=pltpu.CompilerParams(dimension_semantics=("arbitrary", "arbitrary"),
                                             vmem_limit_bytes=VMEM_LIMIT),
        name="dilated",
    )(a1, a1, a1, a4, a4, a4, a16, a16, a16, tabs)


def _moba_kernel(q_ref, k_ref, v_ref, tab_ref, o_ref, vt_ext, s_scr, p_scr, g_scr, *, seq):
    bs = MOBA_BLOCK
    nb = seq // bs
    ntab = tab_ref.shape[2]

    vt = v_ref[0].astype(F32).T
    ones_rows = jnp.ones((vt_ext.shape[1] - HEAD_DIM, seq), BF16)
    for h in range(2):
        vt_ext[h, :HEAD_DIM, :] = vt[h * HEAD_DIM:(h + 1) * HEAD_DIM, :].astype(BF16)
        vt_ext[h, HEAD_DIM:, :] = ones_rows

    kmean = jnp.sum(k_ref[0].astype(F32).reshape(nb, bs, LANES), axis=1) * (1.0 / bs)
    km_hi = kmean.astype(BF16)
    km_lo = (kmean - km_hi.astype(F32)).astype(BF16)

    lane_q = lax.broadcasted_iota(jnp.int32, (bs, LANES), 1)
    head_mask = (lane_q < HEAD_DIM, lane_q >= HEAD_DIM)
    blk_id = lax.broadcasted_iota(jnp.int32, (nb, bs), 0)

    def score_block(i):
        q = q_ref[0, i * bs:(i + 1) * bs, :]
        n_keys = (i + 1) * bs
        for h in range(2):
            qh = jnp.where(head_mask[h], q, jnp.zeros_like(q))
            g_scr[i % 2, h] = _dot_nt(km_hi, qh) + _dot_nt(km_lo, qh)
            s_scr[i % 2, h, :n_keys, :] = _dot_nt(k_ref[0, :n_keys, :], qh)

    def attend_block(i):
        slot = i % 2
        n_keys = (i + 1) * bs
        outs = []
        for h in range(2):
            past = blk_id < i
            g = jnp.where(past, g_scr[slot, h], NEG_INF)
            allow = blk_id == i
            for _ in range(MOBA_TOPK):
                mx = jnp.max(g, axis=0, keepdims=True)
                first = jnp.min(jnp.where(g == mx, blk_id, nb), axis=0, keepdims=True)
                pick = blk_id == first
                allow = allow | (pick & past)
                g = jnp.where(pick, NEG_INF, g)
            blk_bias = jnp.where(allow, 0.0, NEG_INF)

            mx8 = None
            for jt in range(i + 1):
                rows = slice(jt * bs, (jt + 1) * bs)
                t = s_scr[slot, h, rows, :] + tab_ref[0, h, min(i - jt, ntab - 1)] + blk_bias[jt:jt + 1, :]
                s_scr[slot, h, rows, :] = t
                t8 = jnp.max(t.reshape(bs // 8, 8, bs), axis=0)
                mx8 = t8 if mx8 is None else jnp.maximum(mx8, t8)
            m = jnp.max(mx8, axis=0, keepdims=True)
            for jt in range(i + 1):
                rows = slice(jt * bs, (jt + 1) * bs)
                p_scr[h, rows, :] = jnp.exp(s_scr[slot, h, rows, :] - m).astype(BF16)
            ov = _dot(vt_ext[h, :, :n_keys], p_scr[h, :n_keys, :])
            outs.append(ov[:HEAD_DIM, :] / ov[HEAD_DIM:HEAD_DIM + 1, :])
        o_t = jnp.concatenate(outs, axis=0)
        o_ref[0, i * bs:(i + 1) * bs, :] = o_t.T

    score_block(0)
    for i in range(nb):
        if i + 1 < nb:
            score_block(i + 1)
        attend_block(i)


def _moba(b1, tabs):
    B, _, S, _ = b1.shape
    npair = N_HEADS_B // 2

    def col(off):
        return pl.BlockSpec((1, None, S, LANES), lambda b, p, off=off: (b, off + p, 0, 0))

    return pl.pallas_call(
        functools.partial(_moba_kernel, seq=S),
        grid=(B, npair),
        in_specs=[col(0), col(npair), col(2 * npair),
                  pl.BlockSpec((1,) + tabs.shape[1:], lambda b, p: (p, 0, 0, 0, 0))],
        out_specs=pl.BlockSpec((1, None, S, LANES), lambda b, p: (b, p, 0, 0)),
        out_shape=jax.ShapeDtypeStruct((B, npair, S, LANES), F32),
        scratch_shapes=[pltpu.VMEM((2, HEAD_DIM + BF16_SUBLANES, S), BF16),
                        pltpu.VMEM((2, 2, S, MOBA_BLOCK), F32),
                        pltpu.VMEM((2, S, MOBA_BLOCK), BF16),
                        pltpu.VMEM((2, 2, S // MOBA_BLOCK, MOBA_BLOCK), F32)],
        compiler_params=pltpu.CompilerParams(dimension_semantics=("arbitrary", "arbitrary"),
                                             vmem_limit_bytes=VMEM_LIMIT),
        name="moba",
    )(b1, b1, b1, tabs)


def _post_kernel(x_ref, oa_ref, ob_ref, kv_ref, ga_ref, gb_ref, wout_ref, gc_ref, wcq_ref, wco_ref,
                 gf2_ref, wgu_ref, wd_ref, gfin_ref, y_ref, *, d_ff, final_norm):
    d = x_ref.shape[2]
    oa = jnp.concatenate([oa_ref[0, c] for c in range(oa_ref.shape[1])], axis=1)
    ob = jnp.concatenate([ob_ref[0, c] for c in range(ob_ref.shape[1])], axis=1)
    ya = _rms(oa, ga_ref[...]).astype(BF16)
    yb = _rms(ob, gb_ref[...]).astype(BF16)
    x = x_ref[0] + _dot(jnp.concatenate([ya, yb], axis=1), wout_ref[...])

    h = _rms(x, gc_ref[...]).astype(BF16)
    q = _dot(h, wcq_ref[...])
    dh = d // CROSS_HEADS
    scale = dh ** -0.5
    scores = [_dot_nt((q[:, c * dh:(c + 1) * dh] * scale).astype(BF16), kv_ref[0, :, c * dh:(c + 1) * dh])
              for c in range(CROSS_HEADS)]
    heads = []
    for c, s in enumerate(scores):
        e = jnp.exp(s - jnp.max(s, axis=-1, keepdims=True))
        p = e / jnp.sum(e, axis=-1, keepdims=True)
        heads.append(_dot(p.astype(BF16), kv_ref[0, :, d + c * dh:d + (c + 1) * dh]).astype(BF16))
    x = x + _dot(jnp.concatenate(heads, axis=1), wco_ref[...])

    h = _rms(x, gf2_ref[...]).astype(BF16)
    x = x + 0.5 * _swiglu_half(h, wgu_ref, wd_ref, d_ff)
    y_ref[0] = _rms(x, gfin_ref[...]) if final_norm else x


def _post(x1, oa, ob, kv, g_out_a, g_out_b, w_out, g_cross, w_cq, w_co, g_ffn2, wgu, wd, g_final, final_norm):
    B, S, D = x1.shape
    ta, tb = oa.shape[1], ob.shape[1]
    wa, wb = ta * LANES, tb * LANES
    M = kv.shape[1]
    d_ff = wd.shape[0]
    tm = TOKEN_TILE
    tok = lambda w: pl.BlockSpec((1, tm, w), lambda b, i: (b, i, 0))
    slab = lambda t: pl.BlockSpec((1, t, tm, LANES), lambda b, i: (b, 0, i, 0))
    return pl.pallas_call(
        functools.partial(_post_kernel, d_ff=d_ff, final_norm=final_norm),
        grid=(B, S // tm),
        in_specs=[tok(D), slab(ta), slab(tb),
                  pl.BlockSpec((1, M, 2 * D), lambda b, i: (b, 0, 0)),
                  _resident((1, wa)), _resident((1, wb)), _resident((wa + wb, D)),
                  _resident((1, D)), _resident((D, D)), _resident((D, D)),
                  _resident((1, D)), _resident((D, 2 * d_ff)), _resident((d_ff, D)),
                  _resident((1, D))],
        out_specs=tok(D),
        out_shape=jax.ShapeDtypeStruct((B, S, D), F32),
        compiler_params=pltpu.CompilerParams(dimension_semantics=("arbitrary", "arbitrary"),
                                             vmem_limit_bytes=VMEM_LIMIT),
        name="post",
    )(x1, oa, ob, kv, g_out_a.reshape(1, wa), g_out_b.reshape(1, wb), w_out, g_cross.reshape(1, D),
      w_cq, w_co, g_ffn2.reshape(1, D), wgu, wd, g_final.reshape(1, D))


def kernel(x, mem, g_ffn1, w_ffn1_gu, w_ffn1_down, g_mix, w_in, rel_bias, g_out_a, g_out_b, w_out,
           g_cross, g_mem, w_cq, w_ckv, w_co, g_ffn2, w_ffn2_gu, w_ffn2_down, g_final):
    B, S, D = x.shape
    depth = g_ffn1.shape[0]
    dil_tabs = _dilated_tables(rel_bias)
    moba_tabs = _moba_tables(rel_bias)
    bf = lambda w: w.astype(BF16)
    wqa, wka = N_HEADS_A * HEAD_DIM, 3 * N_HEADS_A * HEAD_DIM
    col = jnp.arange(w_in.shape[-1])
    is_q = (col < wqa) | ((col >= wka) & (col < wka + N_HEADS_B * HEAD_DIM))
    q_scale = jnp.where(is_q, HEAD_DIM ** -0.5, 1.0).astype(F32)
    y = None
    for l in range(depth):
        kv = _memkv(mem, g_mem[l], bf(w_ckv[l]))
        x1 = _ffn1(x.reshape(B * S, D), g_ffn1[l], bf(w_ffn1_gu[l]), bf(w_ffn1_down[l])).reshape(B, S, D)
        a1, a4, a16, b1 = _proj(x1, g_mix[l], bf(w_in[l] * q_scale))
        oa = _dilated(a1, a4, a16, dil_tabs)
        ob = _moba(b1, moba_tabs)
        y = _post(x1, oa, ob, kv, g_out_a[l], g_out_b[l], bf(w_out[l]), g_cross[l], bf(w_cq[l]),
                  bf(w_co[l]), g_ffn2[l], bf(w_ffn2_gu[l]), bf(w_ffn2_down[l]), g_final,
                  final_norm=(l == depth - 1))
        x = y
    return y
```

```python
import functools
import math

import numpy as np
import jax
import jax.numpy as jnp
from jax import lax
from jax.experimental import pallas as pl
from jax.experimental.pallas import tpu as pltpu

F32 = jnp.float32
BF16 = jnp.bfloat16

HEAD_DIM = 64
LANES = 128
BF16_SUBLANES = 16
N_HEADS_A = 12
N_HEADS_B = 4
DILATED_PATTERNS = ((128, 1), (512, 4), (2048, 16))
DIL_BLOCK = 128
DIL_GROUP = 16
MOBA_BLOCK = 256
MOBA_TOPK = 3
CROSS_HEADS = 4
REL_BUCKETS = 32
REL_MAX_DIST = 2048
NORM_EPS = 1e-6
NEG_INF = -jnp.inf
VMEM_LIMIT = 56 * 1024 * 1024

FFN_CHUNK = 256
TOKEN_TILE = 512
FFN1_TOKEN_TILE = 1024
PROJ_TOKEN_TILE = 1024


def _rms(x, g):
    xf = x.astype(F32)
    return xf * lax.rsqrt(jnp.mean(xf * xf, axis=-1, keepdims=True) + NORM_EPS) * g


def _dot(a, b):
    return jnp.dot(a, b, preferred_element_type=F32)


def _dot_nt(a, b):
    return lax.dot_general(a, b, (((1,), (1,)), ((), ())), preferred_element_type=F32)


def _resident(shape):
    nd = len(shape)
    return pl.BlockSpec(shape, lambda *_: (0,) * nd, pipeline_mode=pl.Buffered(1))


def _swiglu_half(h, wgu_ref, wd_ref, d_ff):
    n_chunks = d_ff // FFN_CHUNK

    def up(c):
        lo = c * FFN_CHUNK
        return (_dot(h, wgu_ref[:, lo:lo + FFN_CHUNK]),
                _dot(h, wgu_ref[:, d_ff + lo:d_ff + lo + FFN_CHUNK]))

    acc = None
    gu = up(0)
    for c in range(n_chunks):
        nxt = up(c + 1) if c + 1 < n_chunks else None
        g, u = gu
        a = (g * (1.0 / (1.0 + jnp.exp(-g))) * u).astype(BF16)
        part = _dot(a, wd_ref[c * FFN_CHUNK:(c + 1) * FFN_CHUNK, :])
        acc = part if acc is None else acc + part
        gu = nxt
    return acc


def _memkv_kernel(mem_ref, g_ref, w_ref, kv_ref):
    h = _rms(mem_ref[0], g_ref[...]).astype(BF16)
    kv_ref[0] = _dot(h, w_ref[...]).astype(BF16)


def _memkv(mem, g_mem, w_ckv):
    B, M, D = mem.shape
    N = w_ckv.shape[1]
    return pl.pallas_call(
        _memkv_kernel,
        grid=(B,),
        in_specs=[pl.BlockSpec((1, M, D), lambda b: (b, 0, 0)),
                  _resident((1, D)), _resident((D, N))],
        out_specs=pl.BlockSpec((1, M, N), lambda b: (b, 0, 0)),
        out_shape=jax.ShapeDtypeStruct((B, M, N), BF16),
        compiler_params=pltpu.CompilerParams(dimension_semantics=("arbitrary",),
                                             vmem_limit_bytes=VMEM_LIMIT),
        name="memkv",
    )(mem, g_mem.reshape(1, D), w_ckv)


def _ffn1_kernel(x_ref, g_ref, wgu_ref, wd_ref, o_ref, *, d_ff):
    x = x_ref[...]
    h = _rms(x, g_ref[...]).astype(BF16)
    o_ref[...] = x + 0.5 * _swiglu_half(h, wgu_ref, wd_ref, d_ff)


def _ffn1(x2d, g, wgu, wd):
    T, D = x2d.shape
    d_ff = wd.shape[0]
    tm = FFN1_TOKEN_TILE
    return pl.pallas_call(
        functools.partial(_ffn1_kernel, d_ff=d_ff),
        grid=(T // tm,),
        in_specs=[pl.BlockSpec((tm, D), lambda i: (i, 0)),
                  _resident((1, D)), _resident((D, 2 * d_ff)), _resident((d_ff, D))],
        out_specs=pl.BlockSpec((tm, D), lambda i: (i, 0)),
        out_shape=jax.ShapeDtypeStruct((T, D), F32),
        compiler_params=pltpu.CompilerParams(dimension_semantics=("arbitrary",),
                                             vmem_limit_bytes=VMEM_LIMIT),
        name="ffn1",
    )(x2d, g.reshape(1, D), wgu, wd)


def _proj_kernel(x_ref, g_ref, w_ref, a1_ref, a4_ref, a16_ref, b_ref, scr, scr4, *, wa):
    tm = x_ref.shape[1]
    h = _rms(x_ref[0], g_ref[...]).astype(BF16)
    pb = _dot(h, w_ref[:, wa:])
    for c in range(pb.shape[1] // LANES):
        b_ref[0, c] = pb[:, c * LANES:(c + 1) * LANES].astype(BF16)
    group = 2 * LANES
    for c in range(wa // LANES):
        if c % (group // LANES) == 0:
            pa = _dot(h, w_ref[:, c * LANES:c * LANES + group])
        sub = slice((c * LANES) % group, (c * LANES) % group + LANES)
        tile = pa[:, sub]
        a1_ref[0, c] = tile.astype(BF16)
        b = c % 2
        scr[b] = tile
        for r in range(4):
            plane = scr[b, pl.ds(r, tm // 4, stride=4), :]
            a4_ref[0, c, r] = plane.astype(BF16)
            scr4[b, r] = plane
        for r in range(16):
            a16_ref[0, c, r] = scr4[b, r % 4, pl.ds(r // 4, tm // 16, stride=4), :].astype(BF16)


def _proj(x1, g, w_in):
    B, S, D = x1.shape
    wa = 3 * N_HEADS_A * HEAD_DIM
    wb = 3 * N_HEADS_B * HEAD_DIM
    ta, tb = wa // LANES, wb // LANES
    tm = PROJ_TOKEN_TILE
    kern = functools.partial(_proj_kernel, wa=wa)
    a1, a4, a16, b1 = pl.pallas_call(
        kern,
        grid=(B, S // tm),
        in_specs=[pl.BlockSpec((1, tm, D), lambda b, i: (b, i, 0)),
                  _resident((1, D)), _resident((D, wa + wb))],
        out_specs=[pl.BlockSpec((1, ta, tm, LANES), lambda b, i: (b, 0, i, 0)),
                   pl.BlockSpec((1, ta, 4, tm // 4, LANES), lambda b, i: (b, 0, 0, i, 0)),
                   pl.BlockSpec((1, ta, 16, tm // 16, LANES), lambda b, i: (b, 0, 0, i, 0)),
                   pl.BlockSpec((1, tb, tm, LANES), lambda b, i: (b, 0, i, 0))],
        out_shape=[jax.ShapeDtypeStruct((B, ta, S, LANES), BF16),
                   jax.ShapeDtypeStruct((B, ta, 4, S // 4, LANES), BF16),
                   jax.ShapeDtypeStruct((B, ta, 16, S // 16, LANES), BF16),
                   jax.ShapeDtypeStruct((B, tb, S, LANES), BF16)],
        scratch_shapes=[pltpu.VMEM((2, tm, LANES), F32), pltpu.VMEM((2, 4, tm // 4, LANES), F32)],
        compiler_params=pltpu.CompilerParams(dimension_semantics=("arbitrary", "arbitrary"),
                                             vmem_limit_bytes=VMEM_LIMIT),
        name="proj",
    )(x1, g.reshape(1, D), w_in)
    return a1, a4.reshape(a1.shape), a16.reshape(a1.shape), b1


def _rel_bucket(dist):
    max_exact = REL_BUCKETS // 2
    n = jnp.maximum(dist, 0)
    nf = jnp.maximum(n, 1).astype(F32)
    large = max_exact + (jnp.log(nf / max_exact) / math.log(REL_MAX_DIST / max_exact)
                         * (REL_BUCKETS - max_exact)).astype(jnp.int32)
    large = jnp.minimum(large, REL_BUCKETS - 1)
    return jnp.where(n < max_exact, n, large)


def _bias_lookup(dist, bias_cols, spec):
    onehot = (_rel_bucket(dist)[..., None] == jnp.arange(REL_BUCKETS)).astype(F32)
    return jnp.einsum(spec, onehot, bias_cols.astype(F32), precision=lax.Precision.HIGHEST)


def _dilated_tables(rel_bias):
    qb = DIL_BLOCK
    npat = len(DILATED_PATTERNS)
    dil = jnp.array([d for _, d in DILATED_PATTERNS], jnp.int32)[:, None, None, None]
    n_steps = jnp.array([w // d for w, d in DILATED_PATTERNS], jnp.int32)[:, None, None, None]
    variant = jnp.arange(3)[None, :, None, None]
    i = jnp.arange(qb)[None, None, :, None]
    j = jnp.arange(2 * qb)[None, None, None, :]
    delta = jnp.where(variant == 2, i - j, qb + i - j)
    ok = (delta >= 0) & (delta <= n_steps) & ((variant != 1) | (j >= qb))
    bias_cols = rel_bias[:, :N_HEADS_A].reshape(REL_BUCKETS, N_HEADS_A // 2, 2)
    t = _bias_lookup(delta * dil, bias_cols, "pvijb,bnk->npvkij")
    t = jnp.where(ok[None, :, :, None, :, :], t, NEG_INF)
    return t.reshape(N_HEADS_A // 2, npat, 3, 2 * qb, 2 * qb)


def _moba_num_tables():
    return -(-(REL_MAX_DIST + MOBA_BLOCK - 1) // MOBA_BLOCK) + 1


def _moba_tables(rel_bias):
    bs = MOBA_BLOCK
    off = jnp.arange(_moba_num_tables())[:, None, None]
    j = jnp.arange(bs)[None, :, None]
    i = jnp.arange(bs)[None, None, :]
    n = off * bs + i - j
    bias_cols = rel_bias[:, N_HEADS_A:].reshape(REL_BUCKETS, N_HEADS_B // 2, 2)
    t = _bias_lookup(n, bias_cols, "vijb,bnk->nkvij")
    return jnp.where((n >= 0)[None, None], t, NEG_INF)


def _dil_kernel(q1, k1, v1, q4, k4, v4, q16, k16, v16, tab_ref, o_ref, oscr, sc_scr, *, seq):
    qb = DIL_BLOCK
    nblk = seq // qb
    qrefs, krefs, vrefs = (q1, q4, q16), (k1, k4, k16), (v1, v4, v16)
    dils = tuple(d for _, d in DILATED_PATTERNS)
    ones_tile = jnp.ones((2 * qb, LANES), BF16)

    row2 = lax.broadcasted_iota(jnp.int32, (2 * qb, LANES), 0)
    lane2 = lax.broadcasted_iota(jnp.int32, (2 * qb, LANES), 1)
    own = (row2 < qb) == (lane2 < HEAD_DIM)
    lane1 = lax.broadcasted_iota(jnp.int32, (qb, LANES), 1)

    def band_start(blk):
        return pl.multiple_of(jnp.maximum(blk - 1, 0) * qb, qb)

    def score_block(blk, slot):
        q0 = pl.multiple_of(blk * qb, qb)
        for p in range(3):
            q = qrefs[p][0, pl.ds(q0, qb), :]
            q2 = jnp.concatenate([q, q], axis=0)
            qs = jnp.where(own, q2, jnp.zeros_like(q2))
            sc_scr[slot, p] = _dot_nt(qs, krefs[p][0, pl.ds(band_start(blk), 2 * qb), :])

    score_block(0, 0)

    def softmax_block(blk, slot):
        q0 = pl.multiple_of(blk * qb, qb)
        k0 = band_start(blk)
        for p in range(3):
            per = nblk // dils[p]
            variant = jnp.where(blk == 0, 2, jnp.where(blk % per == 0, 1, 0))
            s = sc_scr[slot, p] + tab_ref[0, p, variant]
            m = jnp.max(s, axis=-1, keepdims=True)
            pr = jnp.exp(s - m).astype(BF16)
            pv = _dot(pr, jnp.concatenate([vrefs[p][0, pl.ds(k0, 2 * qb), :], ones_tile], axis=1))
            h0 = lane1 < HEAD_DIM
            num = jnp.where(h0, pv[:qb, :LANES], pv[qb:, :LANES])
            den = jnp.where(h0, pv[:qb, LANES:], pv[qb:, LANES:])
            oscr[p, 0, pl.ds(q0, qb), :] = num / den
            oscr[p, 1, pl.ds(q0, qb), :] = jnp.where(h0, m[:qb], m[qb:]) + jnp.log(den)

    def block_group(i, carry):
        b0 = DIL_GROUP * i
        for g in range(DIL_GROUP):
            score_block(jnp.minimum(b0 + g + 1, nblk - 1), (g + 1) % 2)
            softmax_block(b0 + g, g % 2)
        return carry

    lax.fori_loop(0, nblk // DIL_GROUP, block_group, 0)

    n16 = seq // 16
    n4 = seq // 4
    mc_rows = 128

    def merge(r, carry):
        for mc in range(n16 // mc_rows):
            m_lo = mc * mc_rows
            rows = (pl.ds(r + 16 * m_lo, mc_rows, stride=16),
                    pl.ds((r % 4) * n4 + r // 4 + 4 * m_lo, mc_rows, stride=4),
                    pl.ds(pl.multiple_of(r * n16 + m_lo, mc_rows), mc_rows))
            lse = [oscr[p, 1, rows[p], :] for p in range(3)]
            top = jnp.maximum(jnp.maximum(lse[0], lse[1]), lse[2])
            w = [jnp.exp(x - top) for x in lse]
            inv = 1.0 / (w[0] + w[1] + w[2])
            out = None
            for p in range(3):
                term = (w[p] * inv) * oscr[p, 0, rows[p], :]
                out = term if out is None else out + term
            o_ref[0, rows[0], :] = out
        return carry

    lax.fori_loop(0, 16, merge, 0, unroll=8)


def _dilated(a1, a4, a16, tabs):
    B, _, S, _ = a1.shape
    npair = N_HEADS_A // 2
    nq = npair

    def col(off):
        return pl.BlockSpec((1, None, S, LANES), lambda b, p, off=off: (b, off + p, 0, 0))

    in_specs = []
    for _ in range(3):
        in_specs += [col(0), col(nq), col(2 * nq)]
    in_specs.append(pl.BlockSpec((1,) + tabs.shape[1:], lambda b, p: (p, 0, 0, 0, 0)))
    return pl.pallas_call(
        functools.partial(_dil_kernel, seq=S),
        grid=(B, npair),
        in_specs=in_specs,
        out_specs=pl.BlockSpec((1, None, S, LANES), lambda b, p: (b, p, 0, 0)),
        out_shape=jax.ShapeDtypeStruct((B, npair, S, LANES), F32),
        scratch_shapes=[pltpu.VMEM((3, 2, S, LANES), F32),
                        pltpu.VMEM((2, 3, 2 * DIL_BLOCK, 2 * DIL_BLOCK), F32)],
        compiler_params=pltpu.CompilerParams(dimension_semantics=("arbitrary", "arbitrary"),
                                             vmem_limit_bytes=VMEM_LIMIT),
        name="dilated",
    )(a1, a1, a1, a4, a4, a4, a16, a16, a16, tabs)


def _moba_kernel(q_ref, k_ref, v_ref, tab_ref, o_ref, vt_ext, s_scr, p_scr, g_scr, *, seq):
    bs = MOBA_BLOCK
    nb = seq // bs
    ntab = tab_ref.shape[2]

    vt = v_ref[0].astype(F32).T
    ones_rows = jnp.ones((vt_ext.shape[1] - HEAD_DIM, seq), BF16)
    for h in range(2):
        vt_ext[h, :HEAD_DIM, :] = vt[h * HEAD_DIM:(h + 1) * HEAD_DIM, :].astype(BF16)
        vt_ext[h, HEAD_DIM:, :] = ones_rows

    kmean = jnp.sum(k_ref[0].astype(F32).reshape(nb, bs, LANES), axis=1) * (1.0 / bs)
    km_hi = kmean.astype(BF16)
    km_lo = (kmean - km_hi.astype(F32)).astype(BF16)

    lane_q = lax.broadcasted_iota(jnp.int32, (bs, LANES), 1)
    head_mask = (lane_q < HEAD_DIM, lane_q >= HEAD_DIM)
    blk_id = lax.broadcasted_iota(jnp.int32, (nb, bs), 0)

    def score_block(i):
        q = q_ref[0, i * bs:(i + 1) * bs, :]
        n_keys = (i + 1) * bs
        for h in range(2):
            qh = jnp.where(head_mask[h], q, jnp.zeros_like(q))
            g_scr[i % 2, h] = _dot_nt(km_hi, qh) + _dot_nt(km_lo, qh)
            s_scr[i % 2, h, :n_keys, :] = _dot_nt(k_ref[0, :n_keys, :], qh)

    def attend_block(i):
        slot = i % 2
        n_keys = (i + 1) * bs
        outs = []
        for h in range(2):
            past = blk_id < i
            g = jnp.where(past, g_scr[slot, h], NEG_INF)
            allow = blk_id == i
            for _ in range(MOBA_TOPK):
                mx = jnp.max(g, axis=0, keepdims=True)
                first = jnp.min(jnp.where(g == mx, blk_id, nb), axis=0, keepdims=True)
                pick = blk_id == first
                allow = allow | (pick & past)
                g = jnp.where(pick, NEG_INF, g)
            blk_bias = jnp.where(allow, 0.0, NEG_INF)

            mx8 = None
            for jt in range(i + 1):
                rows = slice(jt * bs, (jt + 1) * bs)
                t = s_scr[slot, h, rows, :] + tab_ref[0, h, min(i - jt, ntab - 1)] + blk_bias[jt:jt + 1, :]
                s_scr[slot, h, rows, :] = t
                t8 = jnp.max(t.reshape(bs // 8, 8, bs), axis=0)
                mx8 = t8 if mx8 is None else jnp.maximum(mx8, t8)
            m = jnp.max(mx8, axis=0, keepdims=True)
            for jt in range(i + 1):
                rows = slice(jt * bs, (jt + 1) * bs)
                p_scr[h, rows, :] = jnp.exp(s_scr[slot, h, rows, :] - m).astype(BF16)
            ov = _dot(vt_ext[h, :, :n_keys], p_scr[h, :n_keys, :])
            outs.append(ov[:HEAD_DIM, :] / ov[HEAD_DIM:HEAD_DIM + 1, :])
        o_t = jnp.concatenate(outs, axis=0)
        o_ref[0, i * bs:(i + 1) * bs, :] = o_t.T

    score_block(0)
    for i in range(nb):
        if i + 1 < nb:
            score_block(i + 1)
        attend_block(i)


def _moba(b1, tabs):
    B, _, S, _ = b1.shape
    npair = N_HEADS_B // 2

    def col(off):
        return pl.BlockSpec((1, None, S, LANES), lambda b, p, off=off: (b, off + p, 0, 0))

    return pl.pallas_call(
        functools.partial(_moba_kernel, seq=S),
        grid=(B, npair),
        in_specs=[col(0), col(npair), col(2 * npair),
                  pl.BlockSpec((1,) + tabs.shape[1:], lambda b, p: (p, 0, 0, 0, 0))],
        out_specs=pl.BlockSpec((1, None, S, LANES), lambda b, p: (b, p, 0, 0)),
        out_shape=jax.ShapeDtypeStruct((B, npair, S, LANES), F32),
        scratch_shapes=[pltpu.VMEM((2, HEAD_DIM + BF16_SUBLANES, S), BF16),
                        pltpu.VMEM((2, 2, S, MOBA_BLOCK), F32),
                        pltpu.VMEM((2, S, MOBA_BLOCK), BF16),
                        pltpu.VMEM((2, 2, S // MOBA_BLOCK, MOBA_BLOCK), F32)],
        compiler_params=pltpu.CompilerParams(dimension_semantics=("arbitrary", "arbitrary"),
                                             vmem_limit_bytes=VMEM_LIMIT),
        name="moba",
    )(b1, b1, b1, tabs)


def _post_kernel(x_ref, oa_ref, ob_ref, kv_ref, ga_ref, gb_ref, wout_ref, gc_ref, wcq_ref, wco_ref,
                 gf2_ref, wgu_ref, wd_ref, gfin_ref, y_ref, *, d_ff, final_norm):
    d = x_ref.shape[2]
    oa = jnp.concatenate([oa_ref[0, c] for c in range(oa_ref.shape[1])], axis=1)
    ob = jnp.concatenate([ob_ref[0, c] for c in range(ob_ref.shape[1])], axis=1)
    ya = _rms(oa, ga_ref[...]).astype(BF16)
    yb = _rms(ob, gb_ref[...]).astype(BF16)
    x = x_ref[0] + _dot(jnp.concatenate([ya, yb], axis=1), wout_ref[...])

    h = _rms(x, gc_ref[...]).astype(BF16)
    q = _dot(h, wcq_ref[...])
    dh = d // CROSS_HEADS
    scale = dh ** -0.5
    scores = [_dot_nt((q[:, c * dh:(c + 1) * dh] * scale).astype(BF16), kv_ref[0, :, c * dh:(c + 1) * dh])
              for c in range(CROSS_HEADS)]
    heads = []
    for c, s in enumerate(scores):
        e = jnp.exp(s - jnp.max(s, axis=-1, keepdims=True))
        p = e / jnp.sum(e, axis=-1, keepdims=True)
        heads.append(_dot(p.astype(BF16), kv_ref[0, :, d + c * dh:d + (c + 1) * dh]).astype(BF16))
    x = x + _dot(jnp.concatenate(heads, axis=1), wco_ref[...])

    h = _rms(x, gf2_ref[...]).astype(BF16)
    x = x + 0.5 * _swiglu_half(h, wgu_ref, wd_ref, d_ff)
    y_ref[0] = _rms(x, gfin_ref[...]) if final_norm else x


def _post(x1, oa, ob, kv, g_out_a, g_out_b, w_out, g_cross, w_cq, w_co, g_ffn2, wgu, wd, g_final, final_norm):
    B, S, D = x1.shape
    ta, tb = oa.shape[1], ob.shape[1]
    wa, wb = ta * LANES, tb * LANES
    M = kv.shape[1]
    d_ff = wd.shape[0]
    tm = TOKEN_TILE
    tok = lambda w: pl.BlockSpec((1, tm, w), lambda b, i: (b, i, 0))
    slab = lambda t: pl.BlockSpec((1, t, tm, LANES), lambda b, i: (b, 0, i, 0))
    return pl.pallas_call(
        functools.partial(_post_kernel, d_ff=d_ff, final_norm=final_norm),
        grid=(B, S // tm),
        in_specs=[tok(D), slab(ta), slab(tb),
                  pl.BlockSpec((1, M, 2 * D), lambda b, i: (b, 0, 0)),
                  _resident((1, wa)), _resident((1, wb)), _resident((wa + wb, D)),
                  _resident((1, D)), _resident((D, D)), _resident((D, D)),
                  _resident((1, D)), _resident((D, 2 * d_ff)), _resident((d_ff, D)),
                  _resident((1, D))],
        out_specs=tok(D),
        out_shape=jax.ShapeDtypeStruct((B, S, D), F32),
        compiler_params=pltpu.CompilerParams(dimension_semantics=("arbitrary", "arbitrary"),
                                             vmem_limit_bytes=VMEM_LIMIT),
        name="post",
    )(x1, oa, ob, kv, g_out_a.reshape(1, wa), g_out_b.reshape(1, wb), w_out, g_cross.reshape(1, D),
      w_cq, w_co, g_ffn2.reshape(1, D), wgu, wd, g_final.reshape(1, D))


def kernel(x, mem, g_ffn1, w_ffn1_gu, w_ffn1_down, g_mix, w_in, rel_bias, g_out_a, g_out_b, w_out,
           g_cross, g_mem, w_cq, w_ckv, w_co, g_ffn2, w_ffn2_gu, w_ffn2_down, g_final):
    B, S, D = x.shape
    depth = g_ffn1.shape[0]
    dil_tabs = _dilated_tables(rel_bias)
    moba_tabs = _moba_tables(rel_bias)
    bf = lambda w: w.astype(BF16)
    wqa, wka = N_HEADS_A * HEAD_DIM, 3 * N_HEADS_A * HEAD_DIM
    col = jnp.arange(w_in.shape[-1])
    is_q = (col < wqa) | ((col >= wka) & (col < wka + N_HEADS_B * HEAD_DIM))
    q_scale = jnp.where(is_q, HEAD_DIM ** -0.5, 1.0).astype(F32)
    y = None
    for l in range(depth):
        kv = _memkv(mem, g_mem[l], bf(w_ckv[l]))
        x1 = _ffn1(x.reshape(B * S, D), g_ffn1[l], bf(w_ffn1_gu[l]), bf(w_ffn1_down[l])).reshape(B, S, D)
        a1, a4, a16, b1 = _proj(x1, g_mix[l], bf(w_in[l] * q_scale))
        oa = _dilated(a1, a4, a16, dil_tabs)
        ob = _moba(b1, moba_tabs)
        y = _post(x1, oa, ob, kv, g_out_a[l], g_out_b[l], bf(w_out[l]), g_cross[l], bf(w_cq[l]),
                  bf(w_co[l]), g_ffn2[l], bf(w_ffn2_gu[l]), bf(w_ffn2_down[l]), g_final,
                  final_norm=(l == depth - 1))
        x = y
    return y
```
